```python
import math
import jax, jax.numpy as jnp
from jax import lax
import numpy as np

D_MODEL = 1024
BATCH = 4
SEQ = 4096
DEPTH = 4
DEC_BATCH = 128
DEC_SEQ = 1
PAST_LEN = 8192
PAGE_SIZE = 128

MIX_WIDTH = D_MODEL
RWKV_WIDTH = MIX_WIDTH // 2
SWA_WIDTH = MIX_WIDTH - RWKV_WIDTH
HEAD_DIM = 64
RWKV_HEADS = RWKV_WIDTH // HEAD_DIM
SWA_HEADS = SWA_WIDTH // HEAD_DIM
SWA_KV_HEADS = 2
SWA_GROUP = SWA_HEADS // SWA_KV_HEADS
KV_WIDTH = SWA_KV_HEADS * HEAD_DIM
WINDOW = 128
Q_BLOCK = WINDOW
DECAY_LORA = 64
ICLR_LORA = 64
GATE_LORA = 128
RWKV_SPLITS = [RWKV_WIDTH, 2 * RWKV_WIDTH, 3 * RWKV_WIDTH, 3 * RWKV_WIDTH + DECAY_LORA,
               3 * RWKV_WIDTH + DECAY_LORA + ICLR_LORA]
RWKV_COLS = 3 * RWKV_WIDTH + DECAY_LORA + ICLR_LORA + GATE_LORA
IN_COLS = RWKV_COLS + SWA_WIDTH + 2 * KV_WIDTH
IN_SPLITS = [RWKV_COLS, RWKV_COLS + SWA_WIDTH, RWKV_COLS + SWA_WIDTH + KV_WIDTH]
LN_X_EPS = 64e-5
RMS_EPS = 1e-6
N_BUCKETS = 32
MAX_DISTANCE = 128
D_FF_DENSE = 2816
D_FF_EXPERT = 3584
N_EXPERTS = 8
TOP_K = 2
NEG_INF = -1e30

kernel_name = 'hybrid_rwkv7_swa_sink_decoder_step'


def rms_norm(x, g):
    xf = x.astype(jnp.float32)
    y = xf * lax.rsqrt(jnp.mean(xf * xf, axis=-1, keepdims=True) + RMS_EPS)
    return (y * g.astype(jnp.float32)).astype(x.dtype)


def ada_mod(c, w, b):
    m = jax.nn.silu(c) @ w + b
    sh, sc, gt = jnp.split(m[:, None, :], 3, axis=-1)
    return sh, sc, gt


def rel_bucket(dist):
    max_exact = N_BUCKETS // 2
    d = jnp.maximum(dist, 0)
    ratio = jnp.log(jnp.maximum(d, 1).astype(jnp.float32) / max_exact) / math.log(MAX_DISTANCE / max_exact)
    large = jnp.minimum(max_exact + (ratio * (N_BUCKETS - max_exact)).astype(jnp.int32), N_BUCKETS - 1)
    return jnp.where(d < max_exact, d, large)


def rel_bias_for(dist, rel_bias):
    b = rel_bias[rel_bucket(dist)].astype(jnp.float32)
    b = jnp.moveaxis(b, -1, 0)
    return b.reshape(SWA_KV_HEADS, SWA_GROUP, *dist.shape)


def sink_softmax(s, valid, sink):
    s = jnp.where(valid, s, NEG_INF)
    sk = jnp.broadcast_to(sink.astype(jnp.float32).reshape(SWA_KV_HEADS, SWA_GROUP, 1, 1), s.shape[:-1] + (1,))
    p = jax.nn.softmax(jnp.concatenate([s, sk], axis=-1), axis=-1)
    return p[..., :-1]


def swa_prompt(q, k, v, sink, rel_bias):
    B, S = q.shape[:2]
    nb = S // Q_BLOCK
    qb = q.reshape(B, nb, Q_BLOCK, SWA_KV_HEADS, SWA_GROUP, HEAD_DIM)

    def band(t):
        tp = jnp.concatenate([jnp.zeros_like(t[:, :Q_BLOCK]), t], axis=1)
        tb = tp.reshape(B, nb + 1, Q_BLOCK, SWA_KV_HEADS, HEAD_DIM)
        return jnp.concatenate([tb[:, :-1], tb[:, 1:]], axis=2)

    kw, vw = band(k), band(v)
    qi = jnp.arange(Q_BLOCK)[:, None]
    kj = jnp.arange(2 * Q_BLOCK)[None, :]
    dist = qi + Q_BLOCK - kj
    key_pos = jnp.arange(nb)[:, None, None] * Q_BLOCK - Q_BLOCK + kj[None]
    valid = ((dist >= 0) & (dist <= WINDOW))[None] & (key_pos >= 0)
    s = jnp.einsum('bnqhgd,bnkhd->bnhgqk', qb, kw).astype(jnp.float32) * (HEAD_DIM ** -0.5)
    s = s + rel_bias_for(dist, rel_bias)
    p = sink_softmax(s, valid[:, None, None], sink)
    o = jnp.einsum('bnhgqk,bnkhd->bnqhgd', p.astype(vw.dtype), vw)
    return o.reshape(B, S, SWA_WIDTH)


def swa_sample(q, k, v, kbuf, vbuf, sink, rel_bias):
    B, T = q.shape[:2]
    W = kbuf.shape[1]
    kall = jnp.concatenate([kbuf.astype(k.dtype), k], axis=1)
    vall = jnp.concatenate([vbuf.astype(v.dtype), v], axis=1)
    dist = (W + jnp.arange(T))[:, None] - jnp.arange(W + T)[None, :]
    valid = (dist >= 0) & (dist <= WINDOW)
    qg = q.reshape(B, T, SWA_KV_HEADS, SWA_GROUP, HEAD_DIM)
    s = jnp.einsum('bqhgd,bkhd->bhgqk', qg, kall).astype(jnp.float32) * (HEAD_DIM ** -0.5)
    s = s + rel_bias_for(dist, rel_bias)
    p = sink_softmax(s, valid, sink)
    o = jnp.einsum('bhgqk,bkhd->bqhgd', p.astype(vall.dtype), vall).reshape(B, T, SWA_WIDTH)
    return o, kall[:, -W:], vall[:, -W:]


def wkv_scan(r, decay, k, v, a, b, s0):
    def step(S, inp):
        rt, wt, kt, vt, at, bt = inp
        sa = jnp.einsum('bhvk,bhk->bhv', S, at)
        S = S * wt[:, :, None, :] + sa[..., None] * bt[:, :, None, :] + vt[..., None] * kt[:, :, None, :]
        return S, jnp.einsum('bhvk,bhk->bhv', S, rt)
    xs = tuple(jnp.swapaxes(t, 0, 1) for t in (r, decay, k, v, a, b))
    S, y = lax.scan(step, s0, xs)
    return jnp.swapaxes(y, 0, 1), S


def rwkv_branch(pr, prev, s0, mu, w_decay_up, decay_base, w_iclr_up, iclr_base, w_gate_up,
                k_k, k_a, r_k, lnx_w, lnx_b):
    f32 = jnp.float32
    B, T = pr.shape[:2]
    shifted = jnp.concatenate([prev[:, None].astype(pr.dtype), pr[:, :-1]], axis=1)
    xs = pr + (shifted - pr) * mu
    r, k, v, wl, al, gl = jnp.split(xs, RWKV_SPLITS, axis=-1)
    logw = -jax.nn.softplus(-(decay_base + jnp.tanh(wl) @ w_decay_up).astype(f32)) - 0.5
    decay = jnp.exp(-jnp.exp(logw))
    iclr = jax.nn.sigmoid((iclr_base + al @ w_iclr_up).astype(f32))
    g = jax.nn.sigmoid(gl) @ w_gate_up
    heads = lambda t: t.astype(f32).reshape(B, T, RWKV_HEADS, HEAD_DIM)
    kk = heads(k * k_k)
    kk = kk / jnp.maximum(jnp.sqrt(jnp.sum(kk * kk, axis=-1, keepdims=True)), 1e-12)
    kh = heads(k.astype(f32) * (1.0 + (iclr - 1.0) * k_a.astype(f32)))
    rh, vh = heads(r), heads(v)
    y, S = wkv_scan(rh, heads(decay), kh, vh, -kk, kk * heads(iclr), s0.astype(f32))
    mean = jnp.mean(y, axis=-1, keepdims=True)
    var = jnp.mean(jnp.square(y - mean), axis=-1, keepdims=True)
    y = ((y - mean) * lax.rsqrt(var + LN_X_EPS)).reshape(B, T, RWKV_WIDTH)
    y = y * lnx_w.astype(f32) + lnx_b.astype(f32)
    bonus = jnp.sum(rh * kh * r_k.astype(f32), axis=-1, keepdims=True) * vh
    y = y + bonus.reshape(B, T, RWKV_WIDTH)
    return (y * g.astype(f32)).astype(pr.dtype), S.astype(s0.dtype), pr[:, -1]


def swiglu(h, wg, wu, wd):
    return (jax.nn.silu(h @ wg) * (h @ wu)) @ wd


def moe_ffn(h, router_w, router_b, wg, wu, wd):
    B, T, D = h.shape
    t = h.reshape(B * T, D)
    logits = (t @ router_w).astype(jnp.float32) + router_b.astype(jnp.float32)
    top_v, top_i = lax.top_k(logits, TOP_K)
    top_p = jax.nn.softmax(top_v, axis=-1)
    combine = jnp.sum(jax.nn.one_hot(top_i, N_EXPERTS, dtype=jnp.float32) * top_p[..., None], axis=1)
    out = jnp.zeros_like(t)
    for e in range(N_EXPERTS):
        out = out + combine[:, e:e + 1].astype(t.dtype) * swiglu(t, wg[e], wu[e], wd[e])
    return out.reshape(B, T, D)


def trunk(x, c, wkv_in, shift_in, kbuf_in, vbuf_in, p, is_prompt):
    B, T, _ = x.shape
    wkv_out, shift_out, k_out, v_out = [], [], [], []
    for l in range(DEPTH):
        sh, sc, gt = ada_mod(c, p['ada_w'][l, 0], p['ada_b'][l, 0])
        h = rms_norm(x, p['norm_pre'][l, 0]) * (1 + sc) + sh
        proj = h @ p['w_in'][l]
        pr, q, k, v = jnp.split(proj, IN_SPLITS, axis=-1)
        q = q.reshape(B, T, SWA_HEADS, HEAD_DIM)
        k = k.reshape(B, T, SWA_KV_HEADS, HEAD_DIM)
        v = v.reshape(B, T, SWA_KV_HEADS, HEAD_DIM)
        if is_prompt:
            prev = jnp.zeros((B, RWKV_COLS), pr.dtype)
            s0 = jnp.zeros((B, RWKV_HEADS, HEAD_DIM, HEAD_DIM), jnp.float32)
        else:
            prev, s0 = shift_in[l], wkv_in[l]
        y_r, s_new, last = rwkv_branch(pr, prev, s0, p['mu_shift'][l], p['w_decay_up'][l], p['decay_base'][l],
                                       p['w_iclr_up'][l], p['iclr_base'][l], p['w_gate_up'][l],
                                       p['k_k'][l], p['k_a'][l], p['r_k'][l], p['lnx_w'][l], p['lnx_b'][l])
        if is_prompt:
            y_a = swa_prompt(q, k, v, p['attn_sinks'][l], p['rel_bias'])
            kb, vb = k[:, -WINDOW:], v[:, -WINDOW:]
        else:
            y_a, kb, vb = swa_sample(q, k, v, kbuf_in[l], vbuf_in[l], p['attn_sinks'][l], p['rel_bias'])
        mix = jnp.concatenate([y_r, y_a.astype(y_r.dtype)], axis=-1) @ p['w_out'][l]
        x = x + gt * rms_norm(mix, p['norm_post'][l, 0])
        sh, sc, gt = ada_mod(c, p['ada_w'][l, 1], p['ada_b'][l, 1])
        h = rms_norm(x, p['norm_pre'][l, 1]) * (1 + sc) + sh
        if l % 2 == 0:
            i = l // 2
            f = swiglu(h, p['ffn_w_gate'][i], p['ffn_w_up'][i], p['ffn_w_down'][i])
        else:
            i = l // 2
            f = moe_ffn(h, p['router_w'][i], p['router_b'][i], p['moe_w_gate'][i], p['moe_w_up'][i],
                        p['moe_w_down'][i])
        x = x + gt * rms_norm(f, p['norm_post'][l, 1])
        wkv_out.append(s_new)
        shift_out.append(last)
        k_out.append(kb)
        v_out.append(vb)
    return x, jnp.stack(wkv_out), jnp.stack(shift_out), jnp.stack(k_out), jnp.stack(v_out)


def setup_inputs(seed: int = 0) -> dict:
    key = jax.random.key(seed)
    ks = jax.random.split(key, 40)
    cnt = [0]

    def nxt():
        kk = ks[cnt[0]]
        cnt[0] += 1
        return kk

    def nrm(shape, s):
        return jax.random.normal(nxt(), shape, jnp.float32) * s

    def uni(shape, lo, hi):
        return jax.random.uniform(nxt(), shape, jnp.float32, minval=lo, maxval=hi)

    D = D_MODEL
    n_dense = (DEPTH + 1) // 2
    n_moe = DEPTH // 2
    buf = min(WINDOW, PAST_LEN)
    return {
        'x_prompt': nrm((BATCH, SEQ, D), 1.0),
        'x_sample': nrm((DEC_BATCH, DEC_SEQ, D), 1.0),
        'c_prompt': nrm((BATCH, D), 1.0),
        'c_sample': nrm((DEC_BATCH, D), 1.0),
        'state_wkv': nrm((DEPTH, DEC_BATCH, RWKV_HEADS, HEAD_DIM, HEAD_DIM), 0.5),
        'state_shift': nrm((DEPTH, DEC_BATCH, RWKV_COLS), 1.0),
        'cache_swa_k': nrm((DEPTH, DEC_BATCH, buf, SWA_KV_HEADS, HEAD_DIM), 1.0),
        'cache_swa_v': nrm((DEPTH, DEC_BATCH, buf, SWA_KV_HEADS, HEAD_DIM), 1.0),
        'rel_bias': nrm((N_BUCKETS, SWA_HEADS), 0.5),
        'ada_w': nrm((DEPTH, 2, D, 3 * D), 0.5 * D ** -0.5),
        'ada_b': nrm((DEPTH, 2, 3 * D), 0.02),
        'norm_pre': 1.0 + nrm((DEPTH, 2, D), 0.02),
        'norm_post': 1.0 + nrm((DEPTH, 2, D), 0.02),
        'w_in': nrm((DEPTH, D, IN_COLS), D ** -0.5),
        'mu_shift': uni((DEPTH, RWKV_COLS), 0.0, 1.0),
        'w_decay_up': nrm((DEPTH, DECAY_LORA, RWKV_WIDTH), 0.5 * DECAY_LORA ** -0.5),
        'decay_base': uni((DEPTH, RWKV_WIDTH), -6.5, -1.5),
        'w_iclr_up': nrm((DEPTH, ICLR_LORA, RWKV_WIDTH), 0.5 * ICLR_LORA ** -0.5),
        'iclr_base': nrm((DEPTH, RWKV_WIDTH), 0.1),
        'w_gate_up': nrm((DEPTH, GATE_LORA, RWKV_WIDTH), GATE_LORA ** -0.5),
        'k_k': 0.85 + nrm((DEPTH, RWKV_WIDTH), 0.02),
        'k_a': 1.0 + nrm((DEPTH, RWKV_WIDTH), 0.02),
        'r_k': nrm((DEPTH, RWKV_HEADS, HEAD_DIM), 0.1),
        'lnx_w': 1.0 + nrm((DEPTH, RWKV_WIDTH), 0.02),
        'lnx_b': nrm((DEPTH, RWKV_WIDTH), 0.02),
        'attn_sinks': nrm((DEPTH, SWA_HEADS), 0.5),
        'w_out': nrm((DEPTH, MIX_WIDTH, D), MIX_WIDTH ** -0.5),
        'ffn_w_gate': nrm((n_dense, D, D_FF_DENSE), D ** -0.5),
        'ffn_w_up': nrm((n_dense, D, D_FF_DENSE), D ** -0.5),
        'ffn_w_down': nrm((n_dense, D_FF_DENSE, D), D_FF_DENSE ** -0.5),
        'router_w': nrm((n_moe, D, N_EXPERTS), D ** -0.5),
        'router_b': nrm((n_moe, N_EXPERTS), 0.01),
        'moe_w_gate': nrm((n_moe, N_EXPERTS, D, D_FF_EXPERT), D ** -0.5),
        'moe_w_up': nrm((n_moe, N_EXPERTS, D, D_FF_EXPERT), D ** -0.5),
        'moe_w_down': nrm((n_moe, N_EXPERTS, D_FF_EXPERT, D), D_FF_EXPERT ** -0.5),
    }


def reference(x_prompt, x_sample, c_prompt, c_sample, state_wkv, state_shift, cache_swa_k, cache_swa_v,
              rel_bias, ada_w, ada_b, norm_pre, norm_post, w_in, mu_shift, w_decay_up, decay_base,
              w_iclr_up, iclr_base, w_gate_up, k_k, k_a, r_k, lnx_w, lnx_b, attn_sinks, w_out,
              ffn_w_gate, ffn_w_up, ffn_w_down, router_w, router_b, moe_w_gate, moe_w_up, moe_w_down):
    p = {'rel_bias': rel_bias, 'ada_w': ada_w, 'ada_b': ada_b, 'norm_pre': norm_pre, 'norm_post': norm_post,
         'w_in': w_in, 'mu_shift': mu_shift, 'w_decay_up': w_decay_up, 'decay_base': decay_base,
         'w_iclr_up': w_iclr_up, 'iclr_base': iclr_base, 'w_gate_up': w_gate_up, 'k_k': k_k, 'k_a': k_a,
         'r_k': r_k, 'lnx_w': lnx_w, 'lnx_b': lnx_b, 'attn_sinks': attn_sinks, 'w_out': w_out,
         'ffn_w_gate': ffn_w_gate, 'ffn_w_up': ffn_w_up, 'ffn_w_down': ffn_w_down,
         'router_w': router_w, 'router_b': router_b, 'moe_w_gate': moe_w_gate, 'moe_w_up': moe_w_up,
         'moe_w_down': moe_w_down}
    y_prompt, wkv_p, shift_p, k_p, v_p = trunk(x_prompt, c_prompt, None, None, None, None, p, True)
    y_sample, wkv_s, shift_s, k_s, v_s = trunk(x_sample, c_sample, state_wkv, state_shift,
                                               cache_swa_k, cache_swa_v, p, False)
    return (y_prompt, y_sample, wkv_p, shift_p, k_p, v_p, wkv_s, shift_s, k_s, v_s)
```

```python
import functools

import numpy as np
import jax
import jax.numpy as jnp
from jax import lax
from jax.experimental import pallas as pl
from jax.experimental.pallas import tpu as pltpu

F32 = jnp.float32
BF16 = jnp.bfloat16
HIGHEST = lax.Precision.HIGHEST

D_MODEL = 1024
HEAD_DIM = 64
RWKV_WIDTH = 512
RWKV_HEADS = RWKV_WIDTH // HEAD_DIM
SWA_WIDTH = 512
SWA_HEADS = SWA_WIDTH // HEAD_DIM
SWA_KV_HEADS = 2
SWA_GROUP = SWA_HEADS // SWA_KV_HEADS
KV_WIDTH = SWA_KV_HEADS * HEAD_DIM
WINDOW = 128
DECAY_LORA = 64
ICLR_LORA = 64
GATE_LORA = 128
RWKV_COLS = 3 * RWKV_WIDTH + DECAY_LORA + ICLR_LORA + GATE_LORA
IN_COLS = RWKV_COLS + SWA_WIDTH + 2 * KV_WIDTH
LORA_OFF = 3 * RWKV_WIDTH
GATE_OFF = LORA_OFF + DECAY_LORA + ICLR_LORA
LN_X_EPS = 64e-5
RMS_EPS = 1e-6
N_BUCKETS = 32
MAX_DISTANCE = 128
N_EXPERTS = 8
NEG_INF = -1e30

LANES = 128
SUBLANES = 8
VMEM_LIMIT = 56 * 1024 * 1024

WKV_CHUNK = 64
PAIR = 2 * HEAD_DIM

NN = (((1,), (0,)), ((), ()))
NT = (((1,), (1,)), ((), ()))
TN = (((0,), (0,)), ((), ()))


def _dot(a, b, dims=NN):
    return lax.dot_general(a.astype(BF16), b.astype(BF16), dims, preferred_element_type=F32)


def _dot_hi(a, b, dims=NN):
    return lax.dot_general(a.astype(F32), b.astype(F32), dims, precision=HIGHEST, preferred_element_type=F32)


def _sigmoid(x):
    return 1.0 / (1.0 + jnp.exp(-x))


def _params(sem):
    return pltpu.CompilerParams(dimension_semantics=sem, vmem_limit_bytes=VMEM_LIMIT)


def _rms(x, g):
    return x * lax.rsqrt(jnp.mean(x * x, axis=-1, keepdims=True) + RMS_EPS) * g


def _mod_rows(mod_ref, per_row, tiles_per_seq, prompt_row0):
    if per_row:
        return mod_ref[...]
    b = pl.program_id(0) // tiles_per_seq
    return mod_ref[pl.ds(prompt_row0 + b, 1), :]


def _ada_body(c_ref, w_ref, b_ref, o_ref):
    c = c_ref[...]
    o_ref[...] = _dot(c * _sigmoid(c), w_ref[...]) + b_ref[...]


def _ada_all(c_all, ada_w, ada_b):
    R, D = c_all.shape
    n = ada_w.shape[0]
    tn = 1024
    return pl.pallas_call(
        _ada_body,
        out_shape=jax.ShapeDtypeStruct((n, R, 3 * D), F32),
        grid=(n, 3 * D // tn),
        in_specs=[pl.BlockSpec((R, D), lambda i, j: (0, 0)),
                  pl.BlockSpec((None, D, tn), lambda i, j: (i, 0, j)),
                  pl.BlockSpec((None, 1, tn), lambda i, j: (i, 0, j))],
        out_specs=pl.BlockSpec((None, R, tn), lambda i, j: (i, 0, j)),
        compiler_params=_params(("arbitrary", "arbitrary")),
        name="ada_mod",
    )(c_all, ada_w, ada_b)


def _inproj_body(per_row, tiles_per_seq, prompt_row0, x_ref, mod_ref, g_ref, w_ref, pr_ref, q_ref, k_ref, v_ref):
    D = D_MODEL
    m = _mod_rows(mod_ref, per_row, tiles_per_seq, prompt_row0)
    h = _rms(x_ref[...], g_ref[...]) * (1.0 + m[:, D:2 * D]) + m[:, 0:D]
    proj = _dot(h, w_ref[...])
    pr_ref[...] = proj[:, 0:RWKV_COLS]
    q_ref[...] = proj[:, RWKV_COLS:RWKV_COLS + SWA_WIDTH]
    k_ref[...] = proj[:, RWKV_COLS + SWA_WIDTH:RWKV_COLS + SWA_WIDTH + KV_WIDTH]
    v_ref[...] = proj[:, RWKV_COLS + SWA_WIDTH + KV_WIDTH:IN_COLS]


def _mod_spec(mod, per_row, tm, nargs):
    R = mod.shape[0]
    if per_row:
        return pl.BlockSpec((tm, 3 * D_MODEL), lambda i, *_: (0, 0))
    return pl.BlockSpec((R, 3 * D_MODEL), lambda i, *_: (0, 0))


def _inproj(x, mod, g, w, per_row, tm, tiles_per_seq, prompt_row0):
    N, D = x.shape
    row = lambda i: (i, 0)
    fixed = lambda i: (0, 0)
    return pl.pallas_call(
        functools.partial(_inproj_body, per_row, tiles_per_seq, prompt_row0),
        out_shape=(jax.ShapeDtypeStruct((N, RWKV_COLS), F32), jax.ShapeDtypeStruct((N, SWA_WIDTH), F32),
                   jax.ShapeDtypeStruct((N, KV_WIDTH), F32), jax.ShapeDtypeStruct((N, KV_WIDTH), F32)),
        grid=(N // tm,),
        in_specs=[pl.BlockSpec((tm, D), row), _mod_spec(mod, per_row, tm, 1),
                  pl.BlockSpec((1, D), fixed), pl.BlockSpec((D, IN_COLS), fixed)],
        out_specs=(pl.BlockSpec((tm, RWKV_COLS), row), pl.BlockSpec((tm, SWA_WIDTH), row),
                   pl.BlockSpec((tm, KV_WIDTH), row), pl.BlockSpec((tm, KV_WIDTH), row)),
        compiler_params=_params(("arbitrary",)),
        name="in_proj",
    )(x, mod, g, w)


def _prep_body(per_row, tiles_per_seq, pr_ref, prev_ref, mu_ref, wd_ref, wi_ref, wg_ref, dbase_ref, ibase_ref,
               kk_ref, ka_ref, rk_ref, hs_ref,
               r_ref, ld_ref, k_ref, v_ref, a_ref, b_ref, g_ref, bonus_ref):
    W = RWKV_WIDTH
    pr = pr_ref[...]
    if per_row:
        shifted = prev_ref[...]
    else:
        first = (pl.program_id(0) % tiles_per_seq) == 0
        carry = jnp.where(first, 0.0, prev_ref[SUBLANES - 1:SUBLANES, :])
        rows = lax.broadcasted_iota(jnp.int32, pr.shape, 0)
        shifted = jnp.where(rows == 0, carry, pltpu.roll(pr, 1, 0))
    xs = pr + (shifted - pr) * mu_ref[...]
    r = xs[:, 0:W]
    k = xs[:, W:2 * W]
    v = xs[:, 2 * W:3 * W]
    lora = xs[:, LORA_OFF:GATE_OFF]
    gl = xs[:, GATE_OFF:RWKV_COLS]
    z = dbase_ref[...] + _dot_hi(jnp.tanh(lora), wd_ref[...])
    ld = -float(np.exp(-0.5)) * _sigmoid(z)
    iclr = _sigmoid(ibase_ref[...] + _dot_hi(lora, wi_ref[...]))
    g = _dot_hi(_sigmoid(gl), wg_ref[...])
    hs = hs_ref[...]
    kk = k * kk_ref[...]
    kk = kk / jnp.maximum(jnp.sqrt(_dot_hi(kk * kk, hs)), 1e-12)
    kh = k * (1.0 + (iclr - 1.0) * ka_ref[...])
    bonus = _dot_hi(r * kh * rk_ref[...], hs) * v
    r_ref[...] = r
    ld_ref[...] = ld
    k_ref[...] = kh
    v_ref[...] = v
    a_ref[...] = -kk
    b_ref[...] = kk * iclr
    g_ref[...] = g
    bonus_ref[...] = bonus


def _prep(pr, prev, lw, per_row, tm, tiles_per_seq):
    N = pr.shape[0]
    W = RWKV_WIDTH
    row = lambda i: (i, 0)
    fixed = lambda i: (0, 0)
    if per_row:
        prev_spec = pl.BlockSpec((tm, RWKV_COLS), row)
    else:
        per = tm // SUBLANES
        prev_spec = pl.BlockSpec((SUBLANES, RWKV_COLS), lambda i: (jnp.maximum(i * per - 1, 0), 0))
    vec = pl.BlockSpec((1, W), fixed)
    out = jax.ShapeDtypeStruct((N, W), F32)
    return pl.pallas_call(
        functools.partial(_prep_body, per_row, tiles_per_seq),
        out_shape=(out,) * 8,
        grid=(N // tm,),
        in_specs=[pl.BlockSpec((tm, RWKV_COLS), row), prev_spec, pl.BlockSpec((1, RWKV_COLS), fixed),
                  pl.BlockSpec((LANES, W), fixed), pl.BlockSpec((LANES, W), fixed), pl.BlockSpec((GATE_LORA, W), fixed),
                  vec, vec, vec, vec, vec, pl.BlockSpec((W, W), fixed)],
        out_specs=(pl.BlockSpec((tm, W), row),) * 8,
        compiler_params=_params(("arbitrary",)),
        name="rwkv_prep",
    )(pr, prev, lw["mu"], lw["wd"], lw["wi"], lw["wg"], lw["dbase"], lw["ibase"], lw["kk"], lw["ka"], lw["rk"],
      lw["headsum"])


def _stack_heads(x, lane_head0):
    return jnp.concatenate([jnp.where(lane_head0, x, 0.0), jnp.where(lane_head0, 0.0, x)], axis=0)


def _fold_heads(x):
    c = x.shape[0] // 2
    return x[0:c] + x[c:2 * c]


def _unit_lower_inverse(n, same16, eye):
    nd = jnp.where(same16, n, 0.0)
    no = n - nd
    n2 = _dot(nd, nd)
    n4 = _dot(n2, n2)
    n8 = _dot(n4, n4)
    td = eye + nd
    td = td + _dot(td, n2)
    td = td + _dot(td, n4)
    td = td + _dot(td, n8)
    q = _dot(td, no)
    q2 = _dot(q, q)
    z = eye + q
    z = z + _dot(z, q2)
    return _dot(z, td)


def _wkv_chunk_body(chunks, r_ref, ld_ref, k_ref, v_ref, a_ref, b_ref, y_ref, s_ref, h_ref):
    C = WKV_CHUNK
    n_pairs = RWKV_WIDTH // PAIR

    @pl.when(pl.program_id(1) == 0)
    def _():
        h_ref[...] = jnp.zeros_like(h_ref)

    ri = lax.broadcasted_iota(jnp.int32, (PAIR, PAIR), 0)
    ci = lax.broadcasted_iota(jnp.int32, (PAIR, PAIR), 1)
    same_head = (ri // C) == (ci // C)
    strict_lower = same_head & (ci < ri)
    incl_lower = same_head & (ci <= ri)
    same16 = (ri // 16) == (ci // 16)
    eye_b = ri == ci
    eye = jnp.where(eye_b, 1.0, 0.0)
    tri = jnp.where(lax.broadcasted_iota(jnp.int32, (C, C), 1) <= lax.broadcasted_iota(jnp.int32, (C, C), 0), 1.0, 0.0)
    lane_head0 = lax.broadcasted_iota(jnp.int32, (C, PAIR), 1) < HEAD_DIM
    zeros = jnp.zeros((PAIR, PAIR), F32)

    for c in range(chunks):
        rows = slice(c * C, (c + 1) * C)
        ld = ld_ref[rows, :]
        cum = _dot_hi(tri, ld)
        last = cum[C - 1:C, :]
        p_incl = jnp.exp(cum)
        p_excl = jnp.exp(cum - ld)
        p_inv = jnp.exp(-cum)
        p_tail = jnp.exp(last - cum)
        p_all = jnp.exp(last)
        a_t = a_ref[rows, :] * p_excl
        r_t = r_ref[rows, :] * p_incl
        b_raw = b_ref[rows, :]
        k_raw = k_ref[rows, :]
        b_t = b_raw * p_inv
        k_t = k_raw * p_inv
        b_h = b_raw * p_tail
        k_h = k_raw * p_tail
        v_all = v_ref[rows, :]
        for j in range(n_pairs):
            lanes = slice(j * PAIR, (j + 1) * PAIR)
            xa = _stack_heads(a_t[:, lanes], lane_head0)
            xr = _stack_heads(r_t[:, lanes], lane_head0)
            v_st = _stack_heads(v_all[:, lanes], lane_head0)
            bh_st = _stack_heads(b_h[:, lanes], lane_head0)
            kh_st = _stack_heads(k_h[:, lanes], lane_head0)
            yb = jnp.concatenate([b_t[:, lanes], b_t[:, lanes]], axis=0)
            yk = jnp.concatenate([k_t[:, lanes], k_t[:, lanes]], axis=0)
            gram = _dot(jnp.concatenate([xa, xr], axis=0), jnp.concatenate([yb, yk], axis=0), NT)
            n_mat = jnp.where(strict_lower, gram[0:PAIR, 0:PAIR], 0.0)
            m_mat = jnp.where(strict_lower, gram[0:PAIR, PAIR:2 * PAIR], 0.0)
            a_rb = jnp.where(incl_lower, gram[PAIR:2 * PAIR, 0:PAIR], 0.0)
            a_rk = jnp.where(incl_lower, gram[PAIR:2 * PAIR, PAIR:2 * PAIR], 0.0)
            t_inv = _unit_lower_inverse(n_mat, same16, eye)
            mv = _dot(m_mat, v_st)
            tx = _dot(t_inv, jnp.concatenate([xa, mv], axis=1))
            rhs = jnp.concatenate([tx, jnp.concatenate([zeros, v_st], axis=1)], axis=0)
            ry = _dot(jnp.concatenate([a_rb, a_rk], axis=1), rhs)
            r_bar = _fold_heads(xr + ry[:, 0:PAIR])
            y0 = _fold_heads(ry[:, PAIR:2 * PAIR])
            pp = _dot(jnp.concatenate([bh_st, kh_st], axis=0), rhs, TN)
            phi = pp[:, 0:PAIR] + jnp.where(eye_b, p_all[:, lanes], 0.0)
            psi = pp[:, PAIR:2 * PAIR]
            h0 = h_ref[j]
            y_ref[rows, lanes] = _dot(r_bar, h0) + y0
            h_ref[j] = _dot(phi, h0) + psi

    @pl.when(pl.program_id(1) == pl.num_programs(1) - 1)
    def _():
        s_ref[...] = h_ref[...]


def _wkv_prompt(r, ld, k, v, a, b, batch, seq):
    N, W = r.shape
    chunks = 2
    tt = chunks * WKV_CHUNK
    steps = seq // tt
    n_pairs = W // PAIR
    row = lambda bb, t: (bb * steps + t, 0)
    spec = pl.BlockSpec((tt, W), row)
    return pl.pallas_call(
        functools.partial(_wkv_chunk_body, chunks),
        out_shape=(jax.ShapeDtypeStruct((N, W), F32), jax.ShapeDtypeStruct((batch, n_pairs, PAIR, PAIR), F32)),
        grid=(batch, steps),
        in_specs=[spec] * 6,
        out_specs=(spec, pl.BlockSpec((None, n_pairs, PAIR, PAIR), lambda bb, t: (bb, 0, 0, 0))),
        scratch_shapes=[pltpu.VMEM((n_pairs, PAIR, PAIR), F32)],
        compiler_params=_params(("arbitrary", "arbitrary")),
        name="wkv_chunk_scan",
    )(r, ld, k, v, a, b)


def _split3(x):
    hi = x.astype(BF16)
    r1 = x - hi.astype(F32)
    mid = r1.astype(BF16)
    lo = (r1 - mid.astype(F32)).astype(BF16)
    return hi, mid, lo


def _select_dot(x, sel):
    hi, mid, lo = _split3(x)
    d = lambda t: lax.dot_general(t, sel, NN, preferred_element_type=F32)
    return d(hi) + d(mid) + d(lo)


def _wkv_step_body(s_ref, r_ref, ld_ref, k_ref, v_ref, a_ref, b_ref, exp_ref, red_ref, y_ref, so_ref):
    HD = HEAD_DIM
    nrep = HD * HD // LANES
    lane = lax.broadcasted_iota(jnp.int32, r_ref.shape, 1)
    low = lane < HD
    y = jnp.zeros(r_ref.shape, F32)
    for hh in range(2):
        def tiled(ref, fn=None):
            x = ref[...]
            if fn is not None:
                x = fn(x)
            sw = pltpu.roll(x, HD, 1)
            both = jnp.where(low, x, sw) if hh == 0 else jnp.where(low, sw, x)
            return jnp.tile(both, (1, nrep))
        cols = slice(hh * HD * HD, (hh + 1) * HD * HD)
        s = s_ref[:, cols]
        expand = exp_ref[hh]
        reduce_ = red_ref[hh]
        sa = _select_dot(s * tiled(a_ref), reduce_)
        s_new = (s * tiled(ld_ref, jnp.exp) + _select_dot(sa, expand) * tiled(b_ref)
                 + _select_dot(v_ref[...], expand) * tiled(k_ref))
        so_ref[:, cols] = s_new
        y = y + _select_dot(s_new * tiled(r_ref), reduce_)
    y_ref[...] = y


def _wkv_step_consts():
    HD = HEAD_DIM
    expand = np.zeros((2, PAIR, HD * HD), np.float32)
    for hh in range(2):
        for vv in range(HD):
            expand[hh, hh * HD + vv, vv * HD:(vv + 1) * HD] = 1.0
    return jnp.asarray(expand, BF16), jnp.asarray(expand.transpose(0, 2, 1), BF16)


def _wkv_sample(state, r, ld, k, v, a, b):
    B = state.shape[0]
    W = RWKV_WIDTH
    HD2 = HEAD_DIM * HEAD_DIM
    expand, reduce_ = _wkv_step_consts()
    s2 = state.reshape(B, RWKV_HEADS * HD2)
    st_spec = pl.BlockSpec((B, 2 * HD2), lambda j: (0, j))
    vec = pl.BlockSpec((B, PAIR), lambda j: (0, j))
    y, s_new = pl.pallas_call(
        _wkv_step_body,
        out_shape=(jax.ShapeDtypeStruct((B, W), F32), jax.ShapeDtypeStruct((B, RWKV_HEADS * HD2), F32)),
        grid=(W // PAIR,),
        in_specs=[st_spec] + [vec] * 6 + [pl.BlockSpec((2, PAIR, HD2), lambda j: (0, 0, 0)),
                                          pl.BlockSpec((2, HD2, PAIR), lambda j: (0, 0, 0))],
        out_specs=(vec, st_spec),
        compiler_params=_params(("arbitrary",)),
        name="wkv_step",
    )(s2, r, ld, k, v, a, b, expand, reduce_)
    return y, s_new.reshape(state.shape)


def _q_perm():
    return np.array([(h * SWA_GROUP + g) * HEAD_DIM + d for g in range(SWA_GROUP) for h in range(SWA_KV_HEADS)
                     for d in range(HEAD_DIM)], np.int32)


def _rel_bucket_np(dist):
    max_exact = N_BUCKETS // 2
    d = np.maximum(dist, 0)
    ratio = np.log(np.maximum(d, 1).astype(np.float32) / np.float32(max_exact)) / np.float32(
        np.log(MAX_DISTANCE / max_exact))
    large = np.minimum(max_exact + (ratio.astype(np.float32) * np.float32(N_BUCKETS - max_exact)).astype(np.int32),
                       N_BUCKETS - 1)
    return np.where(d < max_exact, d, large).astype(np.int32)


def _bias_from_buckets(idx, rb_ref, head):
    acc = jnp.zeros(idx.shape, F32)
    for bk in range(N_BUCKETS):
        acc = jnp.where(idx == bk, rb_ref[bk, head], acc)
    return acc


def _swa_prompt_body(rb_ref, sink_ref, idx_ref, q_ref, kp_ref, kc_ref, vp_ref, vc_ref, o_ref, bias_ref):
    Q = WINDOW
    first = (pl.program_id(0) == 0) & (pl.program_id(1) == 0)

    @pl.when(first)
    def _():
        idx = idx_ref[...]
        for g in range(SWA_GROUP):
            for h in range(SWA_KV_HEADS):
                bias_ref[g * SWA_KV_HEADS + h] = _bias_from_buckets(idx, rb_ref, h * SWA_GROUP + g)

    n = pl.program_id(1)
    qi = lax.broadcasted_iota(jnp.int32, (Q, 2 * Q), 0)
    kj = lax.broadcasted_iota(jnp.int32, (Q, 2 * Q), 1)
    valid = ((kj < Q) & (kj >= qi) & (n > 0)) | ((kj >= Q) & ((kj - Q) <= qi))
    lane_kv0 = lax.broadcasted_iota(jnp.int32, (Q, LANES), 1) < HEAD_DIM
    kcat = jnp.concatenate([kp_ref[...], kc_ref[...]], axis=0).astype(BF16)
    vcat = jnp.concatenate([vp_ref[...], vc_ref[...]], axis=0).astype(BF16)
    scale = HEAD_DIM ** -0.5
    for g in range(SWA_GROUP):
        qg = q_ref[:, g * LANES:(g + 1) * LANES]
        outs = []
        for h in range(SWA_KV_HEADS):
            qm = jnp.where(lane_kv0, qg, 0.0) if h == 0 else jnp.where(lane_kv0, 0.0, qg)
            s = _dot(qm, kcat, NT) * scale + bias_ref[g * SWA_KV_HEADS + h]
            s = jnp.where(valid, s, NEG_INF)
            sink = sink_ref[h * SWA_GROUP + g]
            m = jnp.maximum(jnp.max(s, axis=-1, keepdims=True), sink)
            p = jnp.exp(s - m)
            den = jnp.sum(p, axis=-1, keepdims=True) + jnp.exp(sink - m)
            outs.append(_dot(p, vcat) / den)
        o_ref[:, g * LANES:(g + 1) * LANES] = jnp.where(lane_kv0, outs[0], outs[1])


def _swa_prompt(q, k, v, rel_bias, sinks, batch, seq):
    N = q.shape[0]
    Q = WINDOW
    nb = seq // Q
    qi = np.arange(Q)[:, None]
    kj = np.arange(2 * Q)[None, :]
    idx = jnp.asarray(_rel_bucket_np(qi + Q - kj))
    cur = lambda bb, n: (bb * nb + n, 0)
    prev = lambda bb, n: (bb * nb + jnp.maximum(n - 1, 0), 0)
    kv_c = pl.BlockSpec((Q, KV_WIDTH), cur)
    kv_p = pl.BlockSpec((Q, KV_WIDTH), prev)
    smem = pl.BlockSpec(memory_space=pltpu.SMEM)
    return pl.pallas_call(
        _swa_prompt_body,
        out_shape=jax.ShapeDtypeStruct((N, SWA_WIDTH), F32),
        grid=(batch, nb),
        in_specs=[smem, smem, pl.BlockSpec((Q, 2 * Q), lambda bb, n: (0, 0)),
                  pl.BlockSpec((Q, SWA_WIDTH), cur), kv_p, kv_c, kv_p, kv_c],
        out_specs=pl.BlockSpec((Q, SWA_WIDTH), cur),
        scratch_shapes=[pltpu.VMEM((SWA_HEADS, Q, 2 * Q), F32)],
        compiler_params=_params(("arbitrary", "arbitrary")),
        name="swa_prompt",
    )(rel_bias, sinks, idx, q, k, k, v, v)


def _swa_sample_body(rb_ref, sink_ref, idx_ref, q_ref, kn_ref, vn_ref, kb_ref, vb_ref, o_ref, ko_ref, vo_ref,
                     bias_ref, extra_ref):
    W = WINDOW
    G, KVH = SWA_GROUP, SWA_KV_HEADS

    @pl.when(pl.program_id(0) == 0)
    def _():
        idx = jnp.broadcast_to(idx_ref[...], (SUBLANES, W))
        row = lax.broadcasted_iota(jnp.int32, (SUBLANES, W), 0)
        acc = jnp.zeros((SUBLANES, W), F32)
        ext = jnp.zeros((SUBLANES, LANES), F32)
        lane = lax.broadcasted_iota(jnp.int32, (SUBLANES, LANES), 1)
        for h in range(KVH):
            for g in range(G):
                head = h * G + g
                r = h * G + g
                acc = jnp.where(row == r, _bias_from_buckets(idx, rb_ref, head), acc)
                ext = jnp.where((row == r) & (lane == 0), rb_ref[0, head], ext)
                ext = jnp.where((row == r) & (lane == 1), sink_ref[head], ext)
        bias_ref[...] = acc
        extra_ref[...] = ext

    TB = q_ref.shape[0]
    lane_kv0 = lax.broadcasted_iota(jnp.int32, (TB, G, LANES), 2) < HEAD_DIM
    q4 = q_ref[...]
    qrows = jnp.concatenate([jnp.where(lane_kv0, q4, 0.0), jnp.where(lane_kv0, 0.0, q4)], axis=1)
    kb = kb_ref[...]
    vb = vb_ref[...]
    kn = kn_ref[...]
    vn = vn_ref[...]
    scale = HEAD_DIM ** -0.5
    bdims = (((2,), (2,)), ((0,), (0,)))
    s = lax.dot_general(qrows.astype(BF16), kb.astype(BF16), bdims, preferred_element_type=F32) * scale
    s = s + bias_ref[...][None]
    s_self = jnp.sum(qrows * kn, axis=-1, keepdims=True) * scale + extra_ref[:, 0:1][None]
    sink = extra_ref[:, 1:2][None]
    m = jnp.maximum(jnp.maximum(jnp.max(s, axis=-1, keepdims=True), s_self), sink)
    p = jnp.exp(s - m)
    p_self = jnp.exp(s_self - m)
    den = jnp.sum(p, axis=-1, keepdims=True) + p_self + jnp.exp(sink - m)
    pv = lax.dot_general(p.astype(BF16), vb.astype(BF16), (((2,), (1,)), ((0,), (0,))), preferred_element_type=F32)
    o = (pv + p_self * vn) / den
    o_ref[...] = jnp.where(lane_kv0, o[:, 0:G], o[:, G:2 * G])

    rowmod = lax.broadcasted_iota(jnp.int32, (TB, W, KV_WIDTH), 1)
    for buf, new, out in ((kb, kn, ko_ref), (vb, vn, vo_ref)):
        rolled = pltpu.roll(buf.reshape(TB * W, KV_WIDTH), TB * W - 1, 0).reshape(TB, W, KV_WIDTH)
        out[...] = jnp.where(rowmod == W - 1, new, rolled)


def _swa_sample(q, k, v, kbuf, vbuf, rel_bias, sinks):
    B = q.shape[0]
    W = WINDOW
    tb = 16
    idx = jnp.asarray(_rel_bucket_np(W - np.arange(W))[None, :])
    q4 = q.reshape(B, SWA_GROUP, LANES)
    kn = k.reshape(B, 1, KV_WIDTH)
    vn = v.reshape(B, 1, KV_WIDTH)
    kb = kbuf.reshape(B, W, KV_WIDTH)
    vb = vbuf.reshape(B, W, KV_WIDTH)
    smem = pl.BlockSpec(memory_space=pltpu.SMEM)
    b3 = lambda i: (i, 0, 0)
    o, ko, vo = pl.pallas_call(
        _swa_sample_body,
        out_shape=(jax.ShapeDtypeStruct((B, SWA_GROUP, LANES), F32), jax.ShapeDtypeStruct((B, W, KV_WIDTH), F32),
                   jax.ShapeDtypeStruct((B, W, KV_WIDTH), F32)),
        grid=(B // tb,),
        in_specs=[smem, smem, pl.BlockSpec((1, W), lambda i: (0, 0)),
                  pl.BlockSpec((tb, SWA_GROUP, LANES), b3), pl.BlockSpec((tb, 1, KV_WIDTH), b3),
                  pl.BlockSpec((tb, 1, KV_WIDTH), b3), pl.BlockSpec((tb, W, KV_WIDTH), b3),
                  pl.BlockSpec((tb, W, KV_WIDTH), b3)],
        out_specs=(pl.BlockSpec((tb, SWA_GROUP, LANES), b3), pl.BlockSpec((tb, W, KV_WIDTH), b3),
                   pl.BlockSpec((tb, W, KV_WIDTH), b3)),
        scratch_shapes=[pltpu.VMEM((SUBLANES, W), F32), pltpu.VMEM((SUBLANES, LANES), F32)],
        compiler_params=_params(("arbitrary",)),
        name="swa_sample",
    )(rel_bias, sinks, idx, q4, kn, vn, kb, vb)
    return o.reshape(B, SWA_WIDTH), ko.reshape(kbuf.shape), vo.reshape(vbuf.shape)


def _outproj_body(per_row, tiles_per_seq, prompt_row0, x_ref, mod_ref, yw_ref, g_ref, bonus_ref, ya_ref,
                  lnw_ref, lnb_ref, hm_ref, wr_ref, wa_ref, gpost_ref, o_ref):
    D = D_MODEL
    m = _mod_rows(mod_ref, per_row, tiles_per_seq, prompt_row0)
    yw = yw_ref[...]
    hm = hm_ref[...]
    mean = _dot_hi(yw, hm)
    dv = yw - mean
    var = _dot_hi(dv * dv, hm)
    yn = dv * lax.rsqrt(var + LN_X_EPS) * lnw_ref[...] + lnb_ref[...]
    yr = (yn + bonus_ref[...]) * g_ref[...]
    mix = _dot(yr, wr_ref[...]) + _dot(ya_ref[...], wa_ref[...])
    o_ref[...] = x_ref[...] + m[:, 2 * D:3 * D] * _rms(mix, gpost_ref[...])


def _outproj(x, mod, yw, g, bonus, ya, lw, per_row, tm, tiles_per_seq, prompt_row0):
    N, D = x.shape
    W = RWKV_WIDTH
    row = lambda i: (i, 0)
    fixed = lambda i: (0, 0)
    half = pl.BlockSpec((tm, W), row)
    vec = pl.BlockSpec((1, W), fixed)
    return pl.pallas_call(
        functools.partial(_outproj_body, per_row, tiles_per_seq, prompt_row0),
        out_shape=jax.ShapeDtypeStruct((N, D), F32),
        grid=(N // tm,),
        in_specs=[pl.BlockSpec((tm, D), row), _mod_spec(mod, per_row, tm, 1), half, half, half, half, vec, vec,
                  pl.BlockSpec((W, W), fixed), pl.BlockSpec((W, D), fixed), pl.BlockSpec((W, D), fixed),
                  pl.BlockSpec((1, D), fixed)],
        out_specs=pl.BlockSpec((tm, D), row),
        compiler_params=_params(("arbitrary",)),
        name="out_proj",
    )(x, mod, yw, g, bonus, ya, lw["lnw"], lw["lnb"], lw["headmean"], lw["w_out_r"], lw["w_out_a"], lw["g_post0"])


def _ffn_body(moe, per_row, tiles_per_seq, prompt_row0, x_ref, mod_ref, gpre_ref, gpost_ref, rw_ref, rb_ref,
              wg_ref, wu_ref, wd_ref, o_ref, h_ref, acc_ref, comb_ref):
    D = D_MODEL
    e = pl.program_id(1)
    f = pl.program_id(2)
    first = (e == 0) & (f == 0)
    last = (e == pl.num_programs(1) - 1) & (f == pl.num_programs(2) - 1)

    @pl.when(first)
    def _():
        m = _mod_rows(mod_ref, per_row, tiles_per_seq, prompt_row0)
        h = _rms(x_ref[...], gpre_ref[...]) * (1.0 + m[:, D:2 * D]) + m[:, 0:D]
        h_ref[...] = h.astype(BF16)
        acc_ref[...] = jnp.zeros_like(acc_ref)
        if moe:
            logits = _dot_hi(h, rw_ref[...]) + rb_ref[...]
            lane = lax.broadcasted_iota(jnp.int32, logits.shape, 1)
            m1 = jnp.max(logits, axis=-1, keepdims=True)
            i1 = jnp.min(jnp.where(logits == m1, lane, LANES), axis=-1, keepdims=True)
            rest = jnp.where(lane == i1, -jnp.inf, logits)
            m2 = jnp.max(rest, axis=-1, keepdims=True)
            i2 = jnp.min(jnp.where(rest == m2, lane, LANES), axis=-1, keepdims=True)
            e2 = jnp.exp(m2 - m1)
            comb_ref[...] = jnp.where(lane == i1, 1.0 / (1.0 + e2), 0.0) + jnp.where(lane == i2, e2 / (1.0 + e2), 0.0)

    h = h_ref[...]
    gate = _dot(h, wg_ref[...])
    up = _dot(h, wu_ref[...])
    act = gate * _sigmoid(gate) * up
    if moe:
        comb = comb_ref[...]
        lane = lax.broadcasted_iota(jnp.int32, comb.shape, 1)
        act = act * jnp.sum(jnp.where(lane == e, comb, 0.0), axis=-1, keepdims=True)
    acc_ref[...] += _dot(act, wd_ref[...])

    @pl.when(last)
    def _():
        m = _mod_rows(mod_ref, per_row, tiles_per_seq, prompt_row0)
        o_ref[...] = x_ref[...] + m[:, 2 * D:3 * D] * _rms(acc_ref[...], gpost_ref[...])


def _ffn(x, mod, gpre, gpost, router_w, router_b, wg, wu, wd, moe, per_row, tm, tiles_per_seq, prompt_row0, tf):
    N, D = x.shape
    E, _, F = wg.shape
    row = lambda i, e, f: (i, 0)
    fixed = lambda i, e, f: (0, 0)
    return pl.pallas_call(
        functools.partial(_ffn_body, moe, per_row, tiles_per_seq, prompt_row0),
        out_shape=jax.ShapeDtypeStruct((N, D), F32),
        grid=(N // tm, E, F // tf),
        in_specs=[pl.BlockSpec((tm, D), row), _mod_spec(mod, per_row, tm, 3),
                  pl.BlockSpec((1, D), fixed), pl.BlockSpec((1, D), fixed),
                  pl.BlockSpec((D, LANES), fixed), pl.BlockSpec((1, LANES), fixed),
                  pl.BlockSpec((None, D, tf), lambda i, e, f: (e, 0, f)),
                  pl.BlockSpec((None, D, tf), lambda i, e, f: (e, 0, f)),
                  pl.BlockSpec((None, tf, D), lambda i, e, f: (e, f, 0))],
        out_specs=pl.BlockSpec((tm, D), row),
        scratch_shapes=[pltpu.VMEM((tm, D), BF16), pltpu.VMEM((tm, D), F32), pltpu.VMEM((tm, LANES), F32)],
        compiler_params=_params(("arbitrary", "arbitrary", "arbitrary")),
        name="moe_ffn" if moe else "dense_ffn",
    )(x, mod, gpre, gpost, router_w, router_b, wg, wu, wd)


def _layer_weights(p, l):
    W = RWKV_WIDTH
    heads = np.arange(W) // HEAD_DIM
    headsum = jnp.asarray((heads[:, None] == heads[None, :]).astype(np.float32))
    perm = _q_perm()
    w_in = p["w_in"][l]
    w_in = jnp.concatenate([w_in[:, :RWKV_COLS], w_in[:, RWKV_COLS + perm], w_in[:, RWKV_COLS + SWA_WIDTH:]], axis=1)
    w_out = p["w_out"][l]
    zeros = jnp.zeros((DECAY_LORA, W), F32)
    row = lambda t: t.reshape(1, -1)
    return {
        "w_in": w_in.astype(BF16),
        "w_out_r": w_out[:W].astype(BF16),
        "w_out_a": w_out[W + perm].astype(BF16),
        "mu": row(p["mu_shift"][l]),
        "wd": jnp.concatenate([p["w_decay_up"][l], zeros], axis=0),
        "wi": jnp.concatenate([zeros, p["w_iclr_up"][l]], axis=0),
        "wg": p["w_gate_up"][l],
        "dbase": row(p["decay_base"][l]), "ibase": row(p["iclr_base"][l]),
        "kk": row(p["k_k"][l]), "ka": row(p["k_a"][l]), "rk": row(p["r_k"][l]),
        "lnw": row(p["lnx_w"][l]), "lnb": row(p["lnx_b"][l]),
        "headsum": headsum, "headmean": headsum / HEAD_DIM,
        "g_pre0": row(p["norm_pre"][l, 0]), "g_pre1": row(p["norm_pre"][l, 1]),
        "g_post0": row(p["norm_post"][l, 0]), "g_post1": row(p["norm_post"][l, 1]),
    }


def _ffn_weights(p, l):
    i = l // 2
    if l % 2 == 0:
        return dict(moe=False, rw=jnp.zeros((D_MODEL, LANES), F32), rb=jnp.zeros((1, LANES), F32),
                    wg=p["ffn_w_gate"][i][None].astype(BF16), wu=p["ffn_w_up"][i][None].astype(BF16),
                    wd=p["ffn_w_down"][i][None].astype(BF16))
    rw = jnp.zeros((D_MODEL, LANES), F32).at[:, :N_EXPERTS].set(p["router_w"][i])
    rb = jnp.full((1, LANES), NEG_INF, F32).at[0, :N_EXPERTS].set(p["router_b"][i])
    return dict(moe=True, rw=rw, rb=rb, wg=p["moe_w_gate"][i].astype(BF16), wu=p["moe_w_up"][i].astype(BF16),
                wd=p["moe_w_down"][i].astype(BF16))


def _pick_tile(n, pref):
    t = min(pref, n)
    while n % t:
        t //= 2
    return t


def _ffn_tile(f):
    for t in (1408, 896, 512, 256, 128):
        if f % t == 0:
            return t
    return f


def _trunk(x3, mods, lws, fws, p, prompt, prompt_row0, state=None):
    B, T, D = x3.shape
    N = B * T
    x = x3.reshape(N, D)
    per_row = not prompt
    tm = _pick_tile(T if prompt else N, 512)
    tps = (T // tm) if prompt else 1
    depth = len(lws)
    wkv_out, shift_out, k_out, v_out = [], [], [], []
    for l in range(depth):
        lw, fw = lws[l], fws[l]
        mod0, mod1 = mods[2 * l], mods[2 * l + 1]
        pr, q, k, v = _inproj(x, mod0, lw["g_pre0"], lw["w_in"], per_row, tm, tps, prompt_row0)
        prev = pr if prompt else state["shift"][l]
        r, ld, kh, vv, a, b, g, bonus = _prep(pr, prev, lw, per_row, tm, tps)
        if prompt:
            yw, hbd = _wkv_prompt(r, ld, kh, vv, a, b, B, T)
            n_pairs = RWKV_WIDTH // PAIR
            hb = hbd.reshape(B, n_pairs, 2, HEAD_DIM, 2, HEAD_DIM)
            s_kv = jnp.stack([hb[:, :, 0, :, 0, :], hb[:, :, 1, :, 1, :]], axis=2)
            s_new = jnp.swapaxes(s_kv.reshape(B, RWKV_HEADS, HEAD_DIM, HEAD_DIM), -1, -2)
            ya = _swa_prompt(q, k, v, p["rel_bias"], p["attn_sinks"][l], B, T)
            kb = k.reshape(B, T, SWA_KV_HEADS, HEAD_DIM)[:, -WINDOW:]
            vb = v.reshape(B, T, SWA_KV_HEADS, HEAD_DIM)[:, -WINDOW:]
            last = pr.reshape(B, T, RWKV_COLS)[:, -1]
        else:
            yw, s_new = _wkv_sample(state["wkv"][l], r, ld, kh, vv, a, b)
            ya, kb, vb = _swa_sample(q, k, v, state["k"][l], state["v"][l], p["rel_bias"], p["attn_sinks"][l])
            last = pr
        x = _outproj(x, mod0, yw, g, bonus, ya, lw, per_row, tm, tps, prompt_row0)
        x = _ffn(x, mod1, lw["g_pre1"], lw["g_post1"], fw["rw"], fw["rb"], fw["wg"], fw["wu"], fw["wd"],
                 fw["moe"], per_row, tm, tps, prompt_row0, _ffn_tile(fw["wg"].shape[-1]))
        wkv_out.append(s_new)
        shift_out.append(last)
        k_out.append(kb)
        v_out.append(vb)
    return x.reshape(B, T, D), jnp.stack(wkv_out), jnp.stack(shift_out), jnp.stack(k_out), jnp.stack(v_out)


def _forward(x_prompt, x_sample, c_prompt, c_sample, state_wkv, state_shift, cache_swa_k, cache_swa_v, p):
    depth = p["w_in"].shape[0]
    Bp, Bs = c_prompt.shape[0], c_sample.shape[0]
    D = D_MODEL
    pad = (-(Bs + Bp)) % SUBLANES
    c_all = jnp.concatenate([c_sample, c_prompt, jnp.zeros((pad, D), F32)], axis=0)
    mods = _ada_all(c_all, p["ada_w"].reshape(2 * depth, D, 3 * D), p["ada_b"].reshape(2 * depth, 1, 3 * D))
    lws = [_layer_weights(p, l) for l in range(depth)]
    fws = [_ffn_weights(p, l) for l in range(depth)]
    y_p, wkv_p, shift_p, k_p, v_p = _trunk(x_prompt, mods, lws, fws, p, True, Bs)
    state = {"wkv": state_wkv, "shift": state_shift, "k": cache_swa_k, "v": cache_swa_v}
    y_s, wkv_s, shift_s, k_s, v_s = _trunk(x_sample, mods, lws, fws, p, False, Bs, state)
    return (y_p, y_s, wkv_p, shift_p, k_p, v_p, wkv_s, shift_s, k_s, v_s)


def kernel(x_prompt, x_sample, c_prompt, c_sample, state_wkv, state_shift, cache_swa_k, cache_swa_v, rel_bias, ada_w, ada_b, norm_pre, norm_post, w_in, mu_shift, w_decay_up, decay_base, w_iclr_up, iclr_base, w_gate_up, k_k, k_a, r_k, lnx_w, lnx_b, attn_sinks, w_out, ffn_w_gate, ffn_w_up, ffn_w_down, router_w, router_b, moe_w_gate, moe_w_up, moe_w_down):
    p = {"rel_bias": rel_bias, "ada_w": ada_w, "ada_b": ada_b, "norm_pre": norm_pre, "norm_post": norm_post,
         "w_in": w_in, "mu_shift": mu_shift, "w_decay_up": w_decay_up, "decay_base": decay_base,
         "w_iclr_up": w_iclr_up, "iclr_base": iclr_base, "w_gate_up": w_gate_up, "k_k": k_k, "k_a": k_a,
         "r_k": r_k.reshape(r_k.shape[0], -1), "lnx_w": lnx_w, "lnx_b": lnx_b, "attn_sinks": attn_sinks,
         "w_out": w_out, "ffn_w_gate": ffn_w_gate, "ffn_w_up": ffn_w_up, "ffn_w_down": ffn_w_down,
         "router_w": router_w, "router_b": router_b, "moe_w_gate": moe_w_gate, "moe_w_up": moe_w_up,
         "moe_w_down": moe_w_down}
    return _forward(x_prompt, x_sample, c_prompt, c_sample, state_wkv, state_shift, cache_swa_k, cache_swa_v, p)
```

```python
import functools

import numpy as np
import jax
import jax.numpy as jnp
from jax import lax
from jax.experimental import pallas as pl
from jax.experimental.pallas import tpu as pltpu

F32 = jnp.float32
BF16 = jnp.bfloat16
HIGHEST = lax.Precision.HIGHEST

D_MODEL = 1024
HEAD_DIM = 64
RWKV_WIDTH = 512
RWKV_HEADS = RWKV_WIDTH // HEAD_DIM
SWA_WIDTH = 512
SWA_HEADS = SWA_WIDTH // HEAD_DIM
SWA_KV_HEADS = 2
SWA_GROUP = SWA_HEADS // SWA_KV_HEADS
KV_WIDTH = SWA_KV_HEADS * HEAD_DIM
WINDOW = 128
DECAY_LORA = 64
ICLR_LORA = 64
GATE_LORA = 128
RWKV_COLS = 3 * RWKV_WIDTH + DECAY_LORA + ICLR_LORA + GATE_LORA
IN_COLS = RWKV_COLS + SWA_WIDTH + 2 * KV_WIDTH
LORA_OFF = 3 * RWKV_WIDTH
GATE_OFF = LORA_OFF + DECAY_LORA + ICLR_LORA
LN_X_EPS = 64e-5
RMS_EPS = 1e-6
N_BUCKETS = 32
MAX_DISTANCE = 128
N_EXPERTS = 8
NEG_INF = -1e30

LANES = 128
SUBLANES = 8
VMEM_LIMIT = 56 * 1024 * 1024

WKV_CHUNK = 64
PAIR = 2 * HEAD_DIM

NN = (((1,), (0,)), ((), ()))
NT = (((1,), (1,)), ((), ()))
TN = (((0,), (0,)), ((), ()))


def _dot(a, b, dims=NN):
    return lax.dot_general(a.astype(BF16), b.astype(BF16), dims, preferred_element_type=F32)


def _dot_hi(a, b, dims=NN):
    return lax.dot_general(a.astype(F32), b.astype(F32), dims, precision=HIGHEST, preferred_element_type=F32)


def _sigmoid(x):
    return 1.0 / (1.0 + jnp.exp(-x))


def _params(sem):
    return pltpu.CompilerParams(dimension_semantics=sem, vmem_limit_bytes=VMEM_LIMIT)


def _rms(x, g):
    return x * lax.rsqrt(jnp.mean(x * x, axis=-1, keepdims=True) + RMS_EPS) * g


def _mod_rows(mod_ref, per_row, tiles_per_seq, prompt_row0):
    if per_row:
        return mod_ref[...]
    b = pl.program_id(0) // tiles_per_seq
    return mod_ref[pl.ds(prompt_row0 + b, 1), :]


def _ada_body(c_ref, w_ref, b_ref, o_ref):
    c = c_ref[...]
    o_ref[...] = _dot(c * _sigmoid(c), w_ref[...]) + b_ref[...]


def _ada_all(c_all, ada_w, ada_b):
    R, D = c_all.shape
    n = ada_w.shape[0]
    tn = 1024
    return pl.pallas_call(
        _ada_body,
        out_shape=jax.ShapeDtypeStruct((n, R, 3 * D), F32),
        grid=(n, 3 * D // tn),
        in_specs=[pl.BlockSpec((R, D), lambda i, j: (0, 0)),
                  pl.BlockSpec((None, D, tn), lambda i, j: (i, 0, j)),
                  pl.BlockSpec((None, 1, tn), lambda i, j: (i, 0, j))],
        out_specs=pl.BlockSpec((None, R, tn), lambda i, j: (i, 0, j)),
        compiler_params=_params(("arbitrary", "arbitrary")),
        name="ada_mod",
    )(c_all, ada_w, ada_b)


def _inproj_body(per_row, tiles_per_seq, prompt_row0, x_ref, mod_ref, g_ref, w_ref, pr_ref, q_ref, k_ref, v_ref):
    D = D_MODEL
    m = _mod_rows(mod_ref, per_row, tiles_per_seq, prompt_row0)
    h = _rms(x_ref[...], g_ref[...]) * (1.0 + m[:, D:2 * D]) + m[:, 0:D]
    proj = _dot(h, w_ref[...])
    pr_ref[...] = proj[:, 0:RWKV_COLS]
    q_ref[...] = proj[:, RWKV_COLS:RWKV_COLS + SWA_WIDTH]
    k_ref[...] = proj[:, RWKV_COLS + SWA_WIDTH:RWKV_COLS + SWA_WIDTH + KV_WIDTH]
    v_ref[...] = proj[:, RWKV_COLS + SWA_WIDTH + KV_WIDTH:IN_COLS]


def _mod_spec(mod, per_row, tm, nargs):
    R = mod.shape[0]
    if per_row:
        return pl.BlockSpec((tm, 3 * D_MODEL), lambda i, *_: (0, 0))
    return pl.BlockSpec((R, 3 * D_MODEL), lambda i, *_: (0, 0))


def _inproj(x, mod, g, w, per_row, tm, tiles_per_seq, prompt_row0):
    N, D = x.shape
    row = lambda i: (i, 0)
    fixed = lambda i: (0, 0)
    return pl.pallas_call(
        functools.partial(_inproj_body, per_row, tiles_per_seq, prompt_row0),
        out_shape=(jax.ShapeDtypeStruct((N, RWKV_COLS), F32), jax.ShapeDtypeStruct((N, SWA_WIDTH), F32),
                   jax.ShapeDtypeStruct((N, KV_WIDTH), F32), jax.ShapeDtypeStruct((N, KV_WIDTH), F32)),
        grid=(N // tm,),
        in_specs=[pl.BlockSpec((tm, D), row), _mod_spec(mod, per_row, tm, 1),
                  pl.BlockSpec((1, D), fixed), pl.BlockSpec((D, IN_COLS), fixed)],
        out_specs=(pl.BlockSpec((tm, RWKV_COLS), row), pl.BlockSpec((tm, SWA_WIDTH), row),
                   pl.BlockSpec((tm, KV_WIDTH), row), pl.BlockSpec((tm, KV_WIDTH), row)),
        compiler_params=_params(("arbitrary",)),
        name="in_proj",
    )(x, mod, g, w)


def _prep_body(per_row, tiles_per_seq, pr_ref, prev_ref, mu_ref, wd_ref, wi_ref, wg_ref, dbase_ref, ibase_ref,
               kk_ref, ka_ref, rk_ref, hs_ref,
               r_ref, ld_ref, k_ref, v_ref, a_ref, b_ref, g_ref, bonus_ref):
    W = RWKV_WIDTH
    pr = pr_ref[...]
    if per_row:
        shifted = prev_ref[...]
    else:
        first = (pl.program_id(0) % tiles_per_seq) == 0
        carry = jnp.where(first, 0.0, prev_ref[SUBLANES - 1:SUBLANES, :])
        rows = lax.broadcasted_iota(jnp.int32, pr.shape, 0)
        shifted = jnp.where(rows == 0, carry, pltpu.roll(pr, 1, 0))
    xs = pr + (shifted - pr) * mu_ref[...]
    r = xs[:, 0:W]
    k = xs[:, W:2 * W]
    v = xs[:, 2 * W:3 * W]
    lora = xs[:, LORA_OFF:GATE_OFF]
    gl = xs[:, GATE_OFF:RWKV_COLS]
    z = dbase_ref[...] + _dot_hi(jnp.tanh(lora), wd_ref[...])
    ld = -float(np.exp(-0.5)) * _sigmoid(z)
    iclr = _sigmoid(ibase_ref[...] + _dot_hi(lora, wi_ref[...]))
    g = _dot_hi(_sigmoid(gl), wg_ref[...])
    hs = hs_ref[...]
    kk = k * kk_ref[...]
    kk = kk / jnp.maximum(jnp.sqrt(_dot_hi(kk * kk, hs)), 1e-12)
    kh = k * (1.0 + (iclr - 1.0) * ka_ref[...])
    bonus = _dot_hi(r * kh * rk_ref[...], hs) * v
    r_ref[...] = r
    ld_ref[...] = ld
    k_ref[...] = kh
    v_ref[...] = v
    a_ref[...] = -kk
    b_ref[...] = kk * iclr
    g_ref[...] = g
    bonus_ref[...] = bonus


def _prep(pr, prev, lw, per_row, tm, tiles_per_seq):
    N = pr.shape[0]
    W = RWKV_WIDTH
    row = lambda i: (i, 0)
    fixed = lambda i: (0, 0)
    if per_row:
        prev_spec = pl.BlockSpec((tm, RWKV_COLS), row)
    else:
        per = tm // SUBLANES
        prev_spec = pl.BlockSpec((SUBLANES, RWKV_COLS), lambda i: (jnp.maximum(i * per - 1, 0), 0))
    vec = pl.BlockSpec((1, W), fixed)
    out = jax.ShapeDtypeStruct((N, W), F32)
    return pl.pallas_call(
        functools.partial(_prep_body, per_row, tiles_per_seq),
        out_shape=(out,) * 8,
        grid=(N // tm,),
        in_specs=[pl.BlockSpec((tm, RWKV_COLS), row), prev_spec, pl.BlockSpec((1, RWKV_COLS), fixed),
                  pl.BlockSpec((LANES, W), fixed), pl.BlockSpec((LANES, W), fixed), pl.BlockSpec((GATE_LORA, W), fixed),
                  vec, vec, vec, vec, vec, pl.BlockSpec((W, W), fixed)],
        out_specs=(pl.BlockSpec((tm, W), row),) * 8,
        compiler_params=_params(("arbitrary",)),
        name="rwkv_prep",
    )(pr, prev, lw["mu"], lw["wd"], lw["wi"], lw["wg"], lw["dbase"], lw["ibase"], lw["kk"], lw["ka"], lw["rk"],
      lw["headsum"])


def _stack_heads(x, lane_head0):
    return jnp.concatenate([jnp.where(lane_head0, x, 0.0), jnp.where(lane_head0, 0.0, x)], axis=0)


def _fold_heads(x):
    c = x.shape[0] // 2
    return x[0:c] + x[c:2 * c]


def _dots(xs, ys, dims=NN):
    return [_dot(x, y, dims) for x, y in zip(xs, ys)]


def _unit_lower_inverse(ns, same16, eye):
    nd = [jnp.where(same16, n, 0.0) for n in ns]
    no = [n - d for n, d in zip(ns, nd)]
    n2 = _dots(nd, nd)
    n4 = _dots(n2, n2)
    n8 = _dots(n4, n4)
    td = [eye + d for d in nd]
    for pw in (n2, n4, n8):
        td = [t + u for t, u in zip(td, _dots(td, pw))]
    q = _dots(td, no)
    q2 = _dots(q, q)
    z = [eye + x for x in q]
    z = [t + u for t, u in zip(z, _dots(z, q2))]
    return _dots(z, td)


def _wkv_chunk_body(chunks, r_ref, ld_ref, k_ref, v_ref, a_ref, b_ref, y_ref, s_ref, h_ref):
    C = WKV_CHUNK
    n_pairs = RWKV_WIDTH // PAIR

    @pl.when(pl.program_id(1) == 0)
    def _():
        h_ref[...] = jnp.zeros_like(h_ref)

    ri = lax.broadcasted_iota(jnp.int32, (PAIR, PAIR), 0)
    ci = lax.broadcasted_iota(jnp.int32, (PAIR, PAIR), 1)
    same_head = (ri // C) == (ci // C)
    strict_lower = same_head & (ci < ri)
    incl_lower = same_head & (ci <= ri)
    same16 = (ri // 16) == (ci // 16)
    eye_b = ri == ci
    eye = jnp.where(eye_b, 1.0, 0.0)
    tri = jnp.where(lax.broadcasted_iota(jnp.int32, (C, C), 1) <= lax.broadcasted_iota(jnp.int32, (C, C), 0), 1.0, 0.0)
    lane_head0 = lax.broadcasted_iota(jnp.int32, (C, PAIR), 1) < HEAD_DIM
    zeros = jnp.zeros((PAIR, PAIR), F32)

    cat0 = lambda x, y: jnp.concatenate([x, y], axis=0)
    cat1 = lambda x, y: jnp.concatenate([x, y], axis=1)

    xa, xr, v_st, bh_st, kh_st, yb, yk, p_all = [], [], [], [], [], [], [], []
    for c in range(chunks):
        rows = slice(c * C, (c + 1) * C)
        ld = ld_ref[rows, :]
        cum = _dot_hi(tri, ld)
        last = cum[C - 1:C, :]
        p_inv = jnp.exp(-cum)
        p_tail = jnp.exp(last - cum)
        p_end = jnp.exp(last)
        a_t = a_ref[rows, :] * jnp.exp(cum - ld)
        r_t = r_ref[rows, :] * jnp.exp(cum)
        b_raw = b_ref[rows, :]
        k_raw = k_ref[rows, :]
        b_t = b_raw * p_inv
        k_t = k_raw * p_inv
        b_h = b_raw * p_tail
        k_h = k_raw * p_tail
        v_all = v_ref[rows, :]
        for j in range(n_pairs):
            lanes = slice(j * PAIR, (j + 1) * PAIR)
            xa.append(_stack_heads(a_t[:, lanes], lane_head0))
            xr.append(_stack_heads(r_t[:, lanes], lane_head0))
            v_st.append(_stack_heads(v_all[:, lanes], lane_head0))
            bh_st.append(_stack_heads(b_h[:, lanes], lane_head0))
            kh_st.append(_stack_heads(k_h[:, lanes], lane_head0))
            yb.append(cat0(b_t[:, lanes], b_t[:, lanes]))
            yk.append(cat0(k_t[:, lanes], k_t[:, lanes]))
            p_all.append(p_end[:, lanes])

    gram = _dots([cat0(x, y) for x, y in zip(xa, xr)], [cat0(x, y) for x, y in zip(yb, yk)], NT)
    n_mat = [jnp.where(strict_lower, g[0:PAIR, 0:PAIR], 0.0) for g in gram]
    m_mat = [jnp.where(strict_lower, g[0:PAIR, PAIR:2 * PAIR], 0.0) for g in gram]
    a_rbk = [cat1(jnp.where(incl_lower, g[PAIR:2 * PAIR, 0:PAIR], 0.0),
                  jnp.where(incl_lower, g[PAIR:2 * PAIR, PAIR:2 * PAIR], 0.0)) for g in gram]
    t_inv = _unit_lower_inverse(n_mat, same16, eye)
    mv = _dots(m_mat, v_st)
    tx = _dots(t_inv, [cat1(x, y) for x, y in zip(xa, mv)])
    rhs = [cat0(t, cat1(zeros, v)) for t, v in zip(tx, v_st)]
    ry = _dots(a_rbk, rhs)
    pp = _dots([cat0(x, y) for x, y in zip(bh_st, kh_st)], rhs, TN)

    for c in range(chunks):
        us = [c * n_pairs + j for j in range(n_pairs)]
        h0 = [h_ref[j] for j in range(n_pairs)]
        r_bar = [_fold_heads(xr[u] + ry[u][:, 0:PAIR]) for u in us]
        phi = [pp[u][:, 0:PAIR] + jnp.where(eye_b, p_all[u], 0.0) for u in us]
        ys = _dots(r_bar, h0)
        hs = _dots(phi, h0)
        for j, u in enumerate(us):
            y_ref[c * C:(c + 1) * C, j * PAIR:(j + 1) * PAIR] = ys[j] + _fold_heads(ry[u][:, PAIR:2 * PAIR])
            h_ref[j] = hs[j] + pp[u][:, PAIR:2 * PAIR]

    @pl.when(pl.program_id(1) == pl.num_programs(1) - 1)
    def _():
        s_ref[...] = h_ref[...]


def _wkv_prompt(r, ld, k, v, a, b, batch, seq):
    N, W = r.shape
    chunks = 4
    tt = chunks * WKV_CHUNK
    steps = seq // tt
    n_pairs = W // PAIR
    row = lambda bb, t: (bb * steps + t, 0)
    spec = pl.BlockSpec((tt, W), row)
    return pl.pallas_call(
        functools.partial(_wkv_chunk_body, chunks),
        out_shape=(jax.ShapeDtypeStruct((N, W), F32), jax.ShapeDtypeStruct((batch, n_pairs, PAIR, PAIR), F32)),
        grid=(batch, steps),
        in_specs=[spec] * 6,
        out_specs=(spec, pl.BlockSpec((None, n_pairs, PAIR, PAIR), lambda bb, t: (bb, 0, 0, 0))),
        scratch_shapes=[pltpu.VMEM((n_pairs, PAIR, PAIR), F32)],
        compiler_params=_params(("arbitrary", "arbitrary")),
        name="wkv_chunk_scan",
    )(r, ld, k, v, a, b)


def _split3(x):
    hi = x.astype(BF16)
    r1 = x - hi.astype(F32)
    mid = r1.astype(BF16)
    lo = (r1 - mid.astype(F32)).astype(BF16)
    return hi, mid, lo


def _select_dot(x, sel):
    hi, mid, lo = _split3(x)
    d = lambda t: lax.dot_general(t, sel, NN, preferred_element_type=F32)
    return d(hi) + d(mid) + d(lo)


def _wkv_step_body(s_ref, r_ref, ld_ref, k_ref, v_ref, a_ref, b_ref, exp_ref, red_ref, y_ref, so_ref):
    HD = HEAD_DIM
    nrep = HD * HD // LANES
    lane = lax.broadcasted_iota(jnp.int32, r_ref.shape, 1)
    low = lane < HD
    y = jnp.zeros(r_ref.shape, F32)
    for hh in range(2):
        def tiled(ref, fn=None):
            x = ref[...]
            if fn is not None:
                x = fn(x)
            sw = pltpu.roll(x, HD, 1)
            both = jnp.where(low, x, sw) if hh == 0 else jnp.where(low, sw, x)
            return jnp.tile(both, (1, nrep))
        cols = slice(hh * HD * HD, (hh + 1) * HD * HD)
        s = s_ref[:, cols]
        expand = exp_ref[hh]
        reduce_ = red_ref[hh]
        sa = _select_dot(s * tiled(a_ref), reduce_)
        s_new = (s * tiled(ld_ref, jnp.exp) + _select_dot(sa, expand) * tiled(b_ref)
                 + _select_dot(v_ref[...], expand) * tiled(k_ref))
        so_ref[:, cols] = s_new
        y = y + _select_dot(s_new * tiled(r_ref), reduce_)
    y_ref[...] = y


def _wkv_step_consts():
    HD = HEAD_DIM
    expand = np.zeros((2, PAIR, HD * HD), np.float32)
    for hh in range(2):
        for vv in range(HD):
            expand[hh, hh * HD + vv, vv * HD:(vv + 1) * HD] = 1.0
    return jnp.asarray(expand, BF16), jnp.asarray(expand.transpose(0, 2, 1), BF16)


def _wkv_sample(state, r, ld, k, v, a, b):
    B = state.shape[0]
    W = RWKV_WIDTH
    HD2 = HEAD_DIM * HEAD_DIM
    expand, reduce_ = _wkv_step_consts()
    s2 = state.reshape(B, RWKV_HEADS * HD2)
    st_spec = pl.BlockSpec((B, 2 * HD2), lambda j: (0, j))
    vec = pl.BlockSpec((B, PAIR), lambda j: (0, j))
    y, s_new = pl.pallas_call(
        _wkv_step_body,
        out_shape=(jax.ShapeDtypeStruct((B, W), F32), jax.ShapeDtypeStruct((B, RWKV_HEADS * HD2), F32)),
        grid=(W // PAIR,),
        in_specs=[st_spec] + [vec] * 6 + [pl.BlockSpec((2, PAIR, HD2), lambda j: (0, 0, 0)),
                                          pl.BlockSpec((2, HD2, PAIR), lambda j: (0, 0, 0))],
        out_specs=(vec, st_spec),
        compiler_params=_params(("arbitrary",)),
        name="wkv_step",
    )(s2, r, ld, k, v, a, b, expand, reduce_)
    return y, s_new.reshape(state.shape)


def _q_perm():
    return np.array([(h * SWA_GROUP + g) * HEAD_DIM + d for g in range(SWA_GROUP) for h in range(SWA_KV_HEADS)
                     for d in range(HEAD_DIM)], np.int32)


def _rel_bucket_np(dist):
    max_exact = N_BUCKETS // 2
    d = np.maximum(dist, 0)
    ratio = np.log(np.maximum(d, 1).astype(np.float32) / np.float32(max_exact)) / np.float32(
        np.log(MAX_DISTANCE / max_exact))
    large = np.minimum(max_exact + (ratio.astype(np.float32) * np.float32(N_BUCKETS - max_exact)).astype(np.int32),
                       N_BUCKETS - 1)
    return np.where(d < max_exact, d, large).astype(np.int32)


def _bias_from_buckets(idx, rb_ref, head):
    acc = jnp.zeros(idx.shape, F32)
    for bk in range(N_BUCKETS):
        acc = jnp.where(idx == bk, rb_ref[bk, head], acc)
    return acc


def _swa_prompt_body(rb_ref, sink_ref, idx_ref, q_ref, kp_ref, kc_ref, vp_ref, vc_ref, o_ref, bias_ref):
    Q = WINDOW
    first = (pl.program_id(0) == 0) & (pl.program_id(1) == 0)

    @pl.when(first)
    def _():
        idx = idx_ref[...]
        for g in range(SWA_GROUP):
            for h in range(SWA_KV_HEADS):
                bias_ref[g * SWA_KV_HEADS + h] = _bias_from_buckets(idx, rb_ref, h * SWA_GROUP + g)

    n = pl.program_id(1)
    qi = lax.broadcasted_iota(jnp.int32, (Q, 2 * Q), 0)
    kj = lax.broadcasted_iota(jnp.int32, (Q, 2 * Q), 1)
    valid = ((kj < Q) & (kj >= qi) & (n > 0)) | ((kj >= Q) & ((kj - Q) <= qi))
    lane_kv0 = lax.broadcasted_iota(jnp.int32, (Q, LANES), 1) < HEAD_DIM
    kcat = jnp.concatenate([kp_ref[...], kc_ref[...]], axis=0).astype(BF16)
    vcat = jnp.concatenate([vp_ref[...], vc_ref[...]], axis=0).astype(BF16)
    scale = HEAD_DIM ** -0.5
    for g in range(SWA_GROUP):
        qg = q_ref[:, g * LANES:(g + 1) * LANES]
        outs = []
        for h in range(SWA_KV_HEADS):
            qm = jnp.where(lane_kv0, qg, 0.0) if h == 0 else jnp.where(lane_kv0, 0.0, qg)
            s = _dot(qm, kcat, NT) * scale + bias_ref[g * SWA_KV_HEADS + h]
            s = jnp.where(valid, s, NEG_INF)
            sink = sink_ref[h * SWA_GROUP + g]
            m = jnp.maximum(jnp.max(s, axis=-1, keepdims=True), sink)
            p = jnp.exp(s - m)
            den = jnp.sum(p, axis=-1, keepdims=True) + jnp.exp(sink - m)
            outs.append(_dot(p, vcat) / den)
        o_ref[:, g * LANES:(g + 1) * LANES] = jnp.where(lane_kv0, outs[0], outs[1])


def _swa_prompt(q, k, v, rel_bias, sinks, batch, seq):
    N = q.shape[0]
    Q = WINDOW
    nb = seq // Q
    qi = np.arange(Q)[:, None]
    kj = np.arange(2 * Q)[None, :]
    idx = jnp.asarray(_rel_bucket_np(qi + Q - kj))
    cur = lambda bb, n: (bb * nb + n, 0)
    prev = lambda bb, n: (bb * nb + jnp.maximum(n - 1, 0), 0)
    kv_c = pl.BlockSpec((Q, KV_WIDTH), cur)
    kv_p = pl.BlockSpec((Q, KV_WIDTH), prev)
    smem = pl.BlockSpec(memory_space=pltpu.SMEM)
    return pl.pallas_call(
        _swa_prompt_body,
        out_shape=jax.ShapeDtypeStruct((N, SWA_WIDTH), F32),
        grid=(batch, nb),
        in_specs=[smem, smem, pl.BlockSpec((Q, 2 * Q), lambda bb, n: (0, 0)),
                  pl.BlockSpec((Q, SWA_WIDTH), cur), kv_p, kv_c, kv_p, kv_c],
        out_specs=pl.BlockSpec((Q, SWA_WIDTH), cur),
        scratch_shapes=[pltpu.VMEM((SWA_HEADS, Q, 2 * Q), F32)],
        compiler_params=_params(("arbitrary", "arbitrary")),
        name="swa_prompt",
    )(rel_bias, sinks, idx, q, k, k, v, v)


def _swa_sample_body(rb_ref, sink_ref, idx_ref, q_ref, kn_ref, vn_ref, kb_ref, vb_ref, o_ref, ko_ref, vo_ref,
                     bias_ref, extra_ref):
    W = WINDOW
    G, KVH = SWA_GROUP, SWA_KV_HEADS

    @pl.when(pl.program_id(0) == 0)
    def _():
        idx = jnp.broadcast_to(idx_ref[...], (SUBLANES, W))
        row = lax.broadcasted_iota(jnp.int32, (SUBLANES, W), 0)
        acc = jnp.zeros((SUBLANES, W), F32)
        ext = jnp.zeros((SUBLANES, LANES), F32)
        lane = lax.broadcasted_iota(jnp.int32, (SUBLANES, LANES), 1)
        for h in range(KVH):
            for g in range(G):
                head = h * G + g
                r = h * G + g
                acc = jnp.where(row == r, _bias_from_buckets(idx, rb_ref, head), acc)
                ext = jnp.where((row == r) & (lane == 0), rb_ref[0, head], ext)
                ext = jnp.where((row == r) & (lane == 1), sink_ref[head], ext)
        bias_ref[...] = acc
        extra_ref[...] = ext

    TB = q_ref.shape[0]
    lane_kv0 = lax.broadcasted_iota(jnp.int32, (TB, G, LANES), 2) < HEAD_DIM
    q4 = q_ref[...]
    qrows = jnp.concatenate([jnp.where(lane_kv0, q4, 0.0), jnp.where(lane_kv0, 0.0, q4)], axis=1)
    kb = kb_ref[...]
    vb = vb_ref[...]
    kn = kn_ref[...]
    vn = vn_ref[...]
    scale = HEAD_DIM ** -0.5
    bdims = (((2,), (2,)), ((0,), (0,)))
    s = lax.dot_general(qrows.astype(BF16), kb.astype(BF16), bdims, preferred_element_type=F32) * scale
    s = s + bias_ref[...][None]
    s_self = jnp.sum(qrows * kn, axis=-1, keepdims=True) * scale + extra_ref[:, 0:1][None]
    sink = extra_ref[:, 1:2][None]
    m = jnp.maximum(jnp.maximum(jnp.max(s, axis=-1, keepdims=True), s_self), sink)
    p = jnp.exp(s - m)
    p_self = jnp.exp(s_self - m)
    den = jnp.sum(p, axis=-1, keepdims=True) + p_self + jnp.exp(sink - m)
    pv = lax.dot_general(p.astype(BF16), vb.astype(BF16), (((2,), (1,)), ((0,), (0,))), preferred_element_type=F32)
    o = (pv + p_self * vn) / den
    o_ref[...] = jnp.where(lane_kv0, o[:, 0:G], o[:, G:2 * G])

    rowmod = lax.broadcasted_iota(jnp.int32, (TB, W, KV_WIDTH), 1)
    for buf, new, out in ((kb, kn, ko_ref), (vb, vn, vo_ref)):
        rolled = pltpu.roll(buf.reshape(TB * W, KV_WIDTH), TB * W - 1, 0).reshape(TB, W, KV_WIDTH)
        out[...] = jnp.where(rowmod == W - 1, new, rolled)


def _swa_sample(q, k, v, kbuf, vbuf, rel_bias, sinks):
    B = q.shape[0]
    W = WINDOW
    tb = 16
    idx = jnp.asarray(_rel_bucket_np(W - np.arange(W))[None, :])
    q4 = q.reshape(B, SWA_GROUP, LANES)
    kn = k.reshape(B, 1, KV_WIDTH)
    vn = v.reshape(B, 1, KV_WIDTH)
    kb = kbuf.reshape(B, W, KV_WIDTH)
    vb = vbuf.reshape(B, W, KV_WIDTH)
    smem = pl.BlockSpec(memory_space=pltpu.SMEM)
    b3 = lambda i: (i, 0, 0)
    o, ko, vo = pl.pallas_call(
        _swa_sample_body,
        out_shape=(jax.ShapeDtypeStruct((B, SWA_GROUP, LANES), F32), jax.ShapeDtypeStruct((B, W, KV_WIDTH), F32),
                   jax.ShapeDtypeStruct((B, W, KV_WIDTH), F32)),
        grid=(B // tb,),
        in_specs=[smem, smem, pl.BlockSpec((1, W), lambda i: (0, 0)),
                  pl.BlockSpec((tb, SWA_GROUP, LANES), b3), pl.BlockSpec((tb, 1, KV_WIDTH), b3),
                  pl.BlockSpec((tb, 1, KV_WIDTH), b3), pl.BlockSpec((tb, W, KV_WIDTH), b3),
                  pl.BlockSpec((tb, W, KV_WIDTH), b3)],
        out_specs=(pl.BlockSpec((tb, SWA_GROUP, LANES), b3), pl.BlockSpec((tb, W, KV_WIDTH), b3),
                   pl.BlockSpec((tb, W, KV_WIDTH), b3)),
        scratch_shapes=[pltpu.VMEM((SUBLANES, W), F32), pltpu.VMEM((SUBLANES, LANES), F32)],
        compiler_params=_params(("arbitrary",)),
        name="swa_sample",
    )(rel_bias, sinks, idx, q4, kn, vn, kb, vb)
    return o.reshape(B, SWA_WIDTH), ko.reshape(kbuf.shape), vo.reshape(vbuf.shape)


def _outproj_body(per_row, tiles_per_seq, prompt_row0, x_ref, mod_ref, yw_ref, g_ref, bonus_ref, ya_ref,
                  lnw_ref, lnb_ref, hm_ref, wr_ref, wa_ref, gpost_ref, o_ref):
    D = D_MODEL
    m = _mod_rows(mod_ref, per_row, tiles_per_seq, prompt_row0)
    yw = yw_ref[...]
    hm = hm_ref[...]
    mean = _dot_hi(yw, hm)
    dv = yw - mean
    var = _dot_hi(dv * dv, hm)
    yn = dv * lax.rsqrt(var + LN_X_EPS) * lnw_ref[...] + lnb_ref[...]
    yr = (yn + bonus_ref[...]) * g_ref[...]
    mix = _dot(yr, wr_ref[...]) + _dot(ya_ref[...], wa_ref[...])
    o_ref[...] = x_ref[...] + m[:, 2 * D:3 * D] * _rms(mix, gpost_ref[...])


def _outproj(x, mod, yw, g, bonus, ya, lw, per_row, tm, tiles_per_seq, prompt_row0):
    N, D = x.shape
    W = RWKV_WIDTH
    row = lambda i: (i, 0)
    fixed = lambda i: (0, 0)
    half = pl.BlockSpec((tm, W), row)
    vec = pl.BlockSpec((1, W), fixed)
    return pl.pallas_call(
        functools.partial(_outproj_body, per_row, tiles_per_seq, prompt_row0),
        out_shape=jax.ShapeDtypeStruct((N, D), F32),
        grid=(N // tm,),
        in_specs=[pl.BlockSpec((tm, D), row), _mod_spec(mod, per_row, tm, 1), half, half, half, half, vec, vec,
                  pl.BlockSpec((W, W), fixed), pl.BlockSpec((W, D), fixed), pl.BlockSpec((W, D), fixed),
                  pl.BlockSpec((1, D), fixed)],
        out_specs=pl.BlockSpec((tm, D), row),
        compiler_params=_params(("arbitrary",)),
        name="out_proj",
    )(x, mod, yw, g, bonus, ya, lw["lnw"], lw["lnb"], lw["headmean"], lw["w_out_r"], lw["w_out_a"], lw["g_post0"])


def _ffn_body(moe, per_row, tiles_per_seq, prompt_row0, x_ref, mod_ref, gpre_ref, gpost_ref, rw_ref, rb_ref,
              wg_ref, wu_ref, wd_ref, o_ref, h_ref, acc_ref, comb_ref):
    D = D_MODEL
    e = pl.program_id(1)
    f = pl.program_id(2)
    first = (e == 0) & (f == 0)
    last = (e == pl.num_programs(1) - 1) & (f == pl.num_programs(2) - 1)

    @pl.when(first)
    def _():
        m = _mod_rows(mod_ref, per_row, tiles_per_seq, prompt_row0)
        h = _rms(x_ref[...], gpre_ref[...]) * (1.0 + m[:, D:2 * D]) + m[:, 0:D]
        h_ref[...] = h.astype(BF16)
        acc_ref[...] = jnp.zeros_like(acc_ref)
        if moe:
            logits = _dot_hi(h, rw_ref[...]) + rb_ref[...]
            lane = lax.broadcasted_iota(jnp.int32, logits.shape, 1)
            m1 = jnp.max(logits, axis=-1, keepdims=True)
            i1 = jnp.min(jnp.where(logits == m1, lane, LANES), axis=-1, keepdims=True)
            rest = jnp.where(lane == i1, -jnp.inf, logits)
            m2 = jnp.max(rest, axis=-1, keepdims=True)
            i2 = jnp.min(jnp.where(rest == m2, lane, LANES), axis=-1, keepdims=True)
            e2 = jnp.exp(m2 - m1)
            comb_ref[...] = jnp.where(lane == i1, 1.0 / (1.0 + e2), 0.0) + jnp.where(lane == i2, e2 / (1.0 + e2), 0.0)

    h = h_ref[...]
    gate = _dot(h, wg_ref[...])
    up = _dot(h, wu_ref[...])
    act = gate * _sigmoid(gate) * up
    if moe:
        comb = comb_ref[...]
        lane = lax.broadcasted_iota(jnp.int32, comb.shape, 1)
        act = act * jnp.sum(jnp.where(lane == e, comb, 0.0), axis=-1, keepdims=True)
    acc_ref[...] += _dot(act, wd_ref[...])

    @pl.when(last)
    def _():
        m = _mod_rows(mod_ref, per_row, tiles_per_seq, prompt_row0)
        o_ref[...] = x_ref[...] + m[:, 2 * D:3 * D] * _rms(acc_ref[...], gpost_ref[...])


def _ffn(x, mod, gpre, gpost, router_w, router_b, wg, wu, wd, moe, per_row, tm, tiles_per_seq, prompt_row0, tf):
    N, D = x.shape
    E, _, F = wg.shape
    row = lambda i, e, f: (i, 0)
    fixed = lambda i, e, f: (0, 0)
    return pl.pallas_call(
        functools.partial(_ffn_body, moe, per_row, tiles_per_seq, prompt_row0),
        out_shape=jax.ShapeDtypeStruct((N, D), F32),
        grid=(N // tm, E, F // tf),
        in_specs=[pl.BlockSpec((tm, D), row), _mod_spec(mod, per_row, tm, 3),
                  pl.BlockSpec((1, D), fixed), pl.BlockSpec((1, D), fixed),
                  pl.BlockSpec((D, LANES), fixed), pl.BlockSpec((1, LANES), fixed),
                  pl.BlockSpec((None, D, tf), lambda i, e, f: (e, 0, f)),
                  pl.BlockSpec((None, D, tf), lambda i, e, f: (e, 0, f)),
                  pl.BlockSpec((None, tf, D), lambda i, e, f: (e, f, 0))],
        out_specs=pl.BlockSpec((tm, D), row),
        scratch_shapes=[pltpu.VMEM((tm, D), BF16), pltpu.VMEM((tm, D), F32), pltpu.VMEM((tm, LANES), F32)],
        compiler_params=_params(("arbitrary", "arbitrary", "arbitrary")),
        name="moe_ffn" if moe else "dense_ffn",
    )(x, mod, gpre, gpost, router_w, router_b, wg, wu, wd)


def _layer_weights(p, l):
    W = RWKV_WIDTH
    heads = np.arange(W) // HEAD_DIM
    headsum = jnp.asarray((heads[:, None] == heads[None, :]).astype(np.float32))
    perm = _q_perm()
    w_in = p["w_in"][l]
    w_in = jnp.concatenate([w_in[:, :RWKV_COLS], w_in[:, RWKV_COLS + perm], w_in[:, RWKV_COLS + SWA_WIDTH:]], axis=1)
    w_out = p["w_out"][l]
    zeros = jnp.zeros((DECAY_LORA, W), F32)
    row = lambda t: t.reshape(1, -1)
    return {
        "w_in": w_in.astype(BF16),
        "w_out_r": w_out[:W].astype(BF16),
        "w_out_a": w_out[W + perm].astype(BF16),
        "mu": row(p["mu_shift"][l]),
        "wd": jnp.concatenate([p["w_decay_up"][l], zeros], axis=0),
        "wi": jnp.concatenate([zeros, p["w_iclr_up"][l]], axis=0),
        "wg": p["w_gate_up"][l],
        "dbase": row(p["decay_base"][l]), "ibase": row(p["iclr_base"][l]),
        "kk": row(p["k_k"][l]), "ka": row(p["k_a"][l]), "rk": row(p["r_k"][l]),
        "lnw": row(p["lnx_w"][l]), "lnb": row(p["lnx_b"][l]),
        "headsum": headsum, "headmean": headsum / HEAD_DIM,
        "g_pre0": row(p["norm_pre"][l, 0]), "g_pre1": row(p["norm_pre"][l, 1]),
        "g_post0": row(p["norm_post"][l, 0]), "g_post1": row(p["norm_post"][l, 1]),
    }


def _ffn_weights(p, l):
    i = l // 2
    if l % 2 == 0:
        return dict(moe=False, rw=jnp.zeros((D_MODEL, LANES), F32), rb=jnp.zeros((1, LANES), F32),
                    wg=p["ffn_w_gate"][i][None].astype(BF16), wu=p["ffn_w_up"][i][None].astype(BF16),
                    wd=p["ffn_w_down"][i][None].astype(BF16))
    rw = jnp.zeros((D_MODEL, LANES), F32).at[:, :N_EXPERTS].set(p["router_w"][i])
    rb = jnp.full((1, LANES), NEG_INF, F32).at[0, :N_EXPERTS].set(p["router_b"][i])
    return dict(moe=True, rw=rw, rb=rb, wg=p["moe_w_gate"][i].astype(BF16), wu=p["moe_w_up"][i].astype(BF16),
                wd=p["moe_w_down"][i].astype(BF16))


def _pick_tile(n, pref):
    t = min(pref, n)
    while n % t:
        t //= 2
    return t


def _ffn_tile(f):
    for t in (1408, 896, 512, 256, 128):
        if f % t == 0:
            return t
    return f


def _trunk(x3, mods, lws, fws, p, prompt, prompt_row0, state=None):
    B, T, D = x3.shape
    N = B * T
    x = x3.reshape(N, D)
    per_row = not prompt
    tm = _pick_tile(T if prompt else N, 512)
    tps = (T // tm) if prompt else 1
    depth = len(lws)
    wkv_out, shift_out, k_out, v_out = [], [], [], []
    for l in range(depth):
        lw, fw = lws[l], fws[l]
        mod0, mod1 = mods[2 * l], mods[2 * l + 1]
        pr, q, k, v = _inproj(x, mod0, lw["g_pre0"], lw["w_in"], per_row, tm, tps, prompt_row0)
        prev = pr if prompt else state["shift"][l]
        r, ld, kh, vv, a, b, g, bonus = _prep(pr, prev, lw, per_row, tm, tps)
        if prompt:
            yw, hbd = _wkv_prompt(r, ld, kh, vv, a, b, B, T)
            n_pairs = RWKV_WIDTH // PAIR
            hb = hbd.reshape(B, n_pairs, 2, HEAD_DIM, 2, HEAD_DIM)
            s_kv = jnp.stack([hb[:, :, 0, :, 0, :], hb[:, :, 1, :, 1, :]], axis=2)
            s_new = jnp.swapaxes(s_kv.reshape(B, RWKV_HEADS, HEAD_DIM, HEAD_DIM), -1, -2)
            ya = _swa_prompt(q, k, v, p["rel_bias"], p["attn_sinks"][l], B, T)
            kb = k.reshape(B, T, SWA_KV_HEADS, HEAD_DIM)[:, -WINDOW:]
            vb = v.reshape(B, T, SWA_KV_HEADS, HEAD_DIM)[:, -WINDOW:]
            last = pr.reshape(B, T, RWKV_COLS)[:, -1]
        else:
            yw, s_new = _wkv_sample(state["wkv"][l], r, ld, kh, vv, a, b)
            ya, kb, vb = _swa_sample(q, k, v, state["k"][l], state["v"][l], p["rel_bias"], p["attn_sinks"][l])
            last = pr
        x = _outproj(x, mod0, yw, g, bonus, ya, lw, per_row, tm, tps, prompt_row0)
        x = _ffn(x, mod1, lw["g_pre1"], lw["g_post1"], fw["rw"], fw["rb"], fw["wg"], fw["wu"], fw["wd"],
                 fw["moe"], per_row, tm, tps, prompt_row0, _ffn_tile(fw["wg"].shape[-1]))
        wkv_out.append(s_new)
        shift_out.append(last)
        k_out.append(kb)
        v_out.append(vb)
    return x.reshape(B, T, D), jnp.stack(wkv_out), jnp.stack(shift_out), jnp.stack(k_out), jnp.stack(v_out)


def _forward(x_prompt, x_sample, c_prompt, c_sample, state_wkv, state_shift, cache_swa_k, cache_swa_v, p):
    depth = p["w_in"].shape[0]
    Bp, Bs = c_prompt.shape[0], c_sample.shape[0]
    D = D_MODEL
    pad = (-(Bs + Bp)) % SUBLANES
    c_all = jnp.concatenate([c_sample, c_prompt, jnp.zeros((pad, D), F32)], axis=0)
    mods = _ada_all(c_all, p["ada_w"].reshape(2 * depth, D, 3 * D), p["ada_b"].reshape(2 * depth, 1, 3 * D))
    lws = [_layer_weights(p, l) for l in range(depth)]
    fws = [_ffn_weights(p, l) for l in range(depth)]
    y_p, wkv_p, shift_p, k_p, v_p = _trunk(x_prompt, mods, lws, fws, p, True, Bs)
    state = {"wkv": state_wkv, "shift": state_shift, "k": cache_swa_k, "v": cache_swa_v}
    y_s, wkv_s, shift_s, k_s, v_s = _trunk(x_sample, mods, lws, fws, p, False, Bs, state)
    return (y_p, y_s, wkv_p, shift_p, k_p, v_p, wkv_s, shift_s, k_s, v_s)


def kernel(x_prompt, x_sample, c_prompt, c_sample, state_wkv, state_shift, cache_swa_k, cache_swa_v, rel_bias, ada_w, ada_b, norm_pre, norm_post, w_in, mu_shift, w_decay_up, decay_base, w_iclr_up, iclr_base, w_gate_up, k_k, k_a, r_k, lnx_w, lnx_b, attn_sinks, w_out, ffn_w_gate, ffn_w_up, ffn_w_down, router_w, router_b, moe_w_gate, moe_w_up, moe_w_down):
    p = {"rel_bias": rel_bias, "ada_w": ada_w, "ada_b": ada_b, "norm_pre": norm_pre, "norm_post": norm_post,
         "w_in": w_in, "mu_shift": mu_shift, "w_decay_up": w_decay_up, "decay_base": decay_base,
         "w_iclr_up": w_iclr_up, "iclr_base": iclr_base, "w_gate_up": w_gate_up, "k_k": k_k, "k_a": k_a,
         "r_k": r_k.reshape(r_k.shape[0], -1), "lnx_w": lnx_w, "lnx_b": lnx_b, "attn_sinks": attn_sinks,
         "w_out": w_out, "ffn_w_gate": ffn_w_gate, "ffn_w_up": ffn_w_up, "ffn_w_down": ffn_w_down,
         "router_w": router_w, "router_b": router_b, "moe_w_gate": moe_w_gate, "moe_w_up": moe_w_up,
         "moe_w_down": moe_w_down}
    return _forward(x_prompt, x_sample, c_prompt, c_sample, state_wkv, state_shift, cache_swa_k, cache_swa_v, p)
```

```python
import functools

import numpy as np
import jax
import jax.numpy as jnp
from jax import lax
from jax.experimental import pallas as pl
from jax.experimental.pallas import tpu as pltpu

F32 = jnp.float32
BF16 = jnp.bfloat16
HIGHEST = lax.Precision.HIGHEST

D_MODEL = 1024
HEAD_DIM = 64
RWKV_WIDTH = 512
RWKV_HEADS = RWKV_WIDTH // HEAD_DIM
SWA_WIDTH = 512
SWA_HEADS = SWA_WIDTH // HEAD_DIM
SWA_KV_HEADS = 2
SWA_GROUP = SWA_HEADS // SWA_KV_HEADS
KV_WIDTH = SWA_KV_HEADS * HEAD_DIM
WINDOW = 128
DECAY_LORA = 64
ICLR_LORA = 64
GATE_LORA = 128
RWKV_COLS = 3 * RWKV_WIDTH + DECAY_LORA + ICLR_LORA + GATE_LORA
IN_COLS = RWKV_COLS + SWA_WIDTH + 2 * KV_WIDTH
LORA_OFF = 3 * RWKV_WIDTH
GATE_OFF = LORA_OFF + DECAY_LORA + ICLR_LORA
LN_X_EPS = 64e-5
RMS_EPS = 1e-6
N_BUCKETS = 32
MAX_DISTANCE = 128
N_EXPERTS = 8
NEG_INF = -1e30

LANES = 128
SUBLANES = 8
VMEM_LIMIT = 56 * 1024 * 1024

WKV_CHUNK = 64
PAIR = 2 * HEAD_DIM

NN = (((1,), (0,)), ((), ()))
NT = (((1,), (1,)), ((), ()))
TN = (((0,), (0,)), ((), ()))


def _dot(a, b, dims=NN):
    return lax.dot_general(a.astype(BF16), b.astype(BF16), dims, preferred_element_type=F32)


def _dot_hi(a, b, dims=NN):
    return lax.dot_general(a.astype(F32), b.astype(F32), dims, precision=HIGHEST, preferred_element_type=F32)


def _sigmoid(x):
    return 1.0 / (1.0 + jnp.exp(-x))


def _params(sem):
    return pltpu.CompilerParams(dimension_semantics=sem, vmem_limit_bytes=VMEM_LIMIT)


def _rms(x, g):
    return x * lax.rsqrt(jnp.mean(x * x, axis=-1, keepdims=True) + RMS_EPS) * g


def _mod_rows(mod_ref, per_row, tiles_per_seq, prompt_row0):
    if per_row:
        return mod_ref[...]
    b = pl.program_id(0) // tiles_per_seq
    return mod_ref[pl.ds(prompt_row0 + b, 1), :]


def _ada_body(c_ref, w_ref, b_ref, o_ref):
    c = c_ref[...]
    o_ref[...] = _dot(c * _sigmoid(c), w_ref[...]) + b_ref[...]


def _ada_all(c_all, ada_w, ada_b):
    R, D = c_all.shape
    n = ada_w.shape[0]
    tn = 1024
    return pl.pallas_call(
        _ada_body,
        out_shape=jax.ShapeDtypeStruct((n, R, 3 * D), F32),
        grid=(n, 3 * D // tn),
        in_specs=[pl.BlockSpec((R, D), lambda i, j: (0, 0)),
                  pl.BlockSpec((None, D, tn), lambda i, j: (i, 0, j)),
                  pl.BlockSpec((None, 1, tn), lambda i, j: (i, 0, j))],
        out_specs=pl.BlockSpec((None, R, tn), lambda i, j: (i, 0, j)),
        compiler_params=_params(("arbitrary", "arbitrary")),
        name="ada_mod",
    )(c_all, ada_w, ada_b)


def _inproj_body(per_row, tiles_per_seq, prompt_row0, x_ref, mod_ref, g_ref, w_ref, pr_ref, q_ref, k_ref, v_ref):
    D = D_MODEL
    m = _mod_rows(mod_ref, per_row, tiles_per_seq, prompt_row0)
    h = _rms(x_ref[...], g_ref[...]) * (1.0 + m[:, D:2 * D]) + m[:, 0:D]
    proj = _dot(h, w_ref[...])
    pr_ref[...] = proj[:, 0:RWKV_COLS]
    q_ref[...] = proj[:, RWKV_COLS:RWKV_COLS + SWA_WIDTH]
    k_ref[...] = proj[:, RWKV_COLS + SWA_WIDTH:RWKV_COLS + SWA_WIDTH + KV_WIDTH]
    v_ref[...] = proj[:, RWKV_COLS + SWA_WIDTH + KV_WIDTH:IN_COLS]


def _mod_spec(mod, per_row, tm, nargs):
    R = mod.shape[0]
    if per_row:
        return pl.BlockSpec((tm, 3 * D_MODEL), lambda i, *_: (0, 0))
    return pl.BlockSpec((R, 3 * D_MODEL), lambda i, *_: (0, 0))


def _inproj(x, mod, g, w, per_row, tm, tiles_per_seq, prompt_row0):
    N, D = x.shape
    row = lambda i: (i, 0)
    fixed = lambda i: (0, 0)
    return pl.pallas_call(
        functools.partial(_inproj_body, per_row, tiles_per_seq, prompt_row0),
        out_shape=(jax.ShapeDtypeStruct((N, RWKV_COLS), F32), jax.ShapeDtypeStruct((N, SWA_WIDTH), F32),
                   jax.ShapeDtypeStruct((N, KV_WIDTH), F32), jax.ShapeDtypeStruct((N, KV_WIDTH), F32)),
        grid=(N // tm,),
        in_specs=[pl.BlockSpec((tm, D), row), _mod_spec(mod, per_row, tm, 1),
                  pl.BlockSpec((1, D), fixed), pl.BlockSpec((D, IN_COLS), fixed)],
        out_specs=(pl.BlockSpec((tm, RWKV_COLS), row), pl.BlockSpec((tm, SWA_WIDTH), row),
                   pl.BlockSpec((tm, KV_WIDTH), row), pl.BlockSpec((tm, KV_WIDTH), row)),
        compiler_params=_params(("arbitrary",)),
        name="in_proj",
    )(x, mod, g, w)


def _prep_body(per_row, tiles_per_seq, pr_ref, prev_ref, mu_ref, wd_ref, wi_ref, wg_ref, dbase_ref, ibase_ref,
               kk_ref, ka_ref, rk_ref, hs_ref,
               r_ref, ld_ref, k_ref, v_ref, a_ref, b_ref, g_ref, bonus_ref):
    W = RWKV_WIDTH
    pr = pr_ref[...]
    if per_row:
        shifted = prev_ref[...]
    else:
        first = (pl.program_id(0) % tiles_per_seq) == 0
        carry = jnp.where(first, 0.0, prev_ref[SUBLANES - 1:SUBLANES, :])
        rows = lax.broadcasted_iota(jnp.int32, pr.shape, 0)
        shifted = jnp.where(rows == 0, carry, pltpu.roll(pr, 1, 0))
    xs = pr + (shifted - pr) * mu_ref[...]
    r = xs[:, 0:W]
    k = xs[:, W:2 * W]
    v = xs[:, 2 * W:3 * W]
    lora = xs[:, LORA_OFF:GATE_OFF]
    gl = xs[:, GATE_OFF:RWKV_COLS]
    z = dbase_ref[...] + _dot_hi(jnp.tanh(lora), wd_ref[...])
    ld = -float(np.exp(-0.5)) * _sigmoid(z)
    iclr = _sigmoid(ibase_ref[...] + _dot_hi(lora, wi_ref[...]))
    g = _dot_hi(_sigmoid(gl), wg_ref[...])
    hs = hs_ref[...]
    kk = k * kk_ref[...]
    kk = kk / jnp.maximum(jnp.sqrt(_dot_hi(kk * kk, hs)), 1e-12)
    kh = k * (1.0 + (iclr - 1.0) * ka_ref[...])
    bonus = _dot_hi(r * kh * rk_ref[...], hs) * v
    r_ref[...] = r
    ld_ref[...] = ld
    k_ref[...] = kh
    v_ref[...] = v
    a_ref[...] = -kk
    b_ref[...] = kk * iclr
    g_ref[...] = g
    bonus_ref[...] = bonus


def _prep(pr, prev, lw, per_row, tm, tiles_per_seq):
    N = pr.shape[0]
    W = RWKV_WIDTH
    row = lambda i: (i, 0)
    fixed = lambda i: (0, 0)
    if per_row:
        prev_spec = pl.BlockSpec((tm, RWKV_COLS), row)
    else:
        per = tm // SUBLANES
        prev_spec = pl.BlockSpec((SUBLANES, RWKV_COLS), lambda i: (jnp.maximum(i * per - 1, 0), 0))
    vec = pl.BlockSpec((1, W), fixed)
    out = jax.ShapeDtypeStruct((N, W), F32)
    return pl.pallas_call(
        functools.partial(_prep_body, per_row, tiles_per_seq),
        out_shape=(out,) * 8,
        grid=(N // tm,),
        in_specs=[pl.BlockSpec((tm, RWKV_COLS), row), prev_spec, pl.BlockSpec((1, RWKV_COLS), fixed),
                  pl.BlockSpec((LANES, W), fixed), pl.BlockSpec((LANES, W), fixed), pl.BlockSpec((GATE_LORA, W), fixed),
                  vec, vec, vec, vec, vec, pl.BlockSpec((W, W), fixed)],
        out_specs=(pl.BlockSpec((tm, W), row),) * 8,
        compiler_params=_params(("arbitrary",)),
        name="rwkv_prep",
    )(pr, prev, lw["mu"], lw["wd"], lw["wi"], lw["wg"], lw["dbase"], lw["ibase"], lw["kk"], lw["ka"], lw["rk"],
      lw["headsum"])


def _stack_heads(x, lane_head0):
    return jnp.concatenate([jnp.where(lane_head0, x, 0.0), jnp.where(lane_head0, 0.0, x)], axis=0)


def _fold_heads(x):
    c = x.shape[0] // 2
    return x[0:c] + x[c:2 * c]


def _dots(xs, ys, dims=NN):
    return [_dot(x, y, dims) for x, y in zip(xs, ys)]


def _unit_lower_inverse(ns, same16, eye):
    nd = [jnp.where(same16, n, 0.0) for n in ns]
    no = [n - d for n, d in zip(ns, nd)]
    n2 = _dots(nd, nd)
    n4 = _dots(n2, n2)
    n8 = _dots(n4, n4)
    td = [eye + d for d in nd]
    for pw in (n2, n4, n8):
        td = [t + u for t, u in zip(td, _dots(td, pw))]
    q = _dots(td, no)
    q2 = _dots(q, q)
    z = [eye + x for x in q]
    z = [t + u for t, u in zip(z, _dots(z, q2))]
    return _dots(z, td)


def _wkv_chunk_body(chunks, r_ref, ld_ref, k_ref, v_ref, a_ref, b_ref, y_ref, s_ref, h_ref):
    C = WKV_CHUNK
    n_pairs = RWKV_WIDTH // PAIR

    @pl.when(pl.program_id(1) == 0)
    def _():
        h_ref[...] = jnp.zeros_like(h_ref)

    ri = lax.broadcasted_iota(jnp.int32, (PAIR, PAIR), 0)
    ci = lax.broadcasted_iota(jnp.int32, (PAIR, PAIR), 1)
    same_head = (ri // C) == (ci // C)
    strict_lower = same_head & (ci < ri)
    incl_lower = same_head & (ci <= ri)
    same16 = (ri // 16) == (ci // 16)
    eye_b = ri == ci
    eye = jnp.where(eye_b, 1.0, 0.0)
    tri = jnp.where(lax.broadcasted_iota(jnp.int32, (C, C), 1) <= lax.broadcasted_iota(jnp.int32, (C, C), 0), 1.0, 0.0)
    lane_head0 = lax.broadcasted_iota(jnp.int32, (C, PAIR), 1) < HEAD_DIM
    zeros = jnp.zeros((PAIR, PAIR), F32)

    cat0 = lambda x, y: jnp.concatenate([x, y], axis=0)
    cat1 = lambda x, y: jnp.concatenate([x, y], axis=1)

    xa, xr, v_st, bh_st, kh_st, yb, yk, p_all = [], [], [], [], [], [], [], []
    for c in range(chunks):
        rows = slice(c * C, (c + 1) * C)
        ld = ld_ref[rows, :]
        cum = _dot_hi(tri, ld)
        last = cum[C - 1:C, :]
        p_inv = jnp.exp(-cum)
        p_tail = jnp.exp(last - cum)
        p_end = jnp.exp(last)
        a_t = a_ref[rows, :] * jnp.exp(cum - ld)
        r_t = r_ref[rows, :] * jnp.exp(cum)
        b_raw = b_ref[rows, :]
        k_raw = k_ref[rows, :]
        b_t = b_raw * p_inv
        k_t = k_raw * p_inv
        b_h = b_raw * p_tail
        k_h = k_raw * p_tail
        v_all = v_ref[rows, :]
        for j in range(n_pairs):
            lanes = slice(j * PAIR, (j + 1) * PAIR)
            xa.append(_stack_heads(a_t[:, lanes], lane_head0))
            xr.append(_stack_heads(r_t[:, lanes], lane_head0))
            v_st.append(_stack_heads(v_all[:, lanes], lane_head0))
            bh_st.append(_stack_heads(b_h[:, lanes], lane_head0))
            kh_st.append(_stack_heads(k_h[:, lanes], lane_head0))
            yb.append(cat0(b_t[:, lanes], b_t[:, lanes]))
            yk.append(cat0(k_t[:, lanes], k_t[:, lanes]))
            p_all.append(p_end[:, lanes])

    gram = _dots([cat0(x, y) for x, y in zip(xa, xr)], [cat0(x, y) for x, y in zip(yb, yk)], NT)
    n_mat = [jnp.where(strict_lower, g[0:PAIR, 0:PAIR], 0.0) for g in gram]
    m_mat = [jnp.where(strict_lower, g[0:PAIR, PAIR:2 * PAIR], 0.0) for g in gram]
    a_rbk = [cat1(jnp.where(incl_lower, g[PAIR:2 * PAIR, 0:PAIR], 0.0),
                  jnp.where(incl_lower, g[PAIR:2 * PAIR, PAIR:2 * PAIR], 0.0)) for g in gram]
    t_inv = _unit_lower_inverse(n_mat, same16, eye)
    mv = _dots(m_mat, v_st)
    tx = _dots(t_inv, [cat1(x, y) for x, y in zip(xa, mv)])
    rhs = [cat0(t, cat1(zeros, v)) for t, v in zip(tx, v_st)]
    ry = _dots(a_rbk, rhs)
    pp = _dots([cat0(x, y) for x, y in zip(bh_st, kh_st)], rhs, TN)

    for c in range(chunks):
        us = [c * n_pairs + j for j in range(n_pairs)]
        h0 = [h_ref[j] for j in range(n_pairs)]
        r_bar = [_fold_heads(xr[u] + ry[u][:, 0:PAIR]) for u in us]
        phi = [pp[u][:, 0:PAIR] + jnp.where(eye_b, p_all[u], 0.0) for u in us]
        ys = _dots(r_bar, h0)
        hs = _dots(phi, h0)
        for j, u in enumerate(us):
            y_ref[c * C:(c + 1) * C, j * PAIR:(j + 1) * PAIR] = ys[j] + _fold_heads(ry[u][:, PAIR:2 * PAIR])
            h_ref[j] = hs[j] + pp[u][:, PAIR:2 * PAIR]

    @pl.when(pl.program_id(1) == pl.num_programs(1) - 1)
    def _():
        s_ref[...] = h_ref[...]


def _wkv_prompt(r, ld, k, v, a, b, batch, seq):
    N, W = r.shape
    chunks = 4
    tt = chunks * WKV_CHUNK
    steps = seq // tt
    n_pairs = W // PAIR
    row = lambda bb, t: (bb * steps + t, 0)
    spec = pl.BlockSpec((tt, W), row)
    return pl.pallas_call(
        functools.partial(_wkv_chunk_body, chunks),
        out_shape=(jax.ShapeDtypeStruct((N, W), F32), jax.ShapeDtypeStruct((batch, n_pairs, PAIR, PAIR), F32)),
        grid=(batch, steps),
        in_specs=[spec] * 6,
        out_specs=(spec, pl.BlockSpec((None, n_pairs, PAIR, PAIR), lambda bb, t: (bb, 0, 0, 0))),
        scratch_shapes=[pltpu.VMEM((n_pairs, PAIR, PAIR), F32)],
        compiler_params=_params(("arbitrary", "arbitrary")),
        name="wkv_chunk_scan",
    )(r, ld, k, v, a, b)


def _split3(x):
    hi = x.astype(BF16)
    r1 = x - hi.astype(F32)
    mid = r1.astype(BF16)
    lo = (r1 - mid.astype(F32)).astype(BF16)
    return hi, mid, lo


def _select_dot(x, sel):
    hi, mid, lo = _split3(x)
    d = lambda t: lax.dot_general(t, sel, NN, preferred_element_type=F32)
    return d(hi) + d(mid) + d(lo)


def _wkv_step_body(s_ref, r_ref, ld_ref, k_ref, v_ref, a_ref, b_ref, exp_ref, red_ref, y_ref, so_ref):
    HD = HEAD_DIM
    nrep = HD * HD // LANES
    lane = lax.broadcasted_iota(jnp.int32, r_ref.shape, 1)
    low = lane < HD
    y = jnp.zeros(r_ref.shape, F32)
    for hh in range(2):
        def tiled(ref, fn=None):
            x = ref[...]
            if fn is not None:
                x = fn(x)
            sw = pltpu.roll(x, HD, 1)
            both = jnp.where(low, x, sw) if hh == 0 else jnp.where(low, sw, x)
            return jnp.tile(both, (1, nrep))
        cols = slice(hh * HD * HD, (hh + 1) * HD * HD)
        s = s_ref[:, cols]
        expand = exp_ref[hh]
        reduce_ = red_ref[hh]
        sa = _select_dot(s * tiled(a_ref), reduce_)
        s_new = (s * tiled(ld_ref, jnp.exp) + _select_dot(sa, expand) * tiled(b_ref)
                 + _select_dot(v_ref[...], expand) * tiled(k_ref))
        so_ref[:, cols] = s_new
        y = y + _select_dot(s_new * tiled(r_ref), reduce_)
    y_ref[...] = y


def _wkv_step_consts():
    HD = HEAD_DIM
    expand = np.zeros((2, PAIR, HD * HD), np.float32)
    for hh in range(2):
        for vv in range(HD):
            expand[hh, hh * HD + vv, vv * HD:(vv + 1) * HD] = 1.0
    return jnp.asarray(expand, BF16), jnp.asarray(expand.transpose(0, 2, 1), BF16)


def _wkv_sample(state, r, ld, k, v, a, b):
    B = state.shape[0]
    W = RWKV_WIDTH
    HD2 = HEAD_DIM * HEAD_DIM
    expand, reduce_ = _wkv_step_consts()
    s2 = state.reshape(B, RWKV_HEADS * HD2)
    st_spec = pl.BlockSpec((B, 2 * HD2), lambda j: (0, j))
    vec = pl.BlockSpec((B, PAIR), lambda j: (0, j))
    y, s_new = pl.pallas_call(
        _wkv_step_body,
        out_shape=(jax.ShapeDtypeStruct((B, W), F32), jax.ShapeDtypeStruct((B, RWKV_HEADS * HD2), F32)),
        grid=(W // PAIR,),
        in_specs=[st_spec] + [vec] * 6 + [pl.BlockSpec((2, PAIR, HD2), lambda j: (0, 0, 0)),
                                          pl.BlockSpec((2, HD2, PAIR), lambda j: (0, 0, 0))],
        out_specs=(vec, st_spec),
        compiler_params=_params(("arbitrary",)),
        name="wkv_step",
    )(s2, r, ld, k, v, a, b, expand, reduce_)
    return y, s_new.reshape(state.shape)


def _q_perm():
    return np.array([(h * SWA_GROUP + g) * HEAD_DIM + d for g in range(SWA_GROUP) for h in range(SWA_KV_HEADS)
                     for d in range(HEAD_DIM)], np.int32)


def _rel_bucket_np(dist):
    max_exact = N_BUCKETS // 2
    d = np.maximum(dist, 0)
    ratio = np.log(np.maximum(d, 1).astype(np.float32) / np.float32(max_exact)) / np.float32(
        np.log(MAX_DISTANCE / max_exact))
    large = np.minimum(max_exact + (ratio.astype(np.float32) * np.float32(N_BUCKETS - max_exact)).astype(np.int32),
                       N_BUCKETS - 1)
    return np.where(d < max_exact, d, large).astype(np.int32)


def _bias_from_buckets(idx, rb_ref, head):
    acc = jnp.zeros(idx.shape, F32)
    for bk in range(N_BUCKETS):
        acc = jnp.where(idx == bk, rb_ref[bk, head], acc)
    return acc


def _swa_prompt_body(rb_ref, sink_ref, idx_ref, q_ref, kp_ref, kc_ref, vp_ref, vc_ref, o_ref, bias_ref):
    Q = WINDOW
    first = (pl.program_id(0) == 0) & (pl.program_id(1) == 0)

    @pl.when(first)
    def _():
        idx = idx_ref[...]
        for g in range(SWA_GROUP):
            for h in range(SWA_KV_HEADS):
                bias_ref[g * SWA_KV_HEADS + h] = _bias_from_buckets(idx, rb_ref, h * SWA_GROUP + g)

    n = pl.program_id(1)
    qi = lax.broadcasted_iota(jnp.int32, (Q, 2 * Q), 0)
    kj = lax.broadcasted_iota(jnp.int32, (Q, 2 * Q), 1)
    valid = ((kj < Q) & (kj >= qi) & (n > 0)) | ((kj >= Q) & ((kj - Q) <= qi))
    lane_kv0 = lax.broadcasted_iota(jnp.int32, (Q, LANES), 1) < HEAD_DIM
    kcat = jnp.concatenate([kp_ref[...], kc_ref[...]], axis=0).astype(BF16)
    vcat = jnp.concatenate([vp_ref[...], vc_ref[...]], axis=0).astype(BF16)
    scale = HEAD_DIM ** -0.5
    for g in range(SWA_GROUP):
        qg = q_ref[:, g * LANES:(g + 1) * LANES]
        outs = []
        for h in range(SWA_KV_HEADS):
            qm = jnp.where(lane_kv0, qg, 0.0) if h == 0 else jnp.where(lane_kv0, 0.0, qg)
            s = _dot(qm, kcat, NT) * scale + bias_ref[g * SWA_KV_HEADS + h]
            s = jnp.where(valid, s, NEG_INF)
            sink = sink_ref[h * SWA_GROUP + g]
            m = jnp.maximum(jnp.max(s, axis=-1, keepdims=True), sink)
            p = jnp.exp(s - m)
            den = jnp.sum(p, axis=-1, keepdims=True) + jnp.exp(sink - m)
            outs.append(_dot(p, vcat) / den)
        o_ref[:, g * LANES:(g + 1) * LANES] = jnp.where(lane_kv0, outs[0], outs[1])


def _swa_prompt(q, k, v, rel_bias, sinks, batch, seq):
    N = q.shape[0]
    Q = WINDOW
    nb = seq // Q
    qi = np.arange(Q)[:, None]
    kj = np.arange(2 * Q)[None, :]
    idx = jnp.asarray(_rel_bucket_np(qi + Q - kj))
    cur = lambda bb, n: (bb * nb + n, 0)
    prev = lambda bb, n: (bb * nb + jnp.maximum(n - 1, 0), 0)
    kv_c = pl.BlockSpec((Q, KV_WIDTH), cur)
    kv_p = pl.BlockSpec((Q, KV_WIDTH), prev)
    smem = pl.BlockSpec(memory_space=pltpu.SMEM)
    return pl.pallas_call(
        _swa_prompt_body,
        out_shape=jax.ShapeDtypeStruct((N, SWA_WIDTH), F32),
        grid=(batch, nb),
        in_specs=[smem, smem, pl.BlockSpec((Q, 2 * Q), lambda bb, n: (0, 0)),
                  pl.BlockSpec((Q, SWA_WIDTH), cur), kv_p, kv_c, kv_p, kv_c],
        out_specs=pl.BlockSpec((Q, SWA_WIDTH), cur),
        scratch_shapes=[pltpu.VMEM((SWA_HEADS, Q, 2 * Q), F32)],
        compiler_params=_params(("arbitrary", "arbitrary")),
        name="swa_prompt",
    )(rel_bias, sinks, idx, q, k, k, v, v)


def _swa_sample_body(rb_ref, sink_ref, idx_ref, q_ref, kn_ref, vn_ref, kb_ref, vb_ref, o_ref, ko_ref, vo_ref,
                     bias_ref, extra_ref):
    W = WINDOW
    G, KVH = SWA_GROUP, SWA_KV_HEADS

    @pl.when(pl.program_id(0) == 0)
    def _():
        idx = jnp.broadcast_to(idx_ref[...], (SUBLANES, W))
        row = lax.broadcasted_iota(jnp.int32, (SUBLANES, W), 0)
        acc = jnp.zeros((SUBLANES, W), F32)
        ext = jnp.zeros((SUBLANES, LANES), F32)
        lane = lax.broadcasted_iota(jnp.int32, (SUBLANES, LANES), 1)
        for h in range(KVH):
            for g in range(G):
                head = h * G + g
                r = h * G + g
                acc = jnp.where(row == r, _bias_from_buckets(idx, rb_ref, head), acc)
                ext = jnp.where((row == r) & (lane == 0), rb_ref[0, head], ext)
                ext = jnp.where((row == r) & (lane == 1), sink_ref[head], ext)
        bias_ref[...] = acc
        extra_ref[...] = ext

    TB = q_ref.shape[0]
    lane_kv0 = lax.broadcasted_iota(jnp.int32, (TB, G, LANES), 2) < HEAD_DIM
    q4 = q_ref[...]
    qrows = jnp.concatenate([jnp.where(lane_kv0, q4, 0.0), jnp.where(lane_kv0, 0.0, q4)], axis=1)
    kb = kb_ref[...]
    vb = vb_ref[...]
    kn = kn_ref[...]
    vn = vn_ref[...]
    scale = HEAD_DIM ** -0.5
    bdims = (((2,), (2,)), ((0,), (0,)))
    s = lax.dot_general(qrows.astype(BF16), kb.astype(BF16), bdims, preferred_element_type=F32) * scale
    s = s + bias_ref[...][None]
    s_self = jnp.sum(qrows * kn, axis=-1, keepdims=True) * scale + extra_ref[:, 0:1][None]
    sink = extra_ref[:, 1:2][None]
    m = jnp.maximum(jnp.maximum(jnp.max(s, axis=-1, keepdims=True), s_self), sink)
    p = jnp.exp(s - m)
    p_self = jnp.exp(s_self - m)
    den = jnp.sum(p, axis=-1, keepdims=True) + p_self + jnp.exp(sink - m)
    pv = lax.dot_general(p.astype(BF16), vb.astype(BF16), (((2,), (1,)), ((0,), (0,))), preferred_element_type=F32)
    o = (pv + p_self * vn) / den
    o_ref[...] = jnp.where(lane_kv0, o[:, 0:G], o[:, G:2 * G])

    rowmod = lax.broadcasted_iota(jnp.int32, (TB, W, KV_WIDTH), 1)
    for buf, new, out in ((kb, kn, ko_ref), (vb, vn, vo_ref)):
        rolled = pltpu.roll(buf.reshape(TB * W, KV_WIDTH), TB * W - 1, 0).reshape(TB, W, KV_WIDTH)
        out[...] = jnp.where(rowmod == W - 1, new, rolled)


def _swa_sample(q, k, v, kbuf, vbuf, rel_bias, sinks):
    B = q.shape[0]
    W = WINDOW
    tb = 16
    idx = jnp.asarray(_rel_bucket_np(W - np.arange(W))[None, :])
    q4 = q.reshape(B, SWA_GROUP, LANES)
    kn = k.reshape(B, 1, KV_WIDTH)
    vn = v.reshape(B, 1, KV_WIDTH)
    kb = kbuf.reshape(B, W, KV_WIDTH)
    vb = vbuf.reshape(B, W, KV_WIDTH)
    smem = pl.BlockSpec(memory_space=pltpu.SMEM)
    b3 = lambda i: (i, 0, 0)
    o, ko, vo = pl.pallas_call(
        _swa_sample_body,
        out_shape=(jax.ShapeDtypeStruct((B, SWA_GROUP, LANES), F32), jax.ShapeDtypeStruct((B, W, KV_WIDTH), F32),
                   jax.ShapeDtypeStruct((B, W, KV_WIDTH), F32)),
        grid=(B // tb,),
        in_specs=[smem, smem, pl.BlockSpec((1, W), lambda i: (0, 0)),
                  pl.BlockSpec((tb, SWA_GROUP, LANES), b3), pl.BlockSpec((tb, 1, KV_WIDTH), b3),
                  pl.BlockSpec((tb, 1, KV_WIDTH), b3), pl.BlockSpec((tb, W, KV_WIDTH), b3),
                  pl.BlockSpec((tb, W, KV_WIDTH), b3)],
        out_specs=(pl.BlockSpec((tb, SWA_GROUP, LANES), b3), pl.BlockSpec((tb, W, KV_WIDTH), b3),
                   pl.BlockSpec((tb, W, KV_WIDTH), b3)),
        scratch_shapes=[pltpu.VMEM((SUBLANES, W), F32), pltpu.VMEM((SUBLANES, LANES), F32)],
        compiler_params=_params(("arbitrary",)),
        name="swa_sample",
    )(rel_bias, sinks, idx, q4, kn, vn, kb, vb)
    return o.reshape(B, SWA_WIDTH), ko.reshape(kbuf.shape), vo.reshape(vbuf.shape)


def _outproj_body(per_row, tiles_per_seq, prompt_row0, x_ref, mod_ref, yw_ref, g_ref, bonus_ref, ya_ref,
                  lnw_ref, lnb_ref, hm_ref, wr_ref, wa_ref, gpost_ref, o_ref):
    D = D_MODEL
    m = _mod_rows(mod_ref, per_row, tiles_per_seq, prompt_row0)
    yw = yw_ref[...]
    hm = hm_ref[...]
    mean = _dot_hi(yw, hm)
    dv = yw - mean
    var = _dot_hi(dv * dv, hm)
    yn = dv * lax.rsqrt(var + LN_X_EPS) * lnw_ref[...] + lnb_ref[...]
    yr = (yn + bonus_ref[...]) * g_ref[...]
    mix = _dot(yr, wr_ref[...]) + _dot(ya_ref[...], wa_ref[...])
    o_ref[...] = x_ref[...] + m[:, 2 * D:3 * D] * _rms(mix, gpost_ref[...])


def _outproj(x, mod, yw, g, bonus, ya, lw, per_row, tm, tiles_per_seq, prompt_row0):
    N, D = x.shape
    W = RWKV_WIDTH
    row = lambda i: (i, 0)
    fixed = lambda i: (0, 0)
    half = pl.BlockSpec((tm, W), row)
    vec = pl.BlockSpec((1, W), fixed)
    return pl.pallas_call(
        functools.partial(_outproj_body, per_row, tiles_per_seq, prompt_row0),
        out_shape=jax.ShapeDtypeStruct((N, D), F32),
        grid=(N // tm,),
        in_specs=[pl.BlockSpec((tm, D), row), _mod_spec(mod, per_row, tm, 1), half, half, half, half, vec, vec,
                  pl.BlockSpec((W, W), fixed), pl.BlockSpec((W, D), fixed), pl.BlockSpec((W, D), fixed),
                  pl.BlockSpec((1, D), fixed)],
        out_specs=pl.BlockSpec((tm, D), row),
        compiler_params=_params(("arbitrary",)),
        name="out_proj",
    )(x, mod, yw, g, bonus, ya, lw["lnw"], lw["lnb"], lw["headmean"], lw["w_out_r"], lw["w_out_a"], lw["g_post0"])


def _ffn_body(moe, per_row, tiles_per_seq, prompt_row0, x_ref, mod_ref, gpre_ref, gpost_ref, rw_ref, rb_ref,
              wg_ref, wu_ref, wd_ref, o_ref, h_ref, acc_ref, comb_ref):
    D = D_MODEL
    e = pl.program_id(1)
    f = pl.program_id(2)
    first = (e == 0) & (f == 0)
    last = (e == pl.num_programs(1) - 1) & (f == pl.num_programs(2) - 1)

    @pl.when(first)
    def _():
        m = _mod_rows(mod_ref, per_row, tiles_per_seq, prompt_row0)
        h = _rms(x_ref[...], gpre_ref[...]) * (1.0 + m[:, D:2 * D]) + m[:, 0:D]
        h_ref[...] = h.astype(BF16)
        acc_ref[...] = jnp.zeros_like(acc_ref)
        if moe:
            logits = _dot_hi(h, rw_ref[...]) + rb_ref[...]
            lane = lax.broadcasted_iota(jnp.int32, logits.shape, 1)
            m1 = jnp.max(logits, axis=-1, keepdims=True)
            i1 = jnp.min(jnp.where(logits == m1, lane, LANES), axis=-1, keepdims=True)
            rest = jnp.where(lane == i1, -jnp.inf, logits)
            m2 = jnp.max(rest, axis=-1, keepdims=True)
            i2 = jnp.min(jnp.where(rest == m2, lane, LANES), axis=-1, keepdims=True)
            e2 = jnp.exp(m2 - m1)
            comb_ref[...] = jnp.where(lane == i1, 1.0 / (1.0 + e2), 0.0) + jnp.where(lane == i2, e2 / (1.0 + e2), 0.0)

    h = h_ref[...]
    gate = _dot(h, wg_ref[...])
    up = _dot(h, wu_ref[...])
    act = gate * _sigmoid(gate) * up
    if moe:
        comb = comb_ref[...]
        lane = lax.broadcasted_iota(jnp.int32, comb.shape, 1)
        act = act * jnp.sum(jnp.where(lane == e, comb, 0.0), axis=-1, keepdims=True)
    acc_ref[...] += _dot(act, wd_ref[...])

    @pl.when(last)
    def _():
        m = _mod_rows(mod_ref, per_row, tiles_per_seq, prompt_row0)
        o_ref[...] = x_ref[...] + m[:, 2 * D:3 * D] * _rms(acc_ref[...], gpost_ref[...])


def _ffn(x, mod, gpre, gpost, router_w, router_b, wg, wu, wd, moe, per_row, tm, tiles_per_seq, prompt_row0, tf):
    N, D = x.shape
    E, _, F = wg.shape
    row = lambda i, e, f: (i, 0)
    fixed = lambda i, e, f: (0, 0)
    return pl.pallas_call(
        functools.partial(_ffn_body, moe, per_row, tiles_per_seq, prompt_row0),
        out_shape=jax.ShapeDtypeStruct((N, D), F32),
        grid=(N // tm, E, F // tf),
        in_specs=[pl.BlockSpec((tm, D), row), _mod_spec(mod, per_row, tm, 3),
                  pl.BlockSpec((1, D), fixed), pl.BlockSpec((1, D), fixed),
                  pl.BlockSpec((D, LANES), fixed), pl.BlockSpec((1, LANES), fixed),
                  pl.BlockSpec((None, D, tf), lambda i, e, f: (e, 0, f)),
                  pl.BlockSpec((None, D, tf), lambda i, e, f: (e, 0, f)),
                  pl.BlockSpec((None, tf, D), lambda i, e, f: (e, f, 0))],
        out_specs=pl.BlockSpec((tm, D), row),
        scratch_shapes=[pltpu.VMEM((tm, D), BF16), pltpu.VMEM((tm, D), F32), pltpu.VMEM((tm, LANES), F32)],
        compiler_params=_params(("arbitrary", "arbitrary", "arbitrary")),
        name="moe_ffn" if moe else "dense_ffn",
    )(x, mod, gpre, gpost, router_w, router_b, wg, wu, wd)


ROW_TILE = D_MODEL // LANES
EXPERT_TILE = 512
INFO_E1, INFO_E2, INFO_P1, INFO_P2, INFO_POS1, INFO_POS2 = range(6)


def _top2(logits):
    lane = lax.broadcasted_iota(jnp.int32, logits.shape, 1)
    m1 = jnp.max(logits, axis=-1, keepdims=True)
    i1 = jnp.min(jnp.where(logits == m1, lane, LANES), axis=-1, keepdims=True)
    rest = jnp.where(lane == i1, -jnp.inf, logits)
    m2 = jnp.max(rest, axis=-1, keepdims=True)
    i2 = jnp.min(jnp.where(rest == m2, lane, LANES), axis=-1, keepdims=True)
    e2 = jnp.exp(m2 - m1)
    return lane, i1, i2, 1.0 / (1.0 + e2), e2 / (1.0 + e2)


def _to_row_tiles(ref, x):
    rows = x.shape[0]
    for c in range(ROW_TILE):
        ref[pl.ds(c, rows, stride=ROW_TILE), :] = x[:, c * LANES:(c + 1) * LANES]


def _from_row_tiles(ref, row0, rows):
    return jnp.concatenate([ref[pl.ds(row0 * ROW_TILE + c, rows, stride=ROW_TILE), :] for c in range(ROW_TILE)],
                           axis=1)


def _route_body(tiles_per_seq, prompt_row0, x_ref, mod_ref, gpre_ref, rw_ref, rb_ref, h_ref, info_ref, cnt_ref,
                base_ref):
    D = D_MODEL

    @pl.when(pl.program_id(0) == 0)
    def _():
        base_ref[...] = jnp.zeros_like(base_ref)

    m = _mod_rows(mod_ref, False, tiles_per_seq, prompt_row0)
    h = _rms(x_ref[...], gpre_ref[...]) * (1.0 + m[:, D:2 * D]) + m[:, 0:D]
    _to_row_tiles(h_ref, h)
    lane, i1, i2, p1, p2 = _top2(_dot_hi(h, rw_ref[...]) + rb_ref[...])
    onehot = jnp.where((lane == i1) | (lane == i2), 1.0, 0.0)
    T = h.shape[0]
    before = lax.broadcasted_iota(jnp.int32, (T, T), 1) < lax.broadcasted_iota(jnp.int32, (T, T), 0)
    rank = _dot(jnp.where(before, 1.0, 0.0), onehot) + base_ref[0:1, :]
    pos1 = jnp.sum(jnp.where(lane == i1, rank, 0.0), axis=-1, keepdims=True)
    pos2 = jnp.sum(jnp.where(lane == i2, rank, 0.0), axis=-1, keepdims=True)
    info = jnp.zeros(onehot.shape, F32)
    for col, val in ((INFO_E1, i1.astype(F32)), (INFO_E2, i2.astype(F32)), (INFO_P1, p1), (INFO_P2, p2),
                     (INFO_POS1, pos1), (INFO_POS2, pos2)):
        info = jnp.where(lane == col, val, info)
    info_ref[...] = info
    total = base_ref[...] + jnp.sum(onehot, axis=0, keepdims=True)
    base_ref[...] = total
    cnt_ref[...] = total


def _moe_route(x, mod, gpre, rw, rb, tm, tiles_per_seq, prompt_row0):
    N, D = x.shape
    row = lambda i: (i, 0)
    fixed = lambda i: (0, 0)
    return pl.pallas_call(
        functools.partial(_route_body, tiles_per_seq, prompt_row0),
        out_shape=(jax.ShapeDtypeStruct((N * ROW_TILE, LANES), F32), jax.ShapeDtypeStruct((N, LANES), F32),
                   jax.ShapeDtypeStruct((SUBLANES, LANES), F32)),
        grid=(N // tm,),
        in_specs=[pl.BlockSpec((tm, D), row), _mod_spec(mod, False, tm, 1), pl.BlockSpec((1, D), fixed),
                  pl.BlockSpec((D, LANES), fixed), pl.BlockSpec((1, LANES), fixed)],
        out_specs=(pl.BlockSpec((tm * ROW_TILE, LANES), row), pl.BlockSpec((tm, LANES), row),
                   pl.BlockSpec((SUBLANES, LANES), fixed)),
        scratch_shapes=[pltpu.VMEM((SUBLANES, LANES), F32)],
        compiler_params=_params(("arbitrary",)),
        name="moe_route",
    )(x, mod, gpre, rw, rb)


def _row_copy(src_hbm, dst_vmem, sem, src_row, dst_row):
    return pltpu.make_async_copy(src_hbm.at[pl.ds(pl.multiple_of(src_row * ROW_TILE, ROW_TILE), ROW_TILE)],
                                 dst_vmem.at[pl.ds(pl.multiple_of(dst_row * ROW_TILE, ROW_TILE), ROW_TILE)], sem)


def _start_rows(idx_ref, idx0, src_hbm, dst_vmem, sem, rows):
    def body(r, carry):
        _row_copy(src_hbm, dst_vmem, sem, idx_ref[idx0 + r], r).start()
        return carry
    lax.fori_loop(0, rows, body, 0)


def _wait_rows(src_hbm, dst_vmem, sem, rows):
    pltpu.make_async_copy(src_hbm.at[pl.ds(0, rows * ROW_TILE)], dst_vmem.at[pl.ds(0, rows * ROW_TILE)], sem).wait()


def _experts_body(te_ref, nv_ref, src_ref, h_hbm, wg_ref, wu_ref, wd_ref, o_ref, xbuf, hb_ref, acc_ref, sem):
    TM = EXPERT_TILE
    i = pl.program_id(0)
    f = pl.program_id(1)
    slot = i % 2
    n_valid = nv_ref[0]
    valid = i < n_valid

    @pl.when(f == 0)
    def _():
        @pl.when((i == 0) & valid)
        def _():
            _start_rows(src_ref, 0, h_hbm, xbuf.at[0], sem.at[0], TM)

        @pl.when(i + 1 < n_valid)
        def _():
            _start_rows(src_ref, (i + 1) * TM, h_hbm, xbuf.at[1 - slot], sem.at[1 - slot], TM)

        @pl.when(valid)
        def _():
            _wait_rows(h_hbm, xbuf.at[slot], sem.at[slot], TM)
            hb_ref[...] = _from_row_tiles(xbuf.at[slot], 0, TM).astype(BF16)
            acc_ref[...] = jnp.zeros_like(acc_ref)

    @pl.when(valid)
    def _():
        h = hb_ref[...]
        gate = _dot(h, wg_ref[...])
        up = _dot(h, wu_ref[...])
        acc_ref[...] += _dot(gate * _sigmoid(gate) * up, wd_ref[...])

    @pl.when(f == pl.num_programs(1) - 1)
    def _():
        @pl.when(valid)
        def _():
            _to_row_tiles(o_ref, acc_ref[...])

        @pl.when(jnp.logical_not(valid))
        def _():
            o_ref[...] = jnp.zeros_like(o_ref)


def _moe_experts(h_rows, tile_expert, n_valid, src_tok, wg, wu, wd, tf):
    TM = EXPERT_TILE
    P = src_tok.shape[0]
    E, D, F = wg.shape
    grid_spec = pltpu.PrefetchScalarGridSpec(
        num_scalar_prefetch=3,
        grid=(P // TM, F // tf),
        in_specs=[pl.BlockSpec(memory_space=pl.ANY),
                  pl.BlockSpec((None, D, tf), lambda i, f, te, nv, src: (te[i], 0, f)),
                  pl.BlockSpec((None, D, tf), lambda i, f, te, nv, src: (te[i], 0, f)),
                  pl.BlockSpec((None, tf, D), lambda i, f, te, nv, src: (te[i], f, 0))],
        out_specs=pl.BlockSpec((TM * ROW_TILE, LANES), lambda i, f, te, nv, src: (i, 0)),
        scratch_shapes=[pltpu.VMEM((2, TM * ROW_TILE, LANES), F32), pltpu.VMEM((TM, D), BF16),
                        pltpu.VMEM((TM, D), F32), pltpu.SemaphoreType.DMA((2,))],
    )
    return pl.pallas_call(
        _experts_body,
        out_shape=jax.ShapeDtypeStruct((P * ROW_TILE, LANES), F32),
        grid_spec=grid_spec,
        compiler_params=_params(("arbitrary", "arbitrary")),
        name="moe_experts",
    )(tile_expert, n_valid, src_tok, h_rows, wg, wu, wd)


def _combine_body(tiles_per_seq, prompt_row0, dest_ref, f_hbm, x_ref, mod_ref, info_ref, gpost_ref, o_ref, gbuf, sem):
    D = D_MODEL
    i = pl.program_id(0)
    T = x_ref.shape[0]
    slot = i % 2

    @pl.when(i == 0)
    def _():
        _start_rows(dest_ref, 0, f_hbm, gbuf.at[0], sem.at[0], 2 * T)

    @pl.when(i + 1 < pl.num_programs(0))
    def _():
        _start_rows(dest_ref, (i + 1) * 2 * T, f_hbm, gbuf.at[1 - slot], sem.at[1 - slot], 2 * T)

    _wait_rows(f_hbm, gbuf.at[slot], sem.at[slot], 2 * T)
    info = info_ref[...]
    f1 = _from_row_tiles(gbuf.at[slot], 0, T)
    f2 = _from_row_tiles(gbuf.at[slot], T, T)
    y = info[:, INFO_P1:INFO_P1 + 1] * f1 + info[:, INFO_P2:INFO_P2 + 1] * f2
    m = _mod_rows(mod_ref, False, tiles_per_seq, prompt_row0)
    o_ref[...] = x_ref[...] + m[:, 2 * D:3 * D] * _rms(y, gpost_ref[...])


def _moe_combine(dest, f_rows, x, mod, info, gpost, tm, tiles_per_seq, prompt_row0):
    N, D = x.shape
    row = lambda i, d: (i, 0)
    fixed = lambda i, d: (0, 0)
    grid_spec = pltpu.PrefetchScalarGridSpec(
        num_scalar_prefetch=1,
        grid=(N // tm,),
        in_specs=[pl.BlockSpec(memory_space=pl.ANY), pl.BlockSpec((tm, D), row),
                  pl.BlockSpec(mod.shape, fixed), pl.BlockSpec((tm, LANES), row), pl.BlockSpec((1, D), fixed)],
        out_specs=pl.BlockSpec((tm, D), row),
        scratch_shapes=[pltpu.VMEM((2, 2 * tm * ROW_TILE, LANES), F32), pltpu.SemaphoreType.DMA((2,))],
    )
    return pl.pallas_call(
        functools.partial(_combine_body, tiles_per_seq, prompt_row0),
        out_shape=jax.ShapeDtypeStruct((N, D), F32),
        grid_spec=grid_spec,
        compiler_params=_params(("arbitrary",)),
        name="moe_combine",
    )(dest, f_rows, x, mod, info, gpost)


def _moe_prompt(x, mod, gpre, gpost, fw, tm, tiles_per_seq, prompt_row0, tf):
    N = x.shape[0]
    TM = EXPERT_TILE
    n_tiles = (2 * N) // TM + N_EXPERTS
    h_rows, info, cnt = _moe_route(x, mod, gpre, fw["rw"], fw["rb"], tm, tiles_per_seq, prompt_row0)
    e1 = info[:, INFO_E1].astype(jnp.int32)
    e2 = info[:, INFO_E2].astype(jnp.int32)
    counts = cnt[0, :N_EXPERTS].astype(jnp.int32)
    padded = ((counts + TM - 1) // TM) * TM
    ends = jnp.cumsum(padded)
    starts = ends - padded
    dest1 = starts[e1] + info[:, INFO_POS1].astype(jnp.int32)
    dest2 = starts[e2] + info[:, INFO_POS2].astype(jnp.int32)
    tok = jnp.arange(N, dtype=jnp.int32)
    src_tok = jnp.zeros((n_tiles * TM,), jnp.int32).at[dest1].set(tok).at[dest2].set(tok)
    tile_start = jnp.arange(n_tiles, dtype=jnp.int32) * TM
    tile_expert = jnp.minimum(jnp.sum(ends[None, :] <= tile_start[:, None], axis=1), N_EXPERTS - 1).astype(jnp.int32)
    n_valid = (ends[-1:] // TM).astype(jnp.int32)
    f_rows = _moe_experts(h_rows, tile_expert, n_valid, src_tok, fw["wg"], fw["wu"], fw["wd"], tf)
    dest = jnp.concatenate([dest1.reshape(-1, 1, tm), dest2.reshape(-1, 1, tm)], axis=1).reshape(-1)
    return _moe_combine(dest, f_rows, x, mod, info, gpost, tm, tiles_per_seq, prompt_row0)


def _layer_weights(p, l):
    W = RWKV_WIDTH
    heads = np.arange(W) // HEAD_DIM
    headsum = jnp.asarray((heads[:, None] == heads[None, :]).astype(np.float32))
    perm = _q_perm()
    w_in = p["w_in"][l]
    w_in = jnp.concatenate([w_in[:, :RWKV_COLS], w_in[:, RWKV_COLS + perm], w_in[:, RWKV_COLS + SWA_WIDTH:]], axis=1)
    w_out = p["w_out"][l]
    zeros = jnp.zeros((DECAY_LORA, W), F32)
    row = lambda t: t.reshape(1, -1)
    return {
        "w_in": w_in.astype(BF16),
        "w_out_r": w_out[:W].astype(BF16),
        "w_out_a": w_out[W + perm].astype(BF16),
        "mu": row(p["mu_shift"][l]),
        "wd": jnp.concatenate([p["w_decay_up"][l], zeros], axis=0),
        "wi": jnp.concatenate([zeros, p["w_iclr_up"][l]], axis=0),
        "wg": p["w_gate_up"][l],
        "dbase": row(p["decay_base"][l]), "ibase": row(p["iclr_base"][l]),
        "kk": row(p["k_k"][l]), "ka": row(p["k_a"][l]), "rk": row(p["r_k"][l]),
        "lnw": row(p["lnx_w"][l]), "lnb": row(p["lnx_b"][l]),
        "headsum": headsum, "headmean": headsum / HEAD_DIM,
        "g_pre0": row(p["norm_pre"][l, 0]), "g_pre1": row(p["norm_pre"][l, 1]),
        "g_post0": row(p["norm_post"][l, 0]), "g_post1": row(p["norm_post"][l, 1]),
    }


def _ffn_weights(p, l):
    i = l // 2
    if l % 2 == 0:
        return dict(moe=False, rw=jnp.zeros((D_MODEL, LANES), F32), rb=jnp.zeros((1, LANES), F32),
                    wg=p["ffn_w_gate"][i][None].astype(BF16), wu=p["ffn_w_up"][i][None].astype(BF16),
                    wd=p["ffn_w_down"][i][None].astype(BF16))
    rw = jnp.zeros((D_MODEL, LANES), F32).at[:, :N_EXPERTS].set(p["router_w"][i])
    rb = jnp.full((1, LANES), NEG_INF, F32).at[0, :N_EXPERTS].set(p["router_b"][i])
    return dict(moe=True, rw=rw, rb=rb, wg=p["moe_w_gate"][i].astype(BF16), wu=p["moe_w_up"][i].astype(BF16),
                wd=p["moe_w_down"][i].astype(BF16))


def _pick_tile(n, pref):
    t = min(pref, n)
    while n % t:
        t //= 2
    return t


def _ffn_tile(f):
    for t in (1408, 896, 512, 256, 128):
        if f % t == 0:
            return t
    return f


def _trunk(x3, mods, lws, fws, p, prompt, prompt_row0, state=None):
    B, T, D = x3.shape
    N = B * T
    x = x3.reshape(N, D)
    per_row = not prompt
    tm = _pick_tile(T if prompt else N, 512)
    tps = (T // tm) if prompt else 1
    depth = len(lws)
    wkv_out, shift_out, k_out, v_out = [], [], [], []
    for l in range(depth):
        lw, fw = lws[l], fws[l]
        mod0, mod1 = mods[2 * l], mods[2 * l + 1]
        pr, q, k, v = _inproj(x, mod0, lw["g_pre0"], lw["w_in"], per_row, tm, tps, prompt_row0)
        prev = pr if prompt else state["shift"][l]
        r, ld, kh, vv, a, b, g, bonus = _prep(pr, prev, lw, per_row, tm, tps)
        if prompt:
            yw, hbd = _wkv_prompt(r, ld, kh, vv, a, b, B, T)
            n_pairs = RWKV_WIDTH // PAIR
            hb = hbd.reshape(B, n_pairs, 2, HEAD_DIM, 2, HEAD_DIM)
            s_kv = jnp.stack([hb[:, :, 0, :, 0, :], hb[:, :, 1, :, 1, :]], axis=2)
            s_new = jnp.swapaxes(s_kv.reshape(B, RWKV_HEADS, HEAD_DIM, HEAD_DIM), -1, -2)
            ya = _swa_prompt(q, k, v, p["rel_bias"], p["attn_sinks"][l], B, T)
            kb = k.reshape(B, T, SWA_KV_HEADS, HEAD_DIM)[:, -WINDOW:]
            vb = v.reshape(B, T, SWA_KV_HEADS, HEAD_DIM)[:, -WINDOW:]
            last = pr.reshape(B, T, RWKV_COLS)[:, -1]
        else:
            yw, s_new = _wkv_sample(state["wkv"][l], r, ld, kh, vv, a, b)
            ya, kb, vb = _swa_sample(q, k, v, state["k"][l], state["v"][l], p["rel_bias"], p["attn_sinks"][l])
            last = pr
        x = _outproj(x, mod0, yw, g, bonus, ya, lw, per_row, tm, tps, prompt_row0)
        tf = _ffn_tile(fw["wg"].shape[-1])
        if prompt and fw["moe"]:
            x = _moe_prompt(x, mod1, lw["g_pre1"], lw["g_post1"], fw, tm, tps, prompt_row0, tf)
        else:
            x = _ffn(x, mod1, lw["g_pre1"], lw["g_post1"], fw["rw"], fw["rb"], fw["wg"], fw["wu"], fw["wd"],
                     fw["moe"], per_row, tm, tps, prompt_row0, tf)
        wkv_out.append(s_new)
        shift_out.append(last)
        k_out.append(kb)
        v_out.append(vb)
    return x.reshape(B, T, D), jnp.stack(wkv_out), jnp.stack(shift_out), jnp.stack(k_out), jnp.stack(v_out)


def _forward(x_prompt, x_sample, c_prompt, c_sample, state_wkv, state_shift, cache_swa_k, cache_swa_v, p):
    depth = p["w_in"].shape[0]
    Bp, Bs = c_prompt.shape[0], c_sample.shape[0]
    D = D_MODEL
    pad = (-(Bs + Bp)) % SUBLANES
    c_all = jnp.concatenate([c_sample, c_prompt, jnp.zeros((pad, D), F32)], axis=0)
    mods = _ada_all(c_all, p["ada_w"].reshape(2 * depth, D, 3 * D), p["ada_b"].reshape(2 * depth, 1, 3 * D))
    lws = [_layer_weights(p, l) for l in range(depth)]
    fws = [_ffn_weights(p, l) for l in range(depth)]
    y_p, wkv_p, shift_p, k_p, v_p = _trunk(x_prompt, mods, lws, fws, p, True, Bs)
    state = {"wkv": state_wkv, "shift": state_shift, "k": cache_swa_k, "v": cache_swa_v}
    y_s, wkv_s, shift_s, k_s, v_s = _trunk(x_sample, mods, lws, fws, p, False, Bs, state)
    return (y_p, y_s, wkv_p, shift_p, k_p, v_p, wkv_s, shift_s, k_s, v_s)


def kernel(x_prompt, x_sample, c_prompt, c_sample, state_wkv, state_shift, cache_swa_k, cache_swa_v, rel_bias, ada_w, ada_b, norm_pre, norm_post, w_in, mu_shift, w_decay_up, decay_base, w_iclr_up, iclr_base, w_gate_up, k_k, k_a, r_k, lnx_w, lnx_b, attn_sinks, w_out, ffn_w_gate, ffn_w_up, ffn_w_down, router_w, router_b, moe_w_gate, moe_w_up, moe_w_down):
    p = {"rel_bias": rel_bias, "ada_w": ada_w, "ada_b": ada_b, "norm_pre": norm_pre, "norm_post": norm_post,
         "w_in": w_in, "mu_shift": mu_shift, "w_decay_up": w_decay_up, "decay_base": decay_base,
         "w_iclr_up": w_iclr_up, "iclr_base": iclr_base, "w_gate_up": w_gate_up, "k_k": k_k, "k_a": k_a,
         "r_k": r_k.reshape(r_k.shape[0], -1), "lnx_w": lnx_w, "lnx_b": lnx_b, "attn_sinks": attn_sinks,
         "w_out": w_out, "ffn_w_gate": ffn_w_gate, "ffn_w_up": ffn_w_up, "ffn_w_down": ffn_w_down,
         "router_w": router_w, "router_b": router_b, "moe_w_gate": moe_w_gate, "moe_w_up": moe_w_up,
         "moe_w_down": moe_w_down}
    return _forward(x_prompt, x_sample, c_prompt, c_sample, state_wkv, state_shift, cache_swa_k, cache_swa_v, p)
```

```python
import functools

import numpy as np
import jax
import jax.numpy as jnp
from jax import lax
from jax.experimental import pallas as pl
from jax.experimental.pallas import tpu as pltpu

F32 = jnp.float32
BF16 = jnp.bfloat16

D_MODEL = 1024
HEAD_DIM = 64
RWKV_WIDTH = 512
RWKV_HEADS = RWKV_WIDTH // HEAD_DIM
SWA_WIDTH = 512
SWA_HEADS = SWA_WIDTH // HEAD_DIM
SWA_KV_HEADS = 2
SWA_GROUP = SWA_HEADS // SWA_KV_HEADS
KV_WIDTH = SWA_KV_HEADS * HEAD_DIM
WINDOW = 128
DECAY_LORA = 64
ICLR_LORA = 64
GATE_LORA = 128
RWKV_COLS = 3 * RWKV_WIDTH + DECAY_LORA + ICLR_LORA + GATE_LORA
IN_COLS = RWKV_COLS + SWA_WIDTH + 2 * KV_WIDTH
LORA_OFF = 3 * RWKV_WIDTH
GATE_OFF = LORA_OFF + DECAY_LORA + ICLR_LORA
LN_X_EPS = 64e-5
RMS_EPS = 1e-6
N_BUCKETS = 32
MAX_DISTANCE = 128
N_EXPERTS = 8
NEG_INF = -1e30

LANES = 128
SUBLANES = 8
VMEM_LIMIT = 56 * 1024 * 1024

WKV_CHUNK = 64
PAIR = 2 * HEAD_DIM

NN = (((1,), (0,)), ((), ()))
NT = (((1,), (1,)), ((), ()))
TN = (((0,), (0,)), ((), ()))


def _dot(a, b, dims=NN):
    return lax.dot_general(a.astype(BF16), b.astype(BF16), dims, preferred_element_type=F32)


def _split(x, pieces):
    out = []
    for _ in range(pieces - 1):
        hi = x.astype(BF16)
        out.append(hi)
        x = x - hi.astype(F32)
    out.append(x.astype(BF16))
    return out


def _select_dot(x, sel, pieces=3, sel_left=False):
    d = lambda t: lax.dot_general(*((sel, t) if sel_left else (t, sel)), NN, preferred_element_type=F32)
    return sum(d(t) for t in _split(x, pieces))


def _dot3(a, b):
    a_hi, a_lo = _split(a, 2)
    b_hi, b_lo = _split(b, 2)
    d = lambda x, y: lax.dot_general(x, y, NN, preferred_element_type=F32)
    return d(a_hi, b_hi) + d(a_lo, b_hi) + d(a_hi, b_lo)


def _sigmoid(x):
    return 1.0 / (1.0 + jnp.exp(-x))


def _params(sem):
    return pltpu.CompilerParams(dimension_semantics=sem, vmem_limit_bytes=VMEM_LIMIT)


def _rms(x, g):
    return x * lax.rsqrt(jnp.mean(x * x, axis=-1, keepdims=True) + RMS_EPS) * g


def _mod_rows(mod_ref, per_row, tiles_per_seq, prompt_row0):
    if per_row:
        return mod_ref[...]
    b = pl.program_id(0) // tiles_per_seq
    return mod_ref[pl.ds(prompt_row0 + b, 1), :]


def _ada_body(c_ref, w_ref, b_ref, o_ref):
    c = c_ref[...]
    o_ref[...] = _dot(c * _sigmoid(c), w_ref[...]) + b_ref[...]


def _ada_all(c_all, ada_w, ada_b):
    R, D = c_all.shape
    n = ada_w.shape[0]
    tn = 1024
    return pl.pallas_call(
        _ada_body,
        out_shape=jax.ShapeDtypeStruct((n, R, 3 * D), F32),
        grid=(n, 3 * D // tn),
        in_specs=[pl.BlockSpec((R, D), lambda i, j: (0, 0)),
                  pl.BlockSpec((None, D, tn), lambda i, j: (i, 0, j)),
                  pl.BlockSpec((None, 1, tn), lambda i, j: (i, 0, j))],
        out_specs=pl.BlockSpec((None, R, tn), lambda i, j: (i, 0, j)),
        compiler_params=_params(("arbitrary", "arbitrary")),
        name="ada_mod",
    )(c_all, ada_w, ada_b)


def _inproj_body(per_row, tiles_per_seq, prompt_row0, x_ref, mod_ref, g_ref, w_ref, pr_ref, q_ref, k_ref, v_ref):
    D = D_MODEL
    m = _mod_rows(mod_ref, per_row, tiles_per_seq, prompt_row0)
    h = _rms(x_ref[...], g_ref[...]) * (1.0 + m[:, D:2 * D]) + m[:, 0:D]
    proj = _dot(h, w_ref[...])
    pr_ref[...] = proj[:, 0:RWKV_COLS]
    q_ref[...] = proj[:, RWKV_COLS:RWKV_COLS + SWA_WIDTH]
    k_ref[...] = proj[:, RWKV_COLS + SWA_WIDTH:RWKV_COLS + SWA_WIDTH + KV_WIDTH]
    v_ref[...] = proj[:, RWKV_COLS + SWA_WIDTH + KV_WIDTH:IN_COLS]


def _mod_spec(mod, per_row, tm, nargs):
    R = mod.shape[0]
    if per_row:
        return pl.BlockSpec((tm, 3 * D_MODEL), lambda i, *_: (0, 0))
    return pl.BlockSpec((R, 3 * D_MODEL), lambda i, *_: (0, 0))


def _inproj(x, mod, g, w, per_row, tm, tiles_per_seq, prompt_row0):
    N, D = x.shape
    row = lambda i: (i, 0)
    fixed = lambda i: (0, 0)
    return pl.pallas_call(
        functools.partial(_inproj_body, per_row, tiles_per_seq, prompt_row0),
        out_shape=(jax.ShapeDtypeStruct((N, RWKV_COLS), F32), jax.ShapeDtypeStruct((N, SWA_WIDTH), F32),
                   jax.ShapeDtypeStruct((N, KV_WIDTH), F32), jax.ShapeDtypeStruct((N, KV_WIDTH), F32)),
        grid=(N // tm,),
        in_specs=[pl.BlockSpec((tm, D), row), _mod_spec(mod, per_row, tm, 1),
                  pl.BlockSpec((1, D), fixed), pl.BlockSpec((D, IN_COLS), fixed)],
        out_specs=(pl.BlockSpec((tm, RWKV_COLS), row), pl.BlockSpec((tm, SWA_WIDTH), row),
                   pl.BlockSpec((tm, KV_WIDTH), row), pl.BlockSpec((tm, KV_WIDTH), row)),
        compiler_params=_params(("arbitrary",)),
        name="in_proj",
    )(x, mod, g, w)


def _prep_body(per_row, tiles_per_seq, pr_ref, prev_ref, mu_ref, wd_ref, wi_ref, wg_ref, dbase_ref, ibase_ref,
               kk_ref, ka_ref, rk_ref, hs_ref,
               r_ref, ld_ref, k_ref, v_ref, a_ref, b_ref, g_ref, bonus_ref):
    W = RWKV_WIDTH
    pr = pr_ref[...]
    if per_row:
        shifted = prev_ref[...]
    else:
        first = (pl.program_id(0) % tiles_per_seq) == 0
        carry = jnp.where(first, 0.0, prev_ref[SUBLANES - 1:SUBLANES, :])
        rows = lax.broadcasted_iota(jnp.int32, pr.shape, 0)
        shifted = jnp.where(rows == 0, carry, pltpu.roll(pr, 1, 0))
    xs = pr + (shifted - pr) * mu_ref[...]
    r = xs[:, 0:W]
    k = xs[:, W:2 * W]
    v = xs[:, 2 * W:3 * W]
    lora = xs[:, LORA_OFF:GATE_OFF]
    gl = xs[:, GATE_OFF:RWKV_COLS]
    z = dbase_ref[...] + _dot3(jnp.tanh(lora), wd_ref[...])
    ld = -float(np.exp(-0.5)) * _sigmoid(z)
    iclr = _sigmoid(ibase_ref[...] + _dot3(lora, wi_ref[...]))
    g = _dot3(_sigmoid(gl), wg_ref[...])
    hs = hs_ref[...]
    kk = k * kk_ref[...]
    kk = kk / jnp.maximum(jnp.sqrt(_select_dot(kk * kk, hs, 2)), 1e-12)
    kh = k * (1.0 + (iclr - 1.0) * ka_ref[...])
    bonus = _select_dot(r * kh * rk_ref[...], hs, 2) * v
    r_ref[...] = r
    ld_ref[...] = ld
    k_ref[...] = kh
    v_ref[...] = v
    a_ref[...] = -kk
    b_ref[...] = kk * iclr
    g_ref[...] = g
    bonus_ref[...] = bonus


def _prep(pr, prev, lw, per_row, tm, tiles_per_seq):
    N = pr.shape[0]
    W = RWKV_WIDTH
    row = lambda i: (i, 0)
    fixed = lambda i: (0, 0)
    if per_row:
        prev_spec = pl.BlockSpec((tm, RWKV_COLS), row)
    else:
        per = tm // SUBLANES
        prev_spec = pl.BlockSpec((SUBLANES, RWKV_COLS), lambda i: (jnp.maximum(i * per - 1, 0), 0))
    vec = pl.BlockSpec((1, W), fixed)
    out = jax.ShapeDtypeStruct((N, W), F32)
    return pl.pallas_call(
        functools.partial(_prep_body, per_row, tiles_per_seq),
        out_shape=(out,) * 8,
        grid=(N // tm,),
        in_specs=[pl.BlockSpec((tm, RWKV_COLS), row), prev_spec, pl.BlockSpec((1, RWKV_COLS), fixed),
                  pl.BlockSpec((LANES, W), fixed), pl.BlockSpec((LANES, W), fixed), pl.BlockSpec((GATE_LORA, W), fixed),
                  vec, vec, vec, vec, vec, pl.BlockSpec((W, W), fixed)],
        out_specs=(pl.BlockSpec((tm, W), row),) * 8,
        compiler_params=_params(("arbitrary",)),
        name="rwkv_prep",
    )(pr, prev, lw["mu"], lw["wd"], lw["wi"], lw["wg"], lw["dbase"], lw["ibase"], lw["kk"], lw["ka"], lw["rk"],
      lw["headsum"])


def _stack_heads(x, lane_head0):
    return jnp.concatenate([jnp.where(lane_head0, x, 0.0), jnp.where(lane_head0, 0.0, x)], axis=0)


def _fold_heads(x):
    c = x.shape[0] // 2
    return x[0:c] + x[c:2 * c]


def _dots(xs, ys, dims=NN):
    return [_dot(x, y, dims) for x, y in zip(xs, ys)]


def _unit_lower_inverse(ns, same16, eye):
    nd = [jnp.where(same16, n, 0.0) for n in ns]
    no = [n - d for n, d in zip(ns, nd)]
    n2 = _dots(nd, nd)
    n4 = _dots(n2, n2)
    n8 = _dots(n4, n4)
    td = [eye + d for d in nd]
    for pw in (n2, n4, n8):
        td = [t + u for t, u in zip(td, _dots(td, pw))]
    q = _dots(td, no)
    q2 = _dots(q, q)
    z = [eye + x for x in q]
    z = [t + u for t, u in zip(z, _dots(z, q2))]
    return _dots(z, td)


def _wkv_chunk_body(chunks, r_ref, ld_ref, k_ref, v_ref, a_ref, b_ref, y_ref, s_ref, h_ref):
    C = WKV_CHUNK
    n_pairs = RWKV_WIDTH // PAIR

    @pl.when(pl.program_id(1) == 0)
    def _():
        h_ref[...] = jnp.zeros_like(h_ref)

    ri = lax.broadcasted_iota(jnp.int32, (PAIR, PAIR), 0)
    ci = lax.broadcasted_iota(jnp.int32, (PAIR, PAIR), 1)
    same_head = (ri // C) == (ci // C)
    strict_lower = same_head & (ci < ri)
    incl_lower = same_head & (ci <= ri)
    same16 = (ri // 16) == (ci // 16)
    eye_b = ri == ci
    eye = jnp.where(eye_b, 1.0, 0.0)
    tri = jnp.where(lax.broadcasted_iota(jnp.int32, (C, C), 1) <= lax.broadcasted_iota(jnp.int32, (C, C), 0),
                    1.0, 0.0).astype(BF16)
    lane_head0 = lax.broadcasted_iota(jnp.int32, (C, PAIR), 1) < HEAD_DIM
    zeros = jnp.zeros((PAIR, PAIR), F32)

    cat0 = lambda x, y: jnp.concatenate([x, y], axis=0)
    cat1 = lambda x, y: jnp.concatenate([x, y], axis=1)

    xa, xr, v_st, bh_st, kh_st, yb, yk, p_all = [], [], [], [], [], [], [], []
    for c in range(chunks):
        rows = slice(c * C, (c + 1) * C)
        ld = ld_ref[rows, :]
        cum = _select_dot(ld, tri, 3, sel_left=True)
        last = cum[C - 1:C, :]
        p_inv = jnp.exp(-cum)
        p_tail = jnp.exp(last - cum)
        p_end = jnp.exp(last)
        a_t = a_ref[rows, :] * jnp.exp(cum - ld)
        r_t = r_ref[rows, :] * jnp.exp(cum)
        b_raw = b_ref[rows, :]
        k_raw = k_ref[rows, :]
        b_t = b_raw * p_inv
        k_t = k_raw * p_inv
        b_h = b_raw * p_tail
        k_h = k_raw * p_tail
        v_all = v_ref[rows, :]
        for j in range(n_pairs):
            lanes = slice(j * PAIR, (j + 1) * PAIR)
            xa.append(_stack_heads(a_t[:, lanes], lane_head0))
            xr.append(_stack_heads(r_t[:, lanes], lane_head0))
            v_st.append(_stack_heads(v_all[:, lanes], lane_head0))
            bh_st.append(_stack_heads(b_h[:, lanes], lane_head0))
            kh_st.append(_stack_heads(k_h[:, lanes], lane_head0))
            yb.append(cat0(b_t[:, lanes], b_t[:, lanes]))
            yk.append(cat0(k_t[:, lanes], k_t[:, lanes]))
            p_all.append(p_end[:, lanes])

    gram = _dots([cat0(x, y) for x, y in zip(xa, xr)], [cat0(x, y) for x, y in zip(yb, yk)], NT)
    n_mat = [jnp.where(strict_lower, g[0:PAIR, 0:PAIR], 0.0) for g in gram]
    m_mat = [jnp.where(strict_lower, g[0:PAIR, PAIR:2 * PAIR], 0.0) for g in gram]
    a_rbk = [cat1(jnp.where(incl_lower, g[PAIR:2 * PAIR, 0:PAIR], 0.0),
                  jnp.where(incl_lower, g[PAIR:2 * PAIR, PAIR:2 * PAIR], 0.0)) for g in gram]
    t_inv = _unit_lower_inverse(n_mat, same16, eye)
    mv = _dots(m_mat, v_st)
    tx = _dots(t_inv, [cat1(x, y) for x, y in zip(xa, mv)])
    rhs = [cat0(t, cat1(zeros, v)) for t, v in zip(tx, v_st)]
    ry = _dots(a_rbk, rhs)
    pp = _dots([cat0(x, y) for x, y in zip(bh_st, kh_st)], rhs, TN)

    for c in range(chunks):
        us = [c * n_pairs + j for j in range(n_pairs)]
        h0 = [h_ref[j] for j in range(n_pairs)]
        r_bar = [_fold_heads(xr[u] + ry[u][:, 0:PAIR]) for u in us]
        phi = [pp[u][:, 0:PAIR] + jnp.where(eye_b, p_all[u], 0.0) for u in us]
        ys = _dots(r_bar, h0)
        hs = _dots(phi, h0)
        for j, u in enumerate(us):
            y_ref[c * C:(c + 1) * C, j * PAIR:(j + 1) * PAIR] = ys[j] + _fold_heads(ry[u][:, PAIR:2 * PAIR])
            h_ref[j] = hs[j] + pp[u][:, PAIR:2 * PAIR]

    @pl.when(pl.program_id(1) == pl.num_programs(1) - 1)
    def _():
        s_ref[...] = h_ref[...]


def _wkv_prompt(r, ld, k, v, a, b, batch, seq):
    N, W = r.shape
    chunks = 4
    tt = chunks * WKV_CHUNK
    steps = seq // tt
    n_pairs = W // PAIR
    row = lambda bb, t: (bb * steps + t, 0)
    spec = pl.BlockSpec((tt, W), row)
    return pl.pallas_call(
        functools.partial(_wkv_chunk_body, chunks),
        out_shape=(jax.ShapeDtypeStruct((N, W), F32), jax.ShapeDtypeStruct((batch, n_pairs, PAIR, PAIR), F32)),
        grid=(batch, steps),
        in_specs=[spec] * 6,
        out_specs=(spec, pl.BlockSpec((None, n_pairs, PAIR, PAIR), lambda bb, t: (bb, 0, 0, 0))),
        scratch_shapes=[pltpu.VMEM((n_pairs, PAIR, PAIR), F32)],
        compiler_params=_params(("arbitrary", "arbitrary")),
        name="wkv_chunk_scan",
    )(r, ld, k, v, a, b)


def _wkv_step_body(s_ref, r_ref, ld_ref, k_ref, v_ref, a_ref, b_ref, exp_ref, red_ref, y_ref, so_ref):
    HD = HEAD_DIM
    nrep = HD * HD // LANES
    lane = lax.broadcasted_iota(jnp.int32, r_ref.shape, 1)
    low = lane < HD
    y = jnp.zeros(r_ref.shape, F32)
    for hh in range(2):
        def tiled(ref, fn=None):
            x = ref[...]
            if fn is not None:
                x = fn(x)
            sw = pltpu.roll(x, HD, 1)
            both = jnp.where(low, x, sw) if hh == 0 else jnp.where(low, sw, x)
            return jnp.tile(both, (1, nrep))
        cols = slice(hh * HD * HD, (hh + 1) * HD * HD)
        s = s_ref[:, cols]
        expand = exp_ref[hh]
        reduce_ = red_ref[hh]
        sa = _select_dot(s * tiled(a_ref), reduce_)
        s_new = (s * tiled(ld_ref, jnp.exp) + _select_dot(sa, expand) * tiled(b_ref)
                 + _select_dot(v_ref[...], expand) * tiled(k_ref))
        so_ref[:, cols] = s_new
        y = y + _select_dot(s_new * tiled(r_ref), reduce_)
    y_ref[...] = y


def _wkv_step_consts():
    HD = HEAD_DIM
    expand = np.zeros((2, PAIR, HD * HD), np.float32)
    for hh in range(2):
        for vv in range(HD):
            expand[hh, hh * HD + vv, vv * HD:(vv + 1) * HD] = 1.0
    return jnp.asarray(expand, BF16), jnp.asarray(expand.transpose(0, 2, 1), BF16)


def _wkv_sample(state, r, ld, k, v, a, b):
    B = state.shape[0]
    W = RWKV_WIDTH
    HD2 = HEAD_DIM * HEAD_DIM
    expand, reduce_ = _wkv_step_consts()
    s2 = state.reshape(B, RWKV_HEADS * HD2)
    st_spec = pl.BlockSpec((B, 2 * HD2), lambda j: (0, j))
    vec = pl.BlockSpec((B, PAIR), lambda j: (0, j))
    y, s_new = pl.pallas_call(
        _wkv_step_body,
        out_shape=(jax.ShapeDtypeStruct((B, W), F32), jax.ShapeDtypeStruct((B, RWKV_HEADS * HD2), F32)),
        grid=(W // PAIR,),
        in_specs=[st_spec] + [vec] * 6 + [pl.BlockSpec((2, PAIR, HD2), lambda j: (0, 0, 0)),
                                          pl.BlockSpec((2, HD2, PAIR), lambda j: (0, 0, 0))],
        out_specs=(vec, st_spec),
        compiler_params=_params(("arbitrary",)),
        name="wkv_step",
    )(s2, r, ld, k, v, a, b, expand, reduce_)
    return y, s_new.reshape(state.shape)


def _q_perm():
    return np.array([(h * SWA_GROUP + g) * HEAD_DIM + d for g in range(SWA_GROUP) for h in range(SWA_KV_HEADS)
                     for d in range(HEAD_DIM)], np.int32)


def _rel_bucket_np(dist):
    max_exact = N_BUCKETS // 2
    d = np.maximum(dist, 0)
    ratio = np.log(np.maximum(d, 1).astype(np.float32) / np.float32(max_exact)) / np.float32(
        np.log(MAX_DISTANCE / max_exact))
    large = np.minimum(max_exact + (ratio.astype(np.float32) * np.float32(N_BUCKETS - max_exact)).astype(np.int32),
                       N_BUCKETS - 1)
    return np.where(d < max_exact, d, large).astype(np.int32)


def _bias_from_buckets(idx, rb_ref, head):
    acc = jnp.zeros(idx.shape, F32)
    for bk in range(N_BUCKETS):
        acc = jnp.where(idx == bk, rb_ref[bk, head], acc)
    return acc


def _swa_prompt_body(rb_ref, sink_ref, idx_ref, q_ref, kp_ref, kc_ref, vp_ref, vc_ref, o_ref, bias_ref):
    Q = WINDOW
    first = (pl.program_id(0) == 0) & (pl.program_id(1) == 0)

    @pl.when(first)
    def _():
        idx = idx_ref[...]
        for g in range(SWA_GROUP):
            for h in range(SWA_KV_HEADS):
                bias_ref[g * SWA_KV_HEADS + h] = _bias_from_buckets(idx, rb_ref, h * SWA_GROUP + g)

    n = pl.program_id(1)
    qi = lax.broadcasted_iota(jnp.int32, (Q, 2 * Q), 0)
    kj = lax.broadcasted_iota(jnp.int32, (Q, 2 * Q), 1)
    valid = ((kj < Q) & (kj >= qi) & (n > 0)) | ((kj >= Q) & ((kj - Q) <= qi))
    lane_kv0 = lax.broadcasted_iota(jnp.int32, (Q, LANES), 1) < HEAD_DIM
    kcat = jnp.concatenate([kp_ref[...], kc_ref[...]], axis=0).astype(BF16)
    vcat = jnp.concatenate([vp_ref[...], vc_ref[...]], axis=0).astype(BF16)
    scale = HEAD_DIM ** -0.5
    for g in range(SWA_GROUP):
        qg = q_ref[:, g * LANES:(g + 1) * LANES]
        outs = []
        for h in range(SWA_KV_HEADS):
            qm = jnp.where(lane_kv0, qg, 0.0) if h == 0 else jnp.where(lane_kv0, 0.0, qg)
            s = _dot(qm, kcat, NT) * scale + bias_ref[g * SWA_KV_HEADS + h]
            s = jnp.where(valid, s, NEG_INF)
            sink = sink_ref[h * SWA_GROUP + g]
            m = jnp.maximum(jnp.max(s, axis=-1, keepdims=True), sink)
            p = jnp.exp(s - m)
            den = jnp.sum(p, axis=-1, keepdims=True) + jnp.exp(sink - m)
            outs.append(_dot(p, vcat) / den)
        o_ref[:, g * LANES:(g + 1) * LANES] = jnp.where(lane_kv0, outs[0], outs[1])


def _swa_prompt(q, k, v, rel_bias, sinks, batch, seq):
    N = q.shape[0]
    Q = WINDOW
    nb = seq // Q
    qi = np.arange(Q)[:, None]
    kj = np.arange(2 * Q)[None, :]
    idx = jnp.asarray(_rel_bucket_np(qi + Q - kj))
    cur = lambda bb, n: (bb * nb + n, 0)
    prev = lambda bb, n: (bb * nb + jnp.maximum(n - 1, 0), 0)
    kv_c = pl.BlockSpec((Q, KV_WIDTH), cur)
    kv_p = pl.BlockSpec((Q, KV_WIDTH), prev)
    smem = pl.BlockSpec(memory_space=pltpu.SMEM)
    return pl.pallas_call(
        _swa_prompt_body,
        out_shape=jax.ShapeDtypeStruct((N, SWA_WIDTH), F32),
        grid=(batch, nb),
        in_specs=[smem, smem, pl.BlockSpec((Q, 2 * Q), lambda bb, n: (0, 0)),
                  pl.BlockSpec((Q, SWA_WIDTH), cur), kv_p, kv_c, kv_p, kv_c],
        out_specs=pl.BlockSpec((Q, SWA_WIDTH), cur),
        scratch_shapes=[pltpu.VMEM((SWA_HEADS, Q, 2 * Q), F32)],
        compiler_params=_params(("arbitrary", "arbitrary")),
        name="swa_prompt",
    )(rel_bias, sinks, idx, q, k, k, v, v)


def _swa_sample_body(rb_ref, sink_ref, idx_ref, q_ref, kn_ref, vn_ref, kb_ref, vb_ref, o_ref, ko_ref, vo_ref,
                     bias_ref, extra_ref):
    W = WINDOW
    G, KVH = SWA_GROUP, SWA_KV_HEADS

    @pl.when(pl.program_id(0) == 0)
    def _():
        idx = jnp.broadcast_to(idx_ref[...], (SUBLANES, W))
        row = lax.broadcasted_iota(jnp.int32, (SUBLANES, W), 0)
        acc = jnp.zeros((SUBLANES, W), F32)
        ext = jnp.zeros((SUBLANES, LANES), F32)
        lane = lax.broadcasted_iota(jnp.int32, (SUBLANES, LANES), 1)
        for h in range(KVH):
            for g in range(G):
                head = h * G + g
                r = h * G + g
                acc = jnp.where(row == r, _bias_from_buckets(idx, rb_ref, head), acc)
                ext = jnp.where((row == r) & (lane == 0), rb_ref[0, head], ext)
                ext = jnp.where((row == r) & (lane == 1), sink_ref[head], ext)
        bias_ref[...] = acc
        extra_ref[...] = ext

    TB = q_ref.shape[0]
    lane_kv0 = lax.broadcasted_iota(jnp.int32, (TB, G, LANES), 2) < HEAD_DIM
    q4 = q_ref[...]
    qrows = jnp.concatenate([jnp.where(lane_kv0, q4, 0.0), jnp.where(lane_kv0, 0.0, q4)], axis=1)
    kb = kb_ref[...]
    vb = vb_ref[...]
    kn = kn_ref[...]
    vn = vn_ref[...]
    scale = HEAD_DIM ** -0.5
    bdims = (((2,), (2,)), ((0,), (0,)))
    s = lax.dot_general(qrows.astype(BF16), kb.astype(BF16), bdims, preferred_element_type=F32) * scale
    s = s + bias_ref[...][None]
    s_self = jnp.sum(qrows * kn, axis=-1, keepdims=True) * scale + extra_ref[:, 0:1][None]
    sink = extra_ref[:, 1:2][None]
    m = jnp.maximum(jnp.maximum(jnp.max(s, axis=-1, keepdims=True), s_self), sink)
    p = jnp.exp(s - m)
    p_self = jnp.exp(s_self - m)
    den = jnp.sum(p, axis=-1, keepdims=True) + p_self + jnp.exp(sink - m)
    pv = lax.dot_general(p.astype(BF16), vb.astype(BF16), (((2,), (1,)), ((0,), (0,))), preferred_element_type=F32)
    o = (pv + p_self * vn) / den
    o_ref[...] = jnp.where(lane_kv0, o[:, 0:G], o[:, G:2 * G])

    rowmod = lax.broadcasted_iota(jnp.int32, (TB, W, KV_WIDTH), 1)
    for buf, new, out in ((kb, kn, ko_ref), (vb, vn, vo_ref)):
        rolled = pltpu.roll(buf.reshape(TB * W, KV_WIDTH), TB * W - 1, 0).reshape(TB, W, KV_WIDTH)
        out[...] = jnp.where(rowmod == W - 1, new, rolled)


def _swa_sample(q, k, v, kbuf, vbuf, rel_bias, sinks):
    B = q.shape[0]
    W = WINDOW
    tb = 16
    idx = jnp.asarray(_rel_bucket_np(W - np.arange(W))[None, :])
    q4 = q.reshape(B, SWA_GROUP, LANES)
    kn = k.reshape(B, 1, KV_WIDTH)
    vn = v.reshape(B, 1, KV_WIDTH)
    kb = kbuf.reshape(B, W, KV_WIDTH)
    vb = vbuf.reshape(B, W, KV_WIDTH)
    smem = pl.BlockSpec(memory_space=pltpu.SMEM)
    b3 = lambda i: (i, 0, 0)
    o, ko, vo = pl.pallas_call(
        _swa_sample_body,
        out_shape=(jax.ShapeDtypeStruct((B, SWA_GROUP, LANES), F32), jax.ShapeDtypeStruct((B, W, KV_WIDTH), F32),
                   jax.ShapeDtypeStruct((B, W, KV_WIDTH), F32)),
        grid=(B // tb,),
        in_specs=[smem, smem, pl.BlockSpec((1, W), lambda i: (0, 0)),
                  pl.BlockSpec((tb, SWA_GROUP, LANES), b3), pl.BlockSpec((tb, 1, KV_WIDTH), b3),
                  pl.BlockSpec((tb, 1, KV_WIDTH), b3), pl.BlockSpec((tb, W, KV_WIDTH), b3),
                  pl.BlockSpec((tb, W, KV_WIDTH), b3)],
        out_specs=(pl.BlockSpec((tb, SWA_GROUP, LANES), b3), pl.BlockSpec((tb, W, KV_WIDTH), b3),
                   pl.BlockSpec((tb, W, KV_WIDTH), b3)),
        scratch_shapes=[pltpu.VMEM((SUBLANES, W), F32), pltpu.VMEM((SUBLANES, LANES), F32)],
        compiler_params=_params(("arbitrary",)),
        name="swa_sample",
    )(rel_bias, sinks, idx, q4, kn, vn, kb, vb)
    return o.reshape(B, SWA_WIDTH), ko.reshape(kbuf.shape), vo.reshape(vbuf.shape)


def _outproj_body(per_row, tiles_per_seq, prompt_row0, x_ref, mod_ref, yw_ref, g_ref, bonus_ref, ya_ref,
                  lnw_ref, lnb_ref, hm_ref, wr_ref, wa_ref, gpost_ref, o_ref):
    D = D_MODEL
    m = _mod_rows(mod_ref, per_row, tiles_per_seq, prompt_row0)
    yw = yw_ref[...]
    hm = hm_ref[...]
    mean = _select_dot(yw, hm, 2)
    dv = yw - mean
    var = _select_dot(dv * dv, hm, 2)
    yn = dv * lax.rsqrt(var + LN_X_EPS) * lnw_ref[...] + lnb_ref[...]
    yr = (yn + bonus_ref[...]) * g_ref[...]
    mix = _dot(yr, wr_ref[...]) + _dot(ya_ref[...], wa_ref[...])
    o_ref[...] = x_ref[...] + m[:, 2 * D:3 * D] * _rms(mix, gpost_ref[...])


def _outproj(x, mod, yw, g, bonus, ya, lw, per_row, tm, tiles_per_seq, prompt_row0):
    N, D = x.shape
    W = RWKV_WIDTH
    row = lambda i: (i, 0)
    fixed = lambda i: (0, 0)
    half = pl.BlockSpec((tm, W), row)
    vec = pl.BlockSpec((1, W), fixed)
    return pl.pallas_call(
        functools.partial(_outproj_body, per_row, tiles_per_seq, prompt_row0),
        out_shape=jax.ShapeDtypeStruct((N, D), F32),
        grid=(N // tm,),
        in_specs=[pl.BlockSpec((tm, D), row), _mod_spec(mod, per_row, tm, 1), half, half, half, half, vec, vec,
                  pl.BlockSpec((W, W), fixed), pl.BlockSpec((W, D), fixed), pl.BlockSpec((W, D), fixed),
                  pl.BlockSpec((1, D), fixed)],
        out_specs=pl.BlockSpec((tm, D), row),
        compiler_params=_params(("arbitrary",)),
        name="out_proj",
    )(x, mod, yw, g, bonus, ya, lw["lnw"], lw["lnb"], lw["headmean"], lw["w_out_r"], lw["w_out_a"], lw["g_post0"])


def _ffn_body(moe, per_row, tiles_per_seq, prompt_row0, x_ref, mod_ref, gpre_ref, gpost_ref, rw_ref, rb_ref,
              wg_ref, wu_ref, wd_ref, o_ref, h_ref, acc_ref, comb_ref):
    D = D_MODEL
    e = pl.program_id(1)
    f = pl.program_id(2)
    first = (e == 0) & (f == 0)
    last = (e == pl.num_programs(1) - 1) & (f == pl.num_programs(2) - 1)

    @pl.when(first)
    def _():
        m = _mod_rows(mod_ref, per_row, tiles_per_seq, prompt_row0)
        h = _rms(x_ref[...], gpre_ref[...]) * (1.0 + m[:, D:2 * D]) + m[:, 0:D]
        h_ref[...] = h.astype(BF16)
        acc_ref[...] = jnp.zeros_like(acc_ref)
        if moe:
            logits = _dot3(h, rw_ref[...]) + rb_ref[...]
            lane = lax.broadcasted_iota(jnp.int32, logits.shape, 1)
            m1 = jnp.max(logits, axis=-1, keepdims=True)
            i1 = jnp.min(jnp.where(logits == m1, lane, LANES), axis=-1, keepdims=True)
            rest = jnp.where(lane == i1, -jnp.inf, logits)
            m2 = jnp.max(rest, axis=-1, keepdims=True)
            i2 = jnp.min(jnp.where(rest == m2, lane, LANES), axis=-1, keepdims=True)
            e2 = jnp.exp(m2 - m1)
            comb_ref[...] = jnp.where(lane == i1, 1.0 / (1.0 + e2), 0.0) + jnp.where(lane == i2, e2 / (1.0 + e2), 0.0)

    h = h_ref[...]
    gate = _dot(h, wg_ref[...])
    up = _dot(h, wu_ref[...])
    act = gate * _sigmoid(gate) * up
    if moe:
        comb = comb_ref[...]
        lane = lax.broadcasted_iota(jnp.int32, comb.shape, 1)
        act = act * jnp.sum(jnp.where(lane == e, comb, 0.0), axis=-1, keepdims=True)
    acc_ref[...] += _dot(act, wd_ref[...])

    @pl.when(last)
    def _():
        m = _mod_rows(mod_ref, per_row, tiles_per_seq, prompt_row0)
        o_ref[...] = x_ref[...] + m[:, 2 * D:3 * D] * _rms(acc_ref[...], gpost_ref[...])


def _ffn(x, mod, gpre, gpost, router_w, router_b, wg, wu, wd, moe, per_row, tm, tiles_per_seq, prompt_row0, tf):
    N, D = x.shape
    E, _, F = wg.shape
    row = lambda i, e, f: (i, 0)
    fixed = lambda i, e, f: (0, 0)
    return pl.pallas_call(
        functools.partial(_ffn_body, moe, per_row, tiles_per_seq, prompt_row0),
        out_shape=jax.ShapeDtypeStruct((N, D), F32),
        grid=(N // tm, E, F // tf),
        in_specs=[pl.BlockSpec((tm, D), row), _mod_spec(mod, per_row, tm, 3),
                  pl.BlockSpec((1, D), fixed), pl.BlockSpec((1, D), fixed),
                  pl.BlockSpec((D, LANES), fixed), pl.BlockSpec((1, LANES), fixed),
                  pl.BlockSpec((None, D, tf), lambda i, e, f: (e, 0, f)),
                  pl.BlockSpec((None, D, tf), lambda i, e, f: (e, 0, f)),
                  pl.BlockSpec((None, tf, D), lambda i, e, f: (e, f, 0))],
        out_specs=pl.BlockSpec((tm, D), row),
        scratch_shapes=[pltpu.VMEM((tm, D), BF16), pltpu.VMEM((tm, D), F32), pltpu.VMEM((tm, LANES), F32)],
        compiler_params=_params(("arbitrary", "arbitrary", "arbitrary")),
        name="moe_ffn" if moe else "dense_ffn",
    )(x, mod, gpre, gpost, router_w, router_b, wg, wu, wd)


ROW_TILE = D_MODEL // LANES
EXPERT_TILE = 512
INFO_E1, INFO_E2, INFO_P1, INFO_P2, INFO_POS1, INFO_POS2 = range(6)


def _top2(logits):
    lane = lax.broadcasted_iota(jnp.int32, logits.shape, 1)
    m1 = jnp.max(logits, axis=-1, keepdims=True)
    i1 = jnp.min(jnp.where(logits == m1, lane, LANES), axis=-1, keepdims=True)
    rest = jnp.where(lane == i1, -jnp.inf, logits)
    m2 = jnp.max(rest, axis=-1, keepdims=True)
    i2 = jnp.min(jnp.where(rest == m2, lane, LANES), axis=-1, keepdims=True)
    e2 = jnp.exp(m2 - m1)
    return lane, i1, i2, 1.0 / (1.0 + e2), e2 / (1.0 + e2)


def _to_row_tiles(ref, x):
    rows = x.shape[0]
    for c in range(ROW_TILE):
        ref[pl.ds(c, rows, stride=ROW_TILE), :] = x[:, c * LANES:(c + 1) * LANES]


def _from_row_tiles(ref, row0, rows):
    return jnp.concatenate([ref[pl.ds(row0 * ROW_TILE + c, rows, stride=ROW_TILE), :] for c in range(ROW_TILE)],
                           axis=1)


def _route_body(tiles_per_seq, prompt_row0, x_ref, mod_ref, gpre_ref, rw_ref, rb_ref, h_ref, info_ref, cnt_ref,
                base_ref):
    D = D_MODEL

    @pl.when(pl.program_id(0) == 0)
    def _():
        base_ref[...] = jnp.zeros_like(base_ref)

    m = _mod_rows(mod_ref, False, tiles_per_seq, prompt_row0)
    h = _rms(x_ref[...], gpre_ref[...]) * (1.0 + m[:, D:2 * D]) + m[:, 0:D]
    _to_row_tiles(h_ref, h)
    lane, i1, i2, p1, p2 = _top2(_dot3(h, rw_ref[...]) + rb_ref[...])
    onehot = jnp.where((lane == i1) | (lane == i2), 1.0, 0.0)
    T = h.shape[0]
    before = lax.broadcasted_iota(jnp.int32, (T, T), 1) < lax.broadcasted_iota(jnp.int32, (T, T), 0)
    rank = _dot(jnp.where(before, 1.0, 0.0), onehot) + base_ref[0:1, :]
    pos1 = jnp.sum(jnp.where(lane == i1, rank, 0.0), axis=-1, keepdims=True)
    pos2 = jnp.sum(jnp.where(lane == i2, rank, 0.0), axis=-1, keepdims=True)
    info = jnp.zeros(onehot.shape, F32)
    for col, val in ((INFO_E1, i1.astype(F32)), (INFO_E2, i2.astype(F32)), (INFO_P1, p1), (INFO_P2, p2),
                     (INFO_POS1, pos1), (INFO_POS2, pos2)):
        info = jnp.where(lane == col, val, info)
    info_ref[...] = info
    total = base_ref[...] + jnp.sum(onehot, axis=0, keepdims=True)
    base_ref[...] = total
    cnt_ref[...] = total


def _moe_route(x, mod, gpre, rw, rb, tm, tiles_per_seq, prompt_row0):
    N, D = x.shape
    row = lambda i: (i, 0)
    fixed = lambda i: (0, 0)
    return pl.pallas_call(
        functools.partial(_route_body, tiles_per_seq, prompt_row0),
        out_shape=(jax.ShapeDtypeStruct((N * ROW_TILE, LANES), F32), jax.ShapeDtypeStruct((N, LANES), F32),
                   jax.ShapeDtypeStruct((SUBLANES, LANES), F32)),
        grid=(N // tm,),
        in_specs=[pl.BlockSpec((tm, D), row), _mod_spec(mod, False, tm, 1), pl.BlockSpec((1, D), fixed),
                  pl.BlockSpec((D, LANES), fixed), pl.BlockSpec((1, LANES), fixed)],
        out_specs=(pl.BlockSpec((tm * ROW_TILE, LANES), row), pl.BlockSpec((tm, LANES), row),
                   pl.BlockSpec((SUBLANES, LANES), fixed)),
        scratch_shapes=[pltpu.VMEM((SUBLANES, LANES), F32)],
        compiler_params=_params(("arbitrary",)),
        name="moe_route",
    )(x, mod, gpre, rw, rb)


def _row_copy(src_hbm, dst_vmem, sem, src_row, dst_row):
    return pltpu.make_async_copy(src_hbm.at[pl.ds(pl.multiple_of(src_row * ROW_TILE, ROW_TILE), ROW_TILE)],
                                 dst_vmem.at[pl.ds(pl.multiple_of(dst_row * ROW_TILE, ROW_TILE), ROW_TILE)], sem)


def _start_rows(idx_ref, idx0, src_hbm, dst_vmem, sem, rows):
    def body(r, carry):
        _row_copy(src_hbm, dst_vmem, sem, idx_ref[idx0 + r], r).start()
        return carry
    lax.fori_loop(0, rows, body, 0, unroll=16)


def _wait_rows(src_hbm, dst_vmem, sem, rows):
    pltpu.make_async_copy(src_hbm.at[pl.ds(0, rows * ROW_TILE)], dst_vmem.at[pl.ds(0, rows * ROW_TILE)], sem).wait()


def _experts_body(te_ref, nv_ref, src_ref, h_hbm, wg_ref, wu_ref, wd_ref, o_ref, xbuf, hb_ref, acc_ref, sem):
    TM = EXPERT_TILE
    i = pl.program_id(0)
    f = pl.program_id(1)
    slot = i % 2
    n_valid = nv_ref[0]
    valid = i < n_valid

    @pl.when(f == 0)
    def _():
        @pl.when((i == 0) & valid)
        def _():
            _start_rows(src_ref, 0, h_hbm, xbuf.at[0], sem.at[0], TM)

        @pl.when(i + 1 < n_valid)
        def _():
            _start_rows(src_ref, (i + 1) * TM, h_hbm, xbuf.at[1 - slot], sem.at[1 - slot], TM)

        @pl.when(valid)
        def _():
            _wait_rows(h_hbm, xbuf.at[slot], sem.at[slot], TM)
            hb_ref[...] = _from_row_tiles(xbuf.at[slot], 0, TM).astype(BF16)
            acc_ref[...] = jnp.zeros_like(acc_ref)

    @pl.when(valid)
    def _():
        h = hb_ref[...]
        gate = _dot(h, wg_ref[...])
        up = _dot(h, wu_ref[...])
        acc_ref[...] += _dot(gate * _sigmoid(gate) * up, wd_ref[...])

    @pl.when(f == pl.num_programs(1) - 1)
    def _():
        @pl.when(valid)
        def _():
            _to_row_tiles(o_ref, acc_ref[...])

        @pl.when(jnp.logical_not(valid))
        def _():
            o_ref[...] = jnp.zeros_like(o_ref)


def _moe_experts(h_rows, tile_expert, n_valid, src_tok, wg, wu, wd, tf):
    TM = EXPERT_TILE
    P = src_tok.shape[0]
    E, D, F = wg.shape
    grid_spec = pltpu.PrefetchScalarGridSpec(
        num_scalar_prefetch=3,
        grid=(P // TM, F // tf),
        in_specs=[pl.BlockSpec(memory_space=pl.ANY),
                  pl.BlockSpec((None, D, tf), lambda i, f, te, nv, src: (te[i], 0, f)),
                  pl.BlockSpec((None, D, tf), lambda i, f, te, nv, src: (te[i], 0, f)),
                  pl.BlockSpec((None, tf, D), lambda i, f, te, nv, src: (te[i], f, 0))],
        out_specs=pl.BlockSpec((TM * ROW_TILE, LANES), lambda i, f, te, nv, src: (i, 0)),
        scratch_shapes=[pltpu.VMEM((2, TM * ROW_TILE, LANES), F32), pltpu.VMEM((TM, D), BF16),
                        pltpu.VMEM((TM, D), F32), pltpu.SemaphoreType.DMA((2,))],
    )
    return pl.pallas_call(
        _experts_body,
        out_shape=jax.ShapeDtypeStruct((P * ROW_TILE, LANES), F32),
        grid_spec=grid_spec,
        compiler_params=_params(("arbitrary", "arbitrary")),
        name="moe_experts",
    )(tile_expert, n_valid, src_tok, h_rows, wg, wu, wd)


def _combine_body(tiles_per_seq, prompt_row0, dest_ref, f_hbm, x_ref, mod_ref, info_ref, gpost_ref, o_ref, gbuf, sem):
    D = D_MODEL
    i = pl.program_id(0)
    T = x_ref.shape[0]
    slot = i % 2

    @pl.when(i == 0)
    def _():
        _start_rows(dest_ref, 0, f_hbm, gbuf.at[0], sem.at[0], 2 * T)

    @pl.when(i + 1 < pl.num_programs(0))
    def _():
        _start_rows(dest_ref, (i + 1) * 2 * T, f_hbm, gbuf.at[1 - slot], sem.at[1 - slot], 2 * T)

    _wait_rows(f_hbm, gbuf.at[slot], sem.at[slot], 2 * T)
    info = info_ref[...]
    f1 = _from_row_tiles(gbuf.at[slot], 0, T)
    f2 = _from_row_tiles(gbuf.at[slot], T, T)
    y = info[:, INFO_P1:INFO_P1 + 1] * f1 + info[:, INFO_P2:INFO_P2 + 1] * f2
    m = _mod_rows(mod_ref, False, tiles_per_seq, prompt_row0)
    o_ref[...] = x_ref[...] + m[:, 2 * D:3 * D] * _rms(y, gpost_ref[...])


def _moe_combine(dest, f_rows, x, mod, info, gpost, tm, tiles_per_seq, prompt_row0):
    N, D = x.shape
    row = lambda i, d: (i, 0)
    fixed = lambda i, d: (0, 0)
    grid_spec = pltpu.PrefetchScalarGridSpec(
        num_scalar_prefetch=1,
        grid=(N // tm,),
        in_specs=[pl.BlockSpec(memory_space=pl.ANY), pl.BlockSpec((tm, D), row),
                  pl.BlockSpec(mod.shape, fixed), pl.BlockSpec((tm, LANES), row), pl.BlockSpec((1, D), fixed)],
        out_specs=pl.BlockSpec((tm, D), row),
        scratch_shapes=[pltpu.VMEM((2, 2 * tm * ROW_TILE, LANES), F32), pltpu.SemaphoreType.DMA((2,))],
    )
    return pl.pallas_call(
        functools.partial(_combine_body, tiles_per_seq, prompt_row0),
        out_shape=jax.ShapeDtypeStruct((N, D), F32),
        grid_spec=grid_spec,
        compiler_params=_params(("arbitrary",)),
        name="moe_combine",
    )(dest, f_rows, x, mod, info, gpost)


def _moe_prompt(x, mod, gpre, gpost, fw, tm, tiles_per_seq, prompt_row0, tf):
    N = x.shape[0]
    TM = EXPERT_TILE
    n_tiles = (2 * N) // TM + N_EXPERTS
    h_rows, info, cnt = _moe_route(x, mod, gpre, fw["rw"], fw["rb"], tm, tiles_per_seq, prompt_row0)
    e1 = info[:, INFO_E1].astype(jnp.int32)
    e2 = info[:, INFO_E2].astype(jnp.int32)
    counts = cnt[0, :N_EXPERTS].astype(jnp.int32)
    padded = ((counts + TM - 1) // TM) * TM
    ends = jnp.cumsum(padded)
    starts = ends - padded
    dest1 = starts[e1] + info[:, INFO_POS1].astype(jnp.int32)
    dest2 = starts[e2] + info[:, INFO_POS2].astype(jnp.int32)
    tok = jnp.arange(N, dtype=jnp.int32)
    src_tok = jnp.zeros((n_tiles * TM,), jnp.int32).at[dest1].set(tok).at[dest2].set(tok)
    tile_start = jnp.arange(n_tiles, dtype=jnp.int32) * TM
    tile_expert = jnp.minimum(jnp.sum(ends[None, :] <= tile_start[:, None], axis=1), N_EXPERTS - 1).astype(jnp.int32)
    n_valid = (ends[-1:] // TM).astype(jnp.int32)
    f_rows = _moe_experts(h_rows, tile_expert, n_valid, src_tok, fw["wg"], fw["wu"], fw["wd"], tf)
    dest = jnp.concatenate([dest1.reshape(-1, 1, tm), dest2.reshape(-1, 1, tm)], axis=1).reshape(-1)
    return _moe_combine(dest, f_rows, x, mod, info, gpost, tm, tiles_per_seq, prompt_row0)


def _layer_weights(p, l):
    W = RWKV_WIDTH
    heads = np.arange(W) // HEAD_DIM
    headsum = jnp.asarray((heads[:, None] == heads[None, :]).astype(np.float32))
    perm = _q_perm()
    w_in = p["w_in"][l]
    w_in = jnp.concatenate([w_in[:, :RWKV_COLS], w_in[:, RWKV_COLS + perm], w_in[:, RWKV_COLS + SWA_WIDTH:]], axis=1)
    w_out = p["w_out"][l]
    zeros = jnp.zeros((DECAY_LORA, W), F32)
    row = lambda t: t.reshape(1, -1)
    return {
        "w_in": w_in.astype(BF16),
        "w_out_r": w_out[:W].astype(BF16),
        "w_out_a": w_out[W + perm].astype(BF16),
        "mu": row(p["mu_shift"][l]),
        "wd": jnp.concatenate([p["w_decay_up"][l], zeros], axis=0),
        "wi": jnp.concatenate([zeros, p["w_iclr_up"][l]], axis=0),
        "wg": p["w_gate_up"][l],
        "dbase": row(p["decay_base"][l]), "ibase": row(p["iclr_base"][l]),
        "kk": row(p["k_k"][l]), "ka": row(p["k_a"][l]), "rk": row(p["r_k"][l]),
        "lnw": row(p["lnx_w"][l]), "lnb": row(p["lnx_b"][l]),
        "headsum": headsum.astype(BF16), "headmean": (headsum / HEAD_DIM).astype(BF16),
        "g_pre0": row(p["norm_pre"][l, 0]), "g_pre1": row(p["norm_pre"][l, 1]),
        "g_post0": row(p["norm_post"][l, 0]), "g_post1": row(p["norm_post"][l, 1]),
    }


def _ffn_weights(p, l):
    i = l // 2
    if l % 2 == 0:
        return dict(moe=False, rw=jnp.zeros((D_MODEL, LANES), F32), rb=jnp.zeros((1, LANES), F32),
                    wg=p["ffn_w_gate"][i][None].astype(BF16), wu=p["ffn_w_up"][i][None].astype(BF16),
                    wd=p["ffn_w_down"][i][None].astype(BF16))
    rw = jnp.zeros((D_MODEL, LANES), F32).at[:, :N_EXPERTS].set(p["router_w"][i])
    rb = jnp.full((1, LANES), NEG_INF, F32).at[0, :N_EXPERTS].set(p["router_b"][i])
    return dict(moe=True, rw=rw, rb=rb, wg=p["moe_w_gate"][i].astype(BF16), wu=p["moe_w_up"][i].astype(BF16),
                wd=p["moe_w_down"][i].astype(BF16))


def _pick_tile(n, pref):
    t = min(pref, n)
    while n % t:
        t //= 2
    return t


def _ffn_tile(f):
    for t in (1408, 896, 512, 256, 128):
        if f % t == 0:
            return t
    return f


def _trunk(x3, mods, lws, fws, p, prompt, prompt_row0, state=None):
    B, T, D = x3.shape
    N = B * T
    x = x3.reshape(N, D)
    per_row = not prompt
    tm = _pick_tile(T if prompt else N, 512)
    tps = (T // tm) if prompt else 1
    depth = len(lws)
    wkv_out, shift_out, k_out, v_out = [], [], [], []
    for l in range(depth):
        lw, fw = lws[l], fws[l]
        mod0, mod1 = mods[2 * l], mods[2 * l + 1]
        pr, q, k, v = _inproj(x, mod0, lw["g_pre0"], lw["w_in"], per_row, tm, tps, prompt_row0)
        prev = pr if prompt else state["shift"][l]
        r, ld, kh, vv, a, b, g, bonus = _prep(pr, prev, lw, per_row, tm, tps)
        if prompt:
            yw, hbd = _wkv_prompt(r, ld, kh, vv, a, b, B, T)
            n_pairs = RWKV_WIDTH // PAIR
            hb = hbd.reshape(B, n_pairs, 2, HEAD_DIM, 2, HEAD_DIM)
            s_kv = jnp.stack([hb[:, :, 0, :, 0, :], hb[:, :, 1, :, 1, :]], axis=2)
            s_new = jnp.swapaxes(s_kv.reshape(B, RWKV_HEADS, HEAD_DIM, HEAD_DIM), -1, -2)
            ya = _swa_prompt(q, k, v, p["rel_bias"], p["attn_sinks"][l], B, T)
            kb = k.reshape(B, T, SWA_KV_HEADS, HEAD_DIM)[:, -WINDOW:]
            vb = v.reshape(B, T, SWA_KV_HEADS, HEAD_DIM)[:, -WINDOW:]
            last = pr.reshape(B, T, RWKV_COLS)[:, -1]
        else:
            yw, s_new = _wkv_sample(state["wkv"][l], r, ld, kh, vv, a, b)
            ya, kb, vb = _swa_sample(q, k, v, state["k"][l], state["v"][l], p["rel_bias"], p["attn_sinks"][l])
            last = pr
        x = _outproj(x, mod0, yw, g, bonus, ya, lw, per_row, tm, tps, prompt_row0)
        tf = _ffn_tile(fw["wg"].shape[-1])
        if prompt and fw["moe"]:
            x = _moe_prompt(x, mod1, lw["g_pre1"], lw["g_post1"], fw, tm, tps, prompt_row0, tf)
        else:
            x = _ffn(x, mod1, lw["g_pre1"], lw["g_post1"], fw["rw"], fw["rb"], fw["wg"], fw["wu"], fw["wd"],
                     fw["moe"], per_row, tm, tps, prompt_row0, tf)
        wkv_out.append(s_new)
        shift_out.append(last)
        k_out.append(kb)
        v_out.append(vb)
    return x.reshape(B, T, D), jnp.stack(wkv_out), jnp.stack(shift_out), jnp.stack(k_out), jnp.stack(v_out)


def _forward(x_prompt, x_sample, c_prompt, c_sample, state_wkv, state_shift, cache_swa_k, cache_swa_v, p):
    depth = p["w_in"].shape[0]
    Bp, Bs = c_prompt.shape[0], c_sample.shape[0]
    D = D_MODEL
    pad = (-(Bs + Bp)) % SUBLANES
    c_all = jnp.concatenate([c_sample, c_prompt, jnp.zeros((pad, D), F32)], axis=0)
    mods = _ada_all(c_all, p["ada_w"].reshape(2 * depth, D, 3 * D), p["ada_b"].reshape(2 * depth, 1, 3 * D))
    lws = [_layer_weights(p, l) for l in range(depth)]
    fws = [_ffn_weights(p, l) for l in range(depth)]
    y_p, wkv_p, shift_p, k_p, v_p = _trunk(x_prompt, mods, lws, fws, p, True, Bs)
    state = {"wkv": state_wkv, "shift": state_shift, "k": cache_swa_k, "v": cache_swa_v}
    y_s, wkv_s, shift_s, k_s, v_s = _trunk(x_sample, mods, lws, fws, p, False, Bs, state)
    return (y_p, y_s, wkv_p, shift_p, k_p, v_p, wkv_s, shift_s, k_s, v_s)


def kernel(x_prompt, x_sample, c_prompt, c_sample, state_wkv, state_shift, cache_swa_k, cache_swa_v, rel_bias, ada_w, ada_b, norm_pre, norm_post, w_in, mu_shift, w_decay_up, decay_base, w_iclr_up, iclr_base, w_gate_up, k_k, k_a, r_k, lnx_w, lnx_b, attn_sinks, w_out, ffn_w_gate, ffn_w_up, ffn_w_down, router_w, router_b, moe_w_gate, moe_w_up, moe_w_down):
    p = {"rel_bias": rel_bias, "ada_w": ada_w, "ada_b": ada_b, "norm_pre": norm_pre, "norm_post": norm_post,
         "w_in": w_in, "mu_shift": mu_shift, "w_decay_up": w_decay_up, "decay_base": decay_base,
         "w_iclr_up": w_iclr_up, "iclr_base": iclr_base, "w_gate_up": w_gate_up, "k_k": k_k, "k_a": k_a,
         "r_k": r_k.reshape(r_k.shape[0], -1), "lnx_w": lnx_w, "lnx_b": lnx_b, "attn_sinks": attn_sinks,
         "w_out": w_out, "ffn_w_gate": ffn_w_gate, "ffn_w_up": ffn_w_up, "ffn_w_down": ffn_w_down,
         "router_w": router_w, "router_b": router_b, "moe_w_gate": moe_w_gate, "moe_w_up": moe_w_up,
         "moe_w_down": moe_w_down}
    return _forward(x_prompt, x_sample, c_prompt, c_sample, state_wkv, state_shift, cache_swa_k, cache_swa_v, p)
```

```python
import functools

import numpy as np
import jax
import jax.numpy as jnp
from jax import lax
from jax.experimental import pallas as pl
from jax.experimental.pallas import tpu as pltpu

F32 = jnp.float32
BF16 = jnp.bfloat16

D_MODEL = 1024
HEAD_DIM = 64
RWKV_WIDTH = 512
RWKV_HEADS = RWKV_WIDTH // HEAD_DIM
SWA_WIDTH = 512
SWA_HEADS = SWA_WIDTH // HEAD_DIM
SWA_KV_HEADS = 2
SWA_GROUP = SWA_HEADS // SWA_KV_HEADS
KV_WIDTH = SWA_KV_HEADS * HEAD_DIM
WINDOW = 128
DECAY_LORA = 64
ICLR_LORA = 64
GATE_LORA = 128
RWKV_COLS = 3 * RWKV_WIDTH + DECAY_LORA + ICLR_LORA + GATE_LORA
IN_COLS = RWKV_COLS + SWA_WIDTH + 2 * KV_WIDTH
LORA_OFF = 3 * RWKV_WIDTH
GATE_OFF = LORA_OFF + DECAY_LORA + ICLR_LORA
LN_X_EPS = 64e-5
RMS_EPS = 1e-6
N_BUCKETS = 32
MAX_DISTANCE = 128
N_EXPERTS = 8
NEG_INF = -1e30

LANES = 128
SUBLANES = 8
VMEM_LIMIT = 56 * 1024 * 1024

WKV_CHUNK = 64
PAIR = 2 * HEAD_DIM

NN = (((1,), (0,)), ((), ()))
NT = (((1,), (1,)), ((), ()))
TN = (((0,), (0,)), ((), ()))


def _dot(a, b, dims=NN):
    return lax.dot_general(a.astype(BF16), b.astype(BF16), dims, preferred_element_type=F32)


def _split(x, pieces):
    out = []
    for _ in range(pieces - 1):
        hi = x.astype(BF16)
        out.append(hi)
        x = x - hi.astype(F32)
    out.append(x.astype(BF16))
    return out


def _select_dot(x, sel, pieces=3, sel_left=False):
    d = lambda t: lax.dot_general(*((sel, t) if sel_left else (t, sel)), NN, preferred_element_type=F32)
    return sum(d(t) for t in _split(x, pieces))


def _dot3(a, b):
    a_hi, a_lo = _split(a, 2)
    b_hi, b_lo = _split(b, 2)
    d = lambda x, y: lax.dot_general(x, y, NN, preferred_element_type=F32)
    return d(a_hi, b_hi) + d(a_lo, b_hi) + d(a_hi, b_lo)


def _sigmoid(x):
    return 1.0 / (1.0 + jnp.exp(-x))


def _params(sem):
    return pltpu.CompilerParams(dimension_semantics=sem, vmem_limit_bytes=VMEM_LIMIT)


def _rms(x, g):
    return x * lax.rsqrt(jnp.mean(x * x, axis=-1, keepdims=True) + RMS_EPS) * g


def _mod_rows(mod_ref, per_row, tiles_per_seq, prompt_row0):
    if per_row:
        return mod_ref[...]
    b = pl.program_id(0) // tiles_per_seq
    return mod_ref[pl.ds(prompt_row0 + b, 1), :]


def _ada_body(c_ref, w_ref, b_ref, o_ref):
    c = c_ref[...]
    o_ref[...] = _dot(c * _sigmoid(c), w_ref[...]) + b_ref[...]


def _ada_all(c_all, ada_w, ada_b):
    R, D = c_all.shape
    n = ada_w.shape[0]
    tn = 1024
    return pl.pallas_call(
        _ada_body,
        out_shape=jax.ShapeDtypeStruct((n, R, 3 * D), F32),
        grid=(n, 3 * D // tn),
        in_specs=[pl.BlockSpec((R, D), lambda i, j: (0, 0)),
                  pl.BlockSpec((None, D, tn), lambda i, j: (i, 0, j)),
                  pl.BlockSpec((None, 1, tn), lambda i, j: (i, 0, j))],
        out_specs=pl.BlockSpec((None, R, tn), lambda i, j: (i, 0, j)),
        compiler_params=_params(("arbitrary", "arbitrary")),
        name="ada_mod",
    )(c_all, ada_w, ada_b)


def _inproj_body(per_row, tiles_per_seq, prompt_row0, x_ref, mod_ref, g_ref, w_ref, pr_ref, q_ref, k_ref, v_ref):
    D = D_MODEL
    m = _mod_rows(mod_ref, per_row, tiles_per_seq, prompt_row0)
    h = _rms(x_ref[...], g_ref[...]) * (1.0 + m[:, D:2 * D]) + m[:, 0:D]
    proj = _dot(h, w_ref[...])
    pr_ref[...] = proj[:, 0:RWKV_COLS]
    q_ref[...] = proj[:, RWKV_COLS:RWKV_COLS + SWA_WIDTH]
    k_ref[...] = proj[:, RWKV_COLS + SWA_WIDTH:RWKV_COLS + SWA_WIDTH + KV_WIDTH]
    v_ref[...] = proj[:, RWKV_COLS + SWA_WIDTH + KV_WIDTH:IN_COLS]


def _mod_spec(mod, per_row, tm, nargs):
    R = mod.shape[0]
    if per_row:
        return pl.BlockSpec((tm, 3 * D_MODEL), lambda i, *_: (0, 0))
    return pl.BlockSpec((R, 3 * D_MODEL), lambda i, *_: (0, 0))


def _inproj(x, mod, g, w, per_row, tm, tiles_per_seq, prompt_row0):
    N, D = x.shape
    row = lambda i: (i, 0)
    fixed = lambda i: (0, 0)
    return pl.pallas_call(
        functools.partial(_inproj_body, per_row, tiles_per_seq, prompt_row0),
        out_shape=(jax.ShapeDtypeStruct((N, RWKV_COLS), F32), jax.ShapeDtypeStruct((N, SWA_WIDTH), F32),
                   jax.ShapeDtypeStruct((N, KV_WIDTH), F32), jax.ShapeDtypeStruct((N, KV_WIDTH), F32)),
        grid=(N // tm,),
        in_specs=[pl.BlockSpec((tm, D), row), _mod_spec(mod, per_row, tm, 1),
                  pl.BlockSpec((1, D), fixed), pl.BlockSpec((D, IN_COLS), fixed)],
        out_specs=(pl.BlockSpec((tm, RWKV_COLS), row), pl.BlockSpec((tm, SWA_WIDTH), row),
                   pl.BlockSpec((tm, KV_WIDTH), row), pl.BlockSpec((tm, KV_WIDTH), row)),
        compiler_params=_params(("arbitrary",)),
        name="in_proj",
    )(x, mod, g, w)


def _prep_body(per_row, tiles_per_seq, pr_ref, prev_ref, mu_ref, wd_ref, wi_ref, wg_ref, dbase_ref, ibase_ref,
               kk_ref, ka_ref, rk_ref, hs_ref,
               r_ref, ld_ref, k_ref, v_ref, a_ref, b_ref, g_ref, bonus_ref):
    W = RWKV_WIDTH
    pr = pr_ref[...]
    if per_row:
        shifted = prev_ref[...]
    else:
        first = (pl.program_id(0) % tiles_per_seq) == 0
        carry = jnp.where(first, 0.0, prev_ref[SUBLANES - 1:SUBLANES, :])
        rows = lax.broadcasted_iota(jnp.int32, pr.shape, 0)
        shifted = jnp.where(rows == 0, carry, pltpu.roll(pr, 1, 0))
    xs = pr + (shifted - pr) * mu_ref[...]
    r = xs[:, 0:W]
    k = xs[:, W:2 * W]
    v = xs[:, 2 * W:3 * W]
    lora = xs[:, LORA_OFF:GATE_OFF]
    gl = xs[:, GATE_OFF:RWKV_COLS]
    z = dbase_ref[...] + _dot3(jnp.tanh(lora), wd_ref[...])
    ld = -float(np.exp(-0.5)) * _sigmoid(z)
    iclr = _sigmoid(ibase_ref[...] + _dot3(lora, wi_ref[...]))
    g = _dot3(_sigmoid(gl), wg_ref[...])
    hs = hs_ref[...]
    kk = k * kk_ref[...]
    kk = kk / jnp.maximum(jnp.sqrt(_select_dot(kk * kk, hs, 2)), 1e-12)
    kh = k * (1.0 + (iclr - 1.0) * ka_ref[...])
    bonus = _select_dot(r * kh * rk_ref[...], hs, 2) * v
    r_ref[...] = r
    ld_ref[...] = ld
    k_ref[...] = kh
    v_ref[...] = v
    a_ref[...] = -kk
    b_ref[...] = kk * iclr
    g_ref[...] = g
    bonus_ref[...] = bonus


def _prep(pr, prev, lw, per_row, tm, tiles_per_seq):
    N = pr.shape[0]
    W = RWKV_WIDTH
    row = lambda i: (i, 0)
    fixed = lambda i: (0, 0)
    if per_row:
        prev_spec = pl.BlockSpec((tm, RWKV_COLS), row)
    else:
        per = tm // SUBLANES
        prev_spec = pl.BlockSpec((SUBLANES, RWKV_COLS), lambda i: (jnp.maximum(i * per - 1, 0), 0))
    vec = pl.BlockSpec((1, W), fixed)
    out = jax.ShapeDtypeStruct((N, W), F32)
    return pl.pallas_call(
        functools.partial(_prep_body, per_row, tiles_per_seq),
        out_shape=(out,) * 8,
        grid=(N // tm,),
        in_specs=[pl.BlockSpec((tm, RWKV_COLS), row), prev_spec, pl.BlockSpec((1, RWKV_COLS), fixed),
                  pl.BlockSpec((LANES, W), fixed), pl.BlockSpec((LANES, W), fixed), pl.BlockSpec((GATE_LORA, W), fixed),
                  vec, vec, vec, vec, vec, pl.BlockSpec((W, W), fixed)],
        out_specs=(pl.BlockSpec((tm, W), row),) * 8,
        compiler_params=_params(("arbitrary",)),
        name="rwkv_prep",
    )(pr, prev, lw["mu"], lw["wd"], lw["wi"], lw["wg"], lw["dbase"], lw["ibase"], lw["kk"], lw["ka"], lw["rk"],
      lw["headsum"])


def _stack_heads(x, lane_head0):
    return jnp.concatenate([jnp.where(lane_head0, x, 0.0), jnp.where(lane_head0, 0.0, x)], axis=0)


def _fold_heads(x):
    c = x.shape[0] // 2
    return x[0:c] + x[c:2 * c]


def _dots(xs, ys, dims=NN):
    return [_dot(x, y, dims) for x, y in zip(xs, ys)]


def _unit_lower_inverse(ns, same16, eye):
    nd = [jnp.where(same16, n, 0.0) for n in ns]
    no = [n - d for n, d in zip(ns, nd)]
    n2 = _dots(nd, nd)
    n4 = _dots(n2, n2)
    n8 = _dots(n4, n4)
    td = [eye + d for d in nd]
    for pw in (n2, n4, n8):
        td = [t + u for t, u in zip(td, _dots(td, pw))]
    q = _dots(td, no)
    q2 = _dots(q, q)
    z = [eye + x for x in q]
    z = [t + u for t, u in zip(z, _dots(z, q2))]
    return _dots(z, td)


def _wkv_chunk_body(chunks, r_ref, ld_ref, k_ref, v_ref, a_ref, b_ref, y_ref, s_ref, h_ref):
    C = WKV_CHUNK
    n_pairs = RWKV_WIDTH // PAIR

    @pl.when(pl.program_id(1) == 0)
    def _():
        h_ref[...] = jnp.zeros_like(h_ref)

    ri = lax.broadcasted_iota(jnp.int32, (PAIR, PAIR), 0)
    ci = lax.broadcasted_iota(jnp.int32, (PAIR, PAIR), 1)
    same_head = (ri // C) == (ci // C)
    strict_lower = same_head & (ci < ri)
    incl_lower = same_head & (ci <= ri)
    same16 = (ri // 16) == (ci // 16)
    eye_b = ri == ci
    eye = jnp.where(eye_b, 1.0, 0.0)
    tri = jnp.where(lax.broadcasted_iota(jnp.int32, (C, C), 1) <= lax.broadcasted_iota(jnp.int32, (C, C), 0),
                    1.0, 0.0).astype(BF16)
    lane_head0 = lax.broadcasted_iota(jnp.int32, (C, PAIR), 1) < HEAD_DIM
    zeros = jnp.zeros((PAIR, PAIR), F32)

    cat0 = lambda x, y: jnp.concatenate([x, y], axis=0)
    cat1 = lambda x, y: jnp.concatenate([x, y], axis=1)

    xa, xr, v_st, bh_st, kh_st, yb, yk, p_all = [], [], [], [], [], [], [], []
    for c in range(chunks):
        rows = slice(c * C, (c + 1) * C)
        ld = ld_ref[rows, :]
        cum = _select_dot(ld, tri, 3, sel_left=True)
        last = cum[C - 1:C, :]
        p_inv = jnp.exp(-cum)
        p_tail = jnp.exp(last - cum)
        p_end = jnp.exp(last)
        a_t = a_ref[rows, :] * jnp.exp(cum - ld)
        r_t = r_ref[rows, :] * jnp.exp(cum)
        b_raw = b_ref[rows, :]
        k_raw = k_ref[rows, :]
        b_t = b_raw * p_inv
        k_t = k_raw * p_inv
        b_h = b_raw * p_tail
        k_h = k_raw * p_tail
        v_all = v_ref[rows, :]
        for j in range(n_pairs):
            lanes = slice(j * PAIR, (j + 1) * PAIR)
            xa.append(_stack_heads(a_t[:, lanes], lane_head0))
            xr.append(_stack_heads(r_t[:, lanes], lane_head0))
            v_st.append(_stack_heads(v_all[:, lanes], lane_head0))
            bh_st.append(_stack_heads(b_h[:, lanes], lane_head0))
            kh_st.append(_stack_heads(k_h[:, lanes], lane_head0))
            yb.append(cat0(b_t[:, lanes], b_t[:, lanes]))
            yk.append(cat0(k_t[:, lanes], k_t[:, lanes]))
            p_all.append(p_end[:, lanes])

    gram = _dots([cat0(x, y) for x, y in zip(xa, xr)], [cat0(x, y) for x, y in zip(yb, yk)], NT)
    n_mat = [jnp.where(strict_lower, g[0:PAIR, 0:PAIR], 0.0) for g in gram]
    m_mat = [jnp.where(strict_lower, g[0:PAIR, PAIR:2 * PAIR], 0.0) for g in gram]
    a_rbk = [cat1(jnp.where(incl_lower, g[PAIR:2 * PAIR, 0:PAIR], 0.0),
                  jnp.where(incl_lower, g[PAIR:2 * PAIR, PAIR:2 * PAIR], 0.0)) for g in gram]
    t_inv = _unit_lower_inverse(n_mat, same16, eye)
    mv = _dots(m_mat, v_st)
    tx = _dots(t_inv, [cat1(x, y) for x, y in zip(xa, mv)])
    rhs = [cat0(t, cat1(zeros, v)) for t, v in zip(tx, v_st)]
    ry = _dots(a_rbk, rhs)
    pp = _dots([cat0(x, y) for x, y in zip(bh_st, kh_st)], rhs, TN)

    for c in range(chunks):
        us = [c * n_pairs + j for j in range(n_pairs)]
        h0 = [h_ref[j] for j in range(n_pairs)]
        r_bar = [_fold_heads(xr[u] + ry[u][:, 0:PAIR]) for u in us]
        phi = [pp[u][:, 0:PAIR] + jnp.where(eye_b, p_all[u], 0.0) for u in us]
        ys = _dots(r_bar, h0)
        hs = _dots(phi, h0)
        for j, u in enumerate(us):
            y_ref[c * C:(c + 1) * C, j * PAIR:(j + 1) * PAIR] = ys[j] + _fold_heads(ry[u][:, PAIR:2 * PAIR])
            h_ref[j] = hs[j] + pp[u][:, PAIR:2 * PAIR]

    @pl.when(pl.program_id(1) == pl.num_programs(1) - 1)
    def _():
        s_ref[...] = h_ref[...]


def _wkv_prompt(r, ld, k, v, a, b, batch, seq):
    N, W = r.shape
    chunks = 4
    tt = chunks * WKV_CHUNK
    steps = seq // tt
    n_pairs = W // PAIR
    row = lambda bb, t: (bb * steps + t, 0)
    spec = pl.BlockSpec((tt, W), row)
    return pl.pallas_call(
        functools.partial(_wkv_chunk_body, chunks),
        out_shape=(jax.ShapeDtypeStruct((N, W), F32), jax.ShapeDtypeStruct((batch, n_pairs, PAIR, PAIR), F32)),
        grid=(batch, steps),
        in_specs=[spec] * 6,
        out_specs=(spec, pl.BlockSpec((None, n_pairs, PAIR, PAIR), lambda bb, t: (bb, 0, 0, 0))),
        scratch_shapes=[pltpu.VMEM((n_pairs, PAIR, PAIR), F32)],
        compiler_params=_params(("arbitrary", "arbitrary")),
        name="wkv_chunk_scan",
    )(r, ld, k, v, a, b)


def _wkv_step_body(s_ref, r_ref, ld_ref, k_ref, v_ref, a_ref, b_ref, exp_ref, red_ref, y_ref, so_ref):
    HD = HEAD_DIM
    nrep = HD * HD // LANES
    lane = lax.broadcasted_iota(jnp.int32, r_ref.shape, 1)
    low = lane < HD
    y = jnp.zeros(r_ref.shape, F32)
    for hh in range(2):
        def tiled(ref, fn=None):
            x = ref[...]
            if fn is not None:
                x = fn(x)
            sw = pltpu.roll(x, HD, 1)
            both = jnp.where(low, x, sw) if hh == 0 else jnp.where(low, sw, x)
            return jnp.tile(both, (1, nrep))
        cols = slice(hh * HD * HD, (hh + 1) * HD * HD)
        s = s_ref[:, cols]
        expand = exp_ref[hh]
        reduce_ = red_ref[hh]
        sa = _select_dot(s * tiled(a_ref), reduce_)
        s_new = (s * tiled(ld_ref, jnp.exp) + _select_dot(sa, expand) * tiled(b_ref)
                 + _select_dot(v_ref[...], expand) * tiled(k_ref))
        so_ref[:, cols] = s_new
        y = y + _select_dot(s_new * tiled(r_ref), reduce_)
    y_ref[...] = y


def _wkv_step_consts():
    HD = HEAD_DIM
    expand = np.zeros((2, PAIR, HD * HD), np.float32)
    for hh in range(2):
        for vv in range(HD):
            expand[hh, hh * HD + vv, vv * HD:(vv + 1) * HD] = 1.0
    return jnp.asarray(expand, BF16), jnp.asarray(expand.transpose(0, 2, 1), BF16)


def _wkv_sample(state, r, ld, k, v, a, b):
    B = state.shape[0]
    W = RWKV_WIDTH
    HD2 = HEAD_DIM * HEAD_DIM
    expand, reduce_ = _wkv_step_consts()
    s2 = state.reshape(B, RWKV_HEADS * HD2)
    st_spec = pl.BlockSpec((B, 2 * HD2), lambda j: (0, j))
    vec = pl.BlockSpec((B, PAIR), lambda j: (0, j))
    y, s_new = pl.pallas_call(
        _wkv_step_body,
        out_shape=(jax.ShapeDtypeStruct((B, W), F32), jax.ShapeDtypeStruct((B, RWKV_HEADS * HD2), F32)),
        grid=(W // PAIR,),
        in_specs=[st_spec] + [vec] * 6 + [pl.BlockSpec((2, PAIR, HD2), lambda j: (0, 0, 0)),
                                          pl.BlockSpec((2, HD2, PAIR), lambda j: (0, 0, 0))],
        out_specs=(vec, st_spec),
        compiler_params=_params(("arbitrary",)),
        name="wkv_step",
    )(s2, r, ld, k, v, a, b, expand, reduce_)
    return y, s_new.reshape(state.shape)


def _q_perm():
    return np.array([(h * SWA_GROUP + g) * HEAD_DIM + d for g in range(SWA_GROUP) for h in range(SWA_KV_HEADS)
                     for d in range(HEAD_DIM)], np.int32)


def _rel_bucket_np(dist):
    max_exact = N_BUCKETS // 2
    d = np.maximum(dist, 0)
    ratio = np.log(np.maximum(d, 1).astype(np.float32) / np.float32(max_exact)) / np.float32(
        np.log(MAX_DISTANCE / max_exact))
    large = np.minimum(max_exact + (ratio.astype(np.float32) * np.float32(N_BUCKETS - max_exact)).astype(np.int32),
                       N_BUCKETS - 1)
    return np.where(d < max_exact, d, large).astype(np.int32)


def _bias_from_buckets(idx, rb_ref, head):
    acc = jnp.zeros(idx.shape, F32)
    for bk in range(N_BUCKETS):
        acc = jnp.where(idx == bk, rb_ref[bk, head], acc)
    return acc


def _swa_prompt_body(rb_ref, sink_ref, idx_ref, q_ref, kp_ref, kc_ref, vp_ref, vc_ref, o_ref, bias_ref):
    Q = WINDOW
    first = (pl.program_id(0) == 0) & (pl.program_id(1) == 0)

    @pl.when(first)
    def _():
        idx = idx_ref[...]
        for g in range(SWA_GROUP):
            for h in range(SWA_KV_HEADS):
                bias_ref[g * SWA_KV_HEADS + h] = _bias_from_buckets(idx, rb_ref, h * SWA_GROUP + g)

    n = pl.program_id(1)
    qi = lax.broadcasted_iota(jnp.int32, (Q, 2 * Q), 0)
    kj = lax.broadcasted_iota(jnp.int32, (Q, 2 * Q), 1)
    valid = ((kj < Q) & (kj >= qi) & (n > 0)) | ((kj >= Q) & ((kj - Q) <= qi))
    lane_kv0 = lax.broadcasted_iota(jnp.int32, (Q, LANES), 1) < HEAD_DIM
    kcat = jnp.concatenate([kp_ref[...], kc_ref[...]], axis=0).astype(BF16)
    vcat = jnp.concatenate([vp_ref[...], vc_ref[...]], axis=0).astype(BF16)
    scale = HEAD_DIM ** -0.5
    for g in range(SWA_GROUP):
        qg = q_ref[:, g * LANES:(g + 1) * LANES]
        outs = []
        for h in range(SWA_KV_HEADS):
            qm = jnp.where(lane_kv0, qg, 0.0) if h == 0 else jnp.where(lane_kv0, 0.0, qg)
            s = _dot(qm, kcat, NT) * scale + bias_ref[g * SWA_KV_HEADS + h]
            s = jnp.where(valid, s, NEG_INF)
            sink = sink_ref[h * SWA_GROUP + g]
            m = jnp.maximum(jnp.max(s, axis=-1, keepdims=True), sink)
            p = jnp.exp(s - m)
            den = jnp.sum(p, axis=-1, keepdims=True) + jnp.exp(sink - m)
            outs.append(_dot(p, vcat) / den)
        o_ref[:, g * LANES:(g + 1) * LANES] = jnp.where(lane_kv0, outs[0], outs[1])


def _swa_prompt(q, k, v, rel_bias, sinks, batch, seq):
    N = q.shape[0]
    Q = WINDOW
    nb = seq // Q
    qi = np.arange(Q)[:, None]
    kj = np.arange(2 * Q)[None, :]
    idx = jnp.asarray(_rel_bucket_np(qi + Q - kj))
    cur = lambda bb, n: (bb * nb + n, 0)
    prev = lambda bb, n: (bb * nb + jnp.maximum(n - 1, 0), 0)
    kv_c = pl.BlockSpec((Q, KV_WIDTH), cur)
    kv_p = pl.BlockSpec((Q, KV_WIDTH), prev)
    smem = pl.BlockSpec(memory_space=pltpu.SMEM)
    return pl.pallas_call(
        _swa_prompt_body,
        out_shape=jax.ShapeDtypeStruct((N, SWA_WIDTH), F32),
        grid=(batch, nb),
        in_specs=[smem, smem, pl.BlockSpec((Q, 2 * Q), lambda bb, n: (0, 0)),
                  pl.BlockSpec((Q, SWA_WIDTH), cur), kv_p, kv_c, kv_p, kv_c],
        out_specs=pl.BlockSpec((Q, SWA_WIDTH), cur),
        scratch_shapes=[pltpu.VMEM((SWA_HEADS, Q, 2 * Q), F32)],
        compiler_params=_params(("arbitrary", "arbitrary")),
        name="swa_prompt",
    )(rel_bias, sinks, idx, q, k, k, v, v)


def _swa_sample_body(rb_ref, sink_ref, idx_ref, q_ref, kn_ref, vn_ref, kb_ref, vb_ref, o_ref, ko_ref, vo_ref,
                     bias_ref, extra_ref):
    W = WINDOW
    G, KVH = SWA_GROUP, SWA_KV_HEADS

    @pl.when(pl.program_id(0) == 0)
    def _():
        idx = jnp.broadcast_to(idx_ref[...], (SUBLANES, W))
        row = lax.broadcasted_iota(jnp.int32, (SUBLANES, W), 0)
        acc = jnp.zeros((SUBLANES, W), F32)
        ext = jnp.zeros((SUBLANES, LANES), F32)
        lane = lax.broadcasted_iota(jnp.int32, (SUBLANES, LANES), 1)
        for h in range(KVH):
            for g in range(G):
                head = h * G + g
                r = h * G + g
                acc = jnp.where(row == r, _bias_from_buckets(idx, rb_ref, head), acc)
                ext = jnp.where((row == r) & (lane == 0), rb_ref[0, head], ext)
                ext = jnp.where((row == r) & (lane == 1), sink_ref[head], ext)
        bias_ref[...] = acc
        extra_ref[...] = ext

    TB = q_ref.shape[0]
    lane_kv0 = lax.broadcasted_iota(jnp.int32, (TB, G, LANES), 2) < HEAD_DIM
    q4 = q_ref[...]
    qrows = jnp.concatenate([jnp.where(lane_kv0, q4, 0.0), jnp.where(lane_kv0, 0.0, q4)], axis=1)
    kb = kb_ref[...]
    vb = vb_ref[...]
    kn = kn_ref[...]
    vn = vn_ref[...]
    scale = HEAD_DIM ** -0.5
    bdims = (((2,), (2,)), ((0,), (0,)))
    s = lax.dot_general(qrows.astype(BF16), kb.astype(BF16), bdims, preferred_element_type=F32) * scale
    s = s + bias_ref[...][None]
    s_self = jnp.sum(qrows * kn, axis=-1, keepdims=True) * scale + extra_ref[:, 0:1][None]
    sink = extra_ref[:, 1:2][None]
    m = jnp.maximum(jnp.maximum(jnp.max(s, axis=-1, keepdims=True), s_self), sink)
    p = jnp.exp(s - m)
    p_self = jnp.exp(s_self - m)
    den = jnp.sum(p, axis=-1, keepdims=True) + p_self + jnp.exp(sink - m)
    pv = lax.dot_general(p.astype(BF16), vb.astype(BF16), (((2,), (1,)), ((0,), (0,))), preferred_element_type=F32)
    o = (pv + p_self * vn) / den
    o_ref[...] = jnp.where(lane_kv0, o[:, 0:G], o[:, G:2 * G])

    rowmod = lax.broadcasted_iota(jnp.int32, (TB, W, KV_WIDTH), 1)
    for buf, new, out in ((kb, kn, ko_ref), (vb, vn, vo_ref)):
        rolled = pltpu.roll(buf.reshape(TB * W, KV_WIDTH), TB * W - 1, 0).reshape(TB, W, KV_WIDTH)
        out[...] = jnp.where(rowmod == W - 1, new, rolled)


def _swa_sample(q, k, v, kbuf, vbuf, rel_bias, sinks):
    B = q.shape[0]
    W = WINDOW
    tb = 16
    idx = jnp.asarray(_rel_bucket_np(W - np.arange(W))[None, :])
    q4 = q.reshape(B, SWA_GROUP, LANES)
    kn = k.reshape(B, 1, KV_WIDTH)
    vn = v.reshape(B, 1, KV_WIDTH)
    kb = kbuf.reshape(B, W, KV_WIDTH)
    vb = vbuf.reshape(B, W, KV_WIDTH)
    smem = pl.BlockSpec(memory_space=pltpu.SMEM)
    b3 = lambda i: (i, 0, 0)
    o, ko, vo = pl.pallas_call(
        _swa_sample_body,
        out_shape=(jax.ShapeDtypeStruct((B, SWA_GROUP, LANES), F32), jax.ShapeDtypeStruct((B, W, KV_WIDTH), F32),
                   jax.ShapeDtypeStruct((B, W, KV_WIDTH), F32)),
        grid=(B // tb,),
        in_specs=[smem, smem, pl.BlockSpec((1, W), lambda i: (0, 0)),
                  pl.BlockSpec((tb, SWA_GROUP, LANES), b3), pl.BlockSpec((tb, 1, KV_WIDTH), b3),
                  pl.BlockSpec((tb, 1, KV_WIDTH), b3), pl.BlockSpec((tb, W, KV_WIDTH), b3),
                  pl.BlockSpec((tb, W, KV_WIDTH), b3)],
        out_specs=(pl.BlockSpec((tb, SWA_GROUP, LANES), b3), pl.BlockSpec((tb, W, KV_WIDTH), b3),
                   pl.BlockSpec((tb, W, KV_WIDTH), b3)),
        scratch_shapes=[pltpu.VMEM((SUBLANES, W), F32), pltpu.VMEM((SUBLANES, LANES), F32)],
        compiler_params=_params(("arbitrary",)),
        name="swa_sample",
    )(rel_bias, sinks, idx, q4, kn, vn, kb, vb)
    return o.reshape(B, SWA_WIDTH), ko.reshape(kbuf.shape), vo.reshape(vbuf.shape)


def _outproj_body(per_row, tiles_per_seq, prompt_row0, x_ref, mod_ref, yw_ref, g_ref, bonus_ref, ya_ref,
                  lnw_ref, lnb_ref, hm_ref, wr_ref, wa_ref, gpost_ref, o_ref):
    D = D_MODEL
    m = _mod_rows(mod_ref, per_row, tiles_per_seq, prompt_row0)
    yw = yw_ref[...]
    hm = hm_ref[...]
    mean = _select_dot(yw, hm, 2)
    dv = yw - mean
    var = _select_dot(dv * dv, hm, 2)
    yn = dv * lax.rsqrt(var + LN_X_EPS) * lnw_ref[...] + lnb_ref[...]
    yr = (yn + bonus_ref[...]) * g_ref[...]
    mix = _dot(yr, wr_ref[...]) + _dot(ya_ref[...], wa_ref[...])
    o_ref[...] = x_ref[...] + m[:, 2 * D:3 * D] * _rms(mix, gpost_ref[...])


def _outproj(x, mod, yw, g, bonus, ya, lw, per_row, tm, tiles_per_seq, prompt_row0):
    N, D = x.shape
    W = RWKV_WIDTH
    row = lambda i: (i, 0)
    fixed = lambda i: (0, 0)
    half = pl.BlockSpec((tm, W), row)
    vec = pl.BlockSpec((1, W), fixed)
    return pl.pallas_call(
        functools.partial(_outproj_body, per_row, tiles_per_seq, prompt_row0),
        out_shape=jax.ShapeDtypeStruct((N, D), F32),
        grid=(N // tm,),
        in_specs=[pl.BlockSpec((tm, D), row), _mod_spec(mod, per_row, tm, 1), half, half, half, half, vec, vec,
                  pl.BlockSpec((W, W), fixed), pl.BlockSpec((W, D), fixed), pl.BlockSpec((W, D), fixed),
                  pl.BlockSpec((1, D), fixed)],
        out_specs=pl.BlockSpec((tm, D), row),
        compiler_params=_params(("arbitrary",)),
        name="out_proj",
    )(x, mod, yw, g, bonus, ya, lw["lnw"], lw["lnb"], lw["headmean"], lw["w_out_r"], lw["w_out_a"], lw["g_post0"])


def _ffn_body(moe, per_row, tiles_per_seq, prompt_row0, x_ref, mod_ref, gpre_ref, gpost_ref, rw_ref, rb_ref,
              wg_ref, wu_ref, wd_ref, o_ref, h_ref, acc_ref, comb_ref):
    D = D_MODEL
    e = pl.program_id(1)
    f = pl.program_id(2)
    first = (e == 0) & (f == 0)
    last = (e == pl.num_programs(1) - 1) & (f == pl.num_programs(2) - 1)

    @pl.when(first)
    def _():
        m = _mod_rows(mod_ref, per_row, tiles_per_seq, prompt_row0)
        h = _rms(x_ref[...], gpre_ref[...]) * (1.0 + m[:, D:2 * D]) + m[:, 0:D]
        h_ref[...] = h.astype(BF16)
        acc_ref[...] = jnp.zeros_like(acc_ref)
        if moe:
            logits = _dot3(h, rw_ref[...]) + rb_ref[...]
            lane = lax.broadcasted_iota(jnp.int32, logits.shape, 1)
            m1 = jnp.max(logits, axis=-1, keepdims=True)
            i1 = jnp.min(jnp.where(logits == m1, lane, LANES), axis=-1, keepdims=True)
            rest = jnp.where(lane == i1, -jnp.inf, logits)
            m2 = jnp.max(rest, axis=-1, keepdims=True)
            i2 = jnp.min(jnp.where(rest == m2, lane, LANES), axis=-1, keepdims=True)
            e2 = jnp.exp(m2 - m1)
            comb_ref[...] = jnp.where(lane == i1, 1.0 / (1.0 + e2), 0.0) + jnp.where(lane == i2, e2 / (1.0 + e2), 0.0)

    h = h_ref[...]
    gate = _dot(h, wg_ref[...])
    up = _dot(h, wu_ref[...])
    act = gate * _sigmoid(gate) * up
    if moe:
        comb = comb_ref[...]
        lane = lax.broadcasted_iota(jnp.int32, comb.shape, 1)
        act = act * jnp.sum(jnp.where(lane == e, comb, 0.0), axis=-1, keepdims=True)
    acc_ref[...] += _dot(act, wd_ref[...])

    @pl.when(last)
    def _():
        m = _mod_rows(mod_ref, per_row, tiles_per_seq, prompt_row0)
        o_ref[...] = x_ref[...] + m[:, 2 * D:3 * D] * _rms(acc_ref[...], gpost_ref[...])


def _ffn(x, mod, gpre, gpost, router_w, router_b, wg, wu, wd, w0, E, moe, per_row, tm, tiles_per_seq, prompt_row0,
         tf):
    N, D = x.shape
    F = wg.shape[-1]
    row = lambda i, e, f: (i, 0)
    fixed = lambda i, e, f: (0, 0)
    return pl.pallas_call(
        functools.partial(_ffn_body, moe, per_row, tiles_per_seq, prompt_row0),
        out_shape=jax.ShapeDtypeStruct((N, D), F32),
        grid=(N // tm, E, F // tf),
        in_specs=[pl.BlockSpec((tm, D), row), _mod_spec(mod, per_row, tm, 3),
                  pl.BlockSpec((1, D), fixed), pl.BlockSpec((1, D), fixed),
                  pl.BlockSpec((D, LANES), fixed), pl.BlockSpec((1, LANES), fixed),
                  pl.BlockSpec((None, D, tf), lambda i, e, f: (w0 + e, 0, f)),
                  pl.BlockSpec((None, D, tf), lambda i, e, f: (w0 + e, 0, f)),
                  pl.BlockSpec((None, tf, D), lambda i, e, f: (w0 + e, f, 0))],
        out_specs=pl.BlockSpec((tm, D), row),
        scratch_shapes=[pltpu.VMEM((tm, D), BF16), pltpu.VMEM((tm, D), F32), pltpu.VMEM((tm, LANES), F32)],
        compiler_params=_params(("arbitrary", "arbitrary", "arbitrary")),
        name="moe_ffn" if moe else "dense_ffn",
    )(x, mod, gpre, gpost, router_w, router_b, wg, wu, wd)


ROW_TILE = D_MODEL // LANES
EXPERT_TILE = 512
INFO_E1, INFO_E2, INFO_P1, INFO_P2, INFO_POS1, INFO_POS2 = range(6)


def _top2(logits):
    lane = lax.broadcasted_iota(jnp.int32, logits.shape, 1)
    m1 = jnp.max(logits, axis=-1, keepdims=True)
    i1 = jnp.min(jnp.where(logits == m1, lane, LANES), axis=-1, keepdims=True)
    rest = jnp.where(lane == i1, -jnp.inf, logits)
    m2 = jnp.max(rest, axis=-1, keepdims=True)
    i2 = jnp.min(jnp.where(rest == m2, lane, LANES), axis=-1, keepdims=True)
    e2 = jnp.exp(m2 - m1)
    return lane, i1, i2, 1.0 / (1.0 + e2), e2 / (1.0 + e2)


def _to_row_tiles(ref, x):
    rows = x.shape[0]
    for c in range(ROW_TILE):
        ref[pl.ds(c, rows, stride=ROW_TILE), :] = x[:, c * LANES:(c + 1) * LANES]


def _from_row_tiles(ref, row0, rows):
    return jnp.concatenate([ref[pl.ds(row0 * ROW_TILE + c, rows, stride=ROW_TILE), :] for c in range(ROW_TILE)],
                           axis=1)


def _route_body(tiles_per_seq, prompt_row0, x_ref, mod_ref, gpre_ref, rw_ref, rb_ref, h_ref, info_ref, cnt_ref,
                base_ref):
    D = D_MODEL

    @pl.when(pl.program_id(0) == 0)
    def _():
        base_ref[...] = jnp.zeros_like(base_ref)

    m = _mod_rows(mod_ref, False, tiles_per_seq, prompt_row0)
    h = _rms(x_ref[...], gpre_ref[...]) * (1.0 + m[:, D:2 * D]) + m[:, 0:D]
    _to_row_tiles(h_ref, h)
    lane, i1, i2, p1, p2 = _top2(_dot3(h, rw_ref[...]) + rb_ref[...])
    onehot = jnp.where((lane == i1) | (lane == i2), 1.0, 0.0)
    T = h.shape[0]
    before = lax.broadcasted_iota(jnp.int32, (T, T), 1) < lax.broadcasted_iota(jnp.int32, (T, T), 0)
    rank = _dot(jnp.where(before, 1.0, 0.0), onehot) + base_ref[0:1, :]
    pos1 = jnp.sum(jnp.where(lane == i1, rank, 0.0), axis=-1, keepdims=True)
    pos2 = jnp.sum(jnp.where(lane == i2, rank, 0.0), axis=-1, keepdims=True)
    info = jnp.zeros(onehot.shape, F32)
    for col, val in ((INFO_E1, i1.astype(F32)), (INFO_E2, i2.astype(F32)), (INFO_P1, p1), (INFO_P2, p2),
                     (INFO_POS1, pos1), (INFO_POS2, pos2)):
        info = jnp.where(lane == col, val, info)
    info_ref[...] = info
    total = base_ref[...] + jnp.sum(onehot, axis=0, keepdims=True)
    base_ref[...] = total
    cnt_ref[...] = total


def _moe_route(x, mod, gpre, rw, rb, tm, tiles_per_seq, prompt_row0):
    N, D = x.shape
    row = lambda i: (i, 0)
    fixed = lambda i: (0, 0)
    return pl.pallas_call(
        functools.partial(_route_body, tiles_per_seq, prompt_row0),
        out_shape=(jax.ShapeDtypeStruct((N * ROW_TILE, LANES), F32), jax.ShapeDtypeStruct((N, LANES), F32),
                   jax.ShapeDtypeStruct((SUBLANES, LANES), F32)),
        grid=(N // tm,),
        in_specs=[pl.BlockSpec((tm, D), row), _mod_spec(mod, False, tm, 1), pl.BlockSpec((1, D), fixed),
                  pl.BlockSpec((D, LANES), fixed), pl.BlockSpec((1, LANES), fixed)],
        out_specs=(pl.BlockSpec((tm * ROW_TILE, LANES), row), pl.BlockSpec((tm, LANES), row),
                   pl.BlockSpec((SUBLANES, LANES), fixed)),
        scratch_shapes=[pltpu.VMEM((SUBLANES, LANES), F32)],
        compiler_params=_params(("arbitrary",)),
        name="moe_route",
    )(x, mod, gpre, rw, rb)


def _row_copy(src_hbm, dst_vmem, sem, src_row, dst_row):
    return pltpu.make_async_copy(src_hbm.at[pl.ds(pl.multiple_of(src_row * ROW_TILE, ROW_TILE), ROW_TILE)],
                                 dst_vmem.at[pl.ds(pl.multiple_of(dst_row * ROW_TILE, ROW_TILE), ROW_TILE)], sem)


def _start_rows(idx_ref, idx0, src_hbm, dst_vmem, sem, rows):
    def body(r, carry):
        _row_copy(src_hbm, dst_vmem, sem, idx_ref[idx0 + r], r).start()
        return carry
    lax.fori_loop(0, rows, body, 0, unroll=16)


def _wait_rows(src_hbm, dst_vmem, sem, rows):
    pltpu.make_async_copy(src_hbm.at[pl.ds(0, rows * ROW_TILE)], dst_vmem.at[pl.ds(0, rows * ROW_TILE)], sem).wait()


def _experts_body(te_ref, nv_ref, src_ref, h_hbm, wg_ref, wu_ref, wd_ref, o_ref, xbuf, hb_ref, acc_ref, sem):
    TM = EXPERT_TILE
    i = pl.program_id(0)
    f = pl.program_id(1)
    slot = i % 2
    n_valid = nv_ref[0]
    valid = i < n_valid

    @pl.when(f == 0)
    def _():
        @pl.when((i == 0) & valid)
        def _():
            _start_rows(src_ref, 0, h_hbm, xbuf.at[0], sem.at[0], TM)

        @pl.when(i + 1 < n_valid)
        def _():
            _start_rows(src_ref, (i + 1) * TM, h_hbm, xbuf.at[1 - slot], sem.at[1 - slot], TM)

        @pl.when(valid)
        def _():
            _wait_rows(h_hbm, xbuf.at[slot], sem.at[slot], TM)
            hb_ref[...] = _from_row_tiles(xbuf.at[slot], 0, TM).astype(BF16)
            acc_ref[...] = jnp.zeros_like(acc_ref)

    @pl.when(valid)
    def _():
        h = hb_ref[...]
        gate = _dot(h, wg_ref[...])
        up = _dot(h, wu_ref[...])
        acc_ref[...] += _dot(gate * _sigmoid(gate) * up, wd_ref[...])

    @pl.when(f == pl.num_programs(1) - 1)
    def _():
        @pl.when(valid)
        def _():
            _to_row_tiles(o_ref, acc_ref[...])

        @pl.when(jnp.logical_not(valid))
        def _():
            o_ref[...] = jnp.zeros_like(o_ref)


def _moe_experts(h_rows, tile_expert, n_valid, src_tok, wg, wu, wd, tf):
    TM = EXPERT_TILE
    P = src_tok.shape[0]
    _, D, F = wg.shape
    grid_spec = pltpu.PrefetchScalarGridSpec(
        num_scalar_prefetch=3,
        grid=(P // TM, F // tf),
        in_specs=[pl.BlockSpec(memory_space=pl.ANY),
                  pl.BlockSpec((None, D, tf), lambda i, f, te, nv, src: (te[i], 0, f)),
                  pl.BlockSpec((None, D, tf), lambda i, f, te, nv, src: (te[i], 0, f)),
                  pl.BlockSpec((None, tf, D), lambda i, f, te, nv, src: (te[i], f, 0))],
        out_specs=pl.BlockSpec((TM * ROW_TILE, LANES), lambda i, f, te, nv, src: (i, 0)),
        scratch_shapes=[pltpu.VMEM((2, TM * ROW_TILE, LANES), F32), pltpu.VMEM((TM, D), BF16),
                        pltpu.VMEM((TM, D), F32), pltpu.SemaphoreType.DMA((2,))],
    )
    return pl.pallas_call(
        _experts_body,
        out_shape=jax.ShapeDtypeStruct((P * ROW_TILE, LANES), F32),
        grid_spec=grid_spec,
        compiler_params=_params(("arbitrary", "arbitrary")),
        name="moe_experts",
    )(tile_expert, n_valid, src_tok, h_rows, wg, wu, wd)


def _combine_body(tiles_per_seq, prompt_row0, dest_ref, f_hbm, x_ref, mod_ref, info_ref, gpost_ref, o_ref, gbuf, sem):
    D = D_MODEL
    i = pl.program_id(0)
    T = x_ref.shape[0]
    slot = i % 2

    @pl.when(i == 0)
    def _():
        _start_rows(dest_ref, 0, f_hbm, gbuf.at[0], sem.at[0], 2 * T)

    @pl.when(i + 1 < pl.num_programs(0))
    def _():
        _start_rows(dest_ref, (i + 1) * 2 * T, f_hbm, gbuf.at[1 - slot], sem.at[1 - slot], 2 * T)

    _wait_rows(f_hbm, gbuf.at[slot], sem.at[slot], 2 * T)
    info = info_ref[...]
    f1 = _from_row_tiles(gbuf.at[slot], 0, T)
    f2 = _from_row_tiles(gbuf.at[slot], T, T)
    y = info[:, INFO_P1:INFO_P1 + 1] * f1 + info[:, INFO_P2:INFO_P2 + 1] * f2
    m = _mod_rows(mod_ref, False, tiles_per_seq, prompt_row0)
    o_ref[...] = x_ref[...] + m[:, 2 * D:3 * D] * _rms(y, gpost_ref[...])


def _moe_combine(dest, f_rows, x, mod, info, gpost, tm, tiles_per_seq, prompt_row0):
    N, D = x.shape
    row = lambda i, d: (i, 0)
    fixed = lambda i, d: (0, 0)
    grid_spec = pltpu.PrefetchScalarGridSpec(
        num_scalar_prefetch=1,
        grid=(N // tm,),
        in_specs=[pl.BlockSpec(memory_space=pl.ANY), pl.BlockSpec((tm, D), row),
                  pl.BlockSpec(mod.shape, fixed), pl.BlockSpec((tm, LANES), row), pl.BlockSpec((1, D), fixed)],
        out_specs=pl.BlockSpec((tm, D), row),
        scratch_shapes=[pltpu.VMEM((2, 2 * tm * ROW_TILE, LANES), F32), pltpu.SemaphoreType.DMA((2,))],
    )
    return pl.pallas_call(
        functools.partial(_combine_body, tiles_per_seq, prompt_row0),
        out_shape=jax.ShapeDtypeStruct((N, D), F32),
        grid_spec=grid_spec,
        compiler_params=_params(("arbitrary",)),
        name="moe_combine",
    )(dest, f_rows, x, mod, info, gpost)


def _moe_prompt(x, mod, gpre, gpost, fw, tm, tiles_per_seq, prompt_row0, tf):
    N = x.shape[0]
    TM = EXPERT_TILE
    n_tiles = (2 * N) // TM + N_EXPERTS
    h_rows, info, cnt = _moe_route(x, mod, gpre, fw["rw"], fw["rb"], tm, tiles_per_seq, prompt_row0)
    e1 = info[:, INFO_E1].astype(jnp.int32)
    e2 = info[:, INFO_E2].astype(jnp.int32)
    counts = cnt[0, :N_EXPERTS].astype(jnp.int32)
    padded = ((counts + TM - 1) // TM) * TM
    ends = jnp.cumsum(padded)
    starts = ends - padded
    dest1 = starts[e1] + info[:, INFO_POS1].astype(jnp.int32)
    dest2 = starts[e2] + info[:, INFO_POS2].astype(jnp.int32)
    tok = jnp.arange(N, dtype=jnp.int32)
    src_tok = jnp.zeros((n_tiles * TM,), jnp.int32).at[jnp.concatenate([dest1, dest2])].set(
        jnp.concatenate([tok, tok]), unique_indices=True)
    tile_start = jnp.arange(n_tiles, dtype=jnp.int32) * TM
    tile_expert = jnp.minimum(jnp.sum(ends[None, :] <= tile_start[:, None], axis=1), N_EXPERTS - 1).astype(jnp.int32)
    n_valid = (ends[-1:] // TM).astype(jnp.int32)
    f_rows = _moe_experts(h_rows, tile_expert + fw["w0"], n_valid, src_tok, fw["wg"], fw["wu"], fw["wd"], tf)
    dest = jnp.concatenate([dest1.reshape(-1, 1, tm), dest2.reshape(-1, 1, tm)], axis=1).reshape(-1)
    return _moe_combine(dest, f_rows, x, mod, info, gpost, tm, tiles_per_seq, prompt_row0)


def _layer_weights(p, l):
    W = RWKV_WIDTH
    heads = np.arange(W) // HEAD_DIM
    headsum = jnp.asarray((heads[:, None] == heads[None, :]).astype(np.float32))
    perm = _q_perm()
    w_in = p["w_in"][l]
    w_in = jnp.concatenate([w_in[:, :RWKV_COLS], w_in[:, RWKV_COLS + perm], w_in[:, RWKV_COLS + SWA_WIDTH:]], axis=1)
    w_out = p["w_out"][l]
    zeros = jnp.zeros((DECAY_LORA, W), F32)
    row = lambda t: t.reshape(1, -1)
    return {
        "w_in": w_in.astype(BF16),
        "w_out_r": w_out[:W].astype(BF16),
        "w_out_a": w_out[W + perm].astype(BF16),
        "mu": row(p["mu_shift"][l]),
        "wd": jnp.concatenate([p["w_decay_up"][l], zeros], axis=0),
        "wi": jnp.concatenate([zeros, p["w_iclr_up"][l]], axis=0),
        "wg": p["w_gate_up"][l],
        "dbase": row(p["decay_base"][l]), "ibase": row(p["iclr_base"][l]),
        "kk": row(p["k_k"][l]), "ka": row(p["k_a"][l]), "rk": row(p["r_k"][l]),
        "lnw": row(p["lnx_w"][l]), "lnb": row(p["lnx_b"][l]),
        "headsum": headsum.astype(BF16), "headmean": (headsum / HEAD_DIM).astype(BF16),
        "g_pre0": row(p["norm_pre"][l, 0]), "g_pre1": row(p["norm_pre"][l, 1]),
        "g_post0": row(p["norm_post"][l, 0]), "g_post1": row(p["norm_post"][l, 1]),
    }


def _stacked_bf16(w):
    return w.astype(BF16).reshape((-1,) + w.shape[-2:])


def _ffn_weights(p, stacks, l):
    i = l // 2
    if l % 2 == 0:
        wg, wu, wd = stacks["dense"]
        return dict(moe=False, rw=jnp.zeros((D_MODEL, LANES), F32), rb=jnp.zeros((1, LANES), F32),
                    wg=wg, wu=wu, wd=wd, w0=i, n=1)
    rw = jnp.zeros((D_MODEL, LANES), F32).at[:, :N_EXPERTS].set(p["router_w"][i])
    rb = jnp.full((1, LANES), NEG_INF, F32).at[0, :N_EXPERTS].set(p["router_b"][i])
    wg, wu, wd = stacks["moe"]
    return dict(moe=True, rw=rw, rb=rb, wg=wg, wu=wu, wd=wd, w0=i * N_EXPERTS, n=N_EXPERTS)


def _pick_tile(n, pref):
    t = min(pref, n)
    while n % t:
        t //= 2
    return t


def _ffn_tile(f):
    for t in (1408, 896, 512, 256, 128):
        if f % t == 0:
            return t
    return f


def _trunk(x3, mods, lws, fws, p, prompt, prompt_row0, state=None):
    B, T, D = x3.shape
    N = B * T
    x = x3.reshape(N, D)
    per_row = not prompt
    tm = _pick_tile(T if prompt else N, 512)
    tps = (T // tm) if prompt else 1
    depth = len(lws)
    wkv_out, shift_out, k_out, v_out = [], [], [], []
    for l in range(depth):
        lw, fw = lws[l], fws[l]
        mod0, mod1 = mods[2 * l], mods[2 * l + 1]
        pr, q, k, v = _inproj(x, mod0, lw["g_pre0"], lw["w_in"], per_row, tm, tps, prompt_row0)
        prev = pr if prompt else state["shift"][l]
        r, ld, kh, vv, a, b, g, bonus = _prep(pr, prev, lw, per_row, tm, tps)
        if prompt:
            yw, hbd = _wkv_prompt(r, ld, kh, vv, a, b, B, T)
            n_pairs = RWKV_WIDTH // PAIR
            hb = hbd.reshape(B, n_pairs, 2, HEAD_DIM, 2, HEAD_DIM)
            s_kv = jnp.stack([hb[:, :, 0, :, 0, :], hb[:, :, 1, :, 1, :]], axis=2)
            s_new = jnp.swapaxes(s_kv.reshape(B, RWKV_HEADS, HEAD_DIM, HEAD_DIM), -1, -2)
            ya = _swa_prompt(q, k, v, p["rel_bias"], p["attn_sinks"][l], B, T)
            window = lambda t: t.reshape(B, T, KV_WIDTH)[:, -WINDOW:].reshape(B, WINDOW, SWA_KV_HEADS, HEAD_DIM)
            kb, vb = window(k), window(v)
            last = pr.reshape(B, T, RWKV_COLS)[:, -1]
        else:
            yw, s_new = _wkv_sample(state["wkv"][l], r, ld, kh, vv, a, b)
            ya, kb, vb = _swa_sample(q, k, v, state["k"][l], state["v"][l], p["rel_bias"], p["attn_sinks"][l])
            last = pr
        x = _outproj(x, mod0, yw, g, bonus, ya, lw, per_row, tm, tps, prompt_row0)
        tf = _ffn_tile(fw["wg"].shape[-1])
        if prompt and fw["moe"]:
            x = _moe_prompt(x, mod1, lw["g_pre1"], lw["g_post1"], fw, tm, tps, prompt_row0, tf)
        else:
            x = _ffn(x, mod1, lw["g_pre1"], lw["g_post1"], fw["rw"], fw["rb"], fw["wg"], fw["wu"], fw["wd"],
                     fw["w0"], fw["n"], fw["moe"], per_row, tm, tps, prompt_row0, tf)
        wkv_out.append(s_new)
        shift_out.append(last)
        k_out.append(kb)
        v_out.append(vb)
    return x.reshape(B, T, D), jnp.stack(wkv_out), jnp.stack(shift_out), jnp.stack(k_out), jnp.stack(v_out)


def _forward(x_prompt, x_sample, c_prompt, c_sample, state_wkv, state_shift, cache_swa_k, cache_swa_v, p):
    depth = p["w_in"].shape[0]
    Bp, Bs = c_prompt.shape[0], c_sample.shape[0]
    D = D_MODEL
    pad = (-(Bs + Bp)) % SUBLANES
    c_all = jnp.concatenate([c_sample, c_prompt, jnp.zeros((pad, D), F32)], axis=0)
    mods = _ada_all(c_all, p["ada_w"].reshape(2 * depth, D, 3 * D), p["ada_b"].reshape(2 * depth, 1, 3 * D))
    lws = [_layer_weights(p, l) for l in range(depth)]
    stacks = {"dense": tuple(_stacked_bf16(p[k]) for k in ("ffn_w_gate", "ffn_w_up", "ffn_w_down")),
              "moe": tuple(_stacked_bf16(p[k]) for k in ("moe_w_gate", "moe_w_up", "moe_w_down"))}
    fws = [_ffn_weights(p, stacks, l) for l in range(depth)]
    y_p, wkv_p, shift_p, k_p, v_p = _trunk(x_prompt, mods, lws, fws, p, True, Bs)
    state = {"wkv": state_wkv, "shift": state_shift, "k": cache_swa_k, "v": cache_swa_v}
    y_s, wkv_s, shift_s, k_s, v_s = _trunk(x_sample, mods, lws, fws, p, False, Bs, state)
    return (y_p, y_s, wkv_p, shift_p, k_p, v_p, wkv_s, shift_s, k_s, v_s)


def kernel(x_prompt, x_sample, c_prompt, c_sample, state_wkv, state_shift, cache_swa_k, cache_swa_v, rel_bias, ada_w, ada_b, norm_pre, norm_post, w_in, mu_shift, w_decay_up, decay_base, w_iclr_up, iclr_base, w_gate_up, k_k, k_a, r_k, lnx_w, lnx_b, attn_sinks, w_out, ffn_w_gate, ffn_w_up, ffn_w_down, router_w, router_b, moe_w_gate, moe_w_up, moe_w_down):
    p = {"rel_bias": rel_bias, "ada_w": ada_w, "ada_b": ada_b, "norm_pre": norm_pre, "norm_post": norm_post,
         "w_in": w_in, "mu_shift": mu_shift, "w_decay_up": w_decay_up, "decay_base": decay_base,
         "w_iclr_up": w_iclr_up, "iclr_base": iclr_base, "w_gate_up": w_gate_up, "k_k": k_k, "k_a": k_a,
         "r_k": r_k.reshape(r_k.shape[0], -1), "lnx_w": lnx_w, "lnx_b": lnx_b, "attn_sinks": attn_sinks,
         "w_out": w_out, "ffn_w_gate": ffn_w_gate, "ffn_w_up": ffn_w_up, "ffn_w_down": ffn_w_down,
         "router_w": router_w, "router_b": router_b, "moe_w_gate": moe_w_gate, "moe_w_up": moe_w_up,
         "moe_w_down": moe_w_down}
    return _forward(x_prompt, x_sample, c_prompt, c_sample, state_wkv, state_shift, cache_swa_k, cache_swa_v, p)
```

```python
import functools

import numpy as np
import jax
import jax.numpy as jnp
from jax import lax
from jax.experimental import pallas as pl
from jax.experimental.pallas import tpu as pltpu

F32 = jnp.float32
BF16 = jnp.bfloat16

D_MODEL = 1024
HEAD_DIM = 64
RWKV_WIDTH = 512
RWKV_HEADS = RWKV_WIDTH // HEAD_DIM
SWA_WIDTH = 512
SWA_HEADS = SWA_WIDTH // HEAD_DIM
SWA_KV_HEADS = 2
SWA_GROUP = SWA_HEADS // SWA_KV_HEADS
KV_WIDTH = SWA_KV_HEADS * HEAD_DIM
WINDOW = 128
DECAY_LORA = 64
ICLR_LORA = 64
GATE_LORA = 128
RWKV_COLS = 3 * RWKV_WIDTH + DECAY_LORA + ICLR_LORA + GATE_LORA
IN_COLS = RWKV_COLS + SWA_WIDTH + 2 * KV_WIDTH
LORA_OFF = 3 * RWKV_WIDTH
GATE_OFF = LORA_OFF + DECAY_LORA + ICLR_LORA
LN_X_EPS = 64e-5
RMS_EPS = 1e-6
N_BUCKETS = 32
MAX_DISTANCE = 128
N_EXPERTS = 8
NEG_INF = -1e30

LANES = 128
SUBLANES = 8
VMEM_LIMIT = 56 * 1024 * 1024

WKV_CHUNK = 64
PAIR = 2 * HEAD_DIM

NN = (((1,), (0,)), ((), ()))
NT = (((1,), (1,)), ((), ()))
TN = (((0,), (0,)), ((), ()))


def _dot(a, b, dims=NN):
    return lax.dot_general(a.astype(BF16), b.astype(BF16), dims, preferred_element_type=F32)


def _split(x, pieces):
    out = []
    for _ in range(pieces - 1):
        hi = x.astype(BF16)
        out.append(hi)
        x = x - hi.astype(F32)
    out.append(x.astype(BF16))
    return out


def _select_dot(x, sel, pieces=3, sel_left=False):
    d = lambda t: lax.dot_general(*((sel, t) if sel_left else (t, sel)), NN, preferred_element_type=F32)
    return sum(d(t) for t in _split(x, pieces))


def _dot3(a, b):
    a_hi, a_lo = _split(a, 2)
    b_hi, b_lo = _split(b, 2)
    d = lambda x, y: lax.dot_general(x, y, NN, preferred_element_type=F32)
    return d(a_hi, b_hi) + d(a_lo, b_hi) + d(a_hi, b_lo)


def _sigmoid(x):
    return 1.0 / (1.0 + jnp.exp(-x))


def _params(sem):
    return pltpu.CompilerParams(dimension_semantics=sem, vmem_limit_bytes=VMEM_LIMIT)


def _rms(x, g):
    return x * lax.rsqrt(jnp.mean(x * x, axis=-1, keepdims=True) + RMS_EPS) * g


def _mod_rows(mod_ref, per_row, tiles_per_seq, prompt_row0):
    if per_row:
        return mod_ref[...]
    b = pl.program_id(0) // tiles_per_seq
    return mod_ref[pl.ds(prompt_row0 + b, 1), :]


def _ada_body(c_ref, w_ref, b_ref, o_ref):
    c = c_ref[...]
    o_ref[...] = _dot(c * _sigmoid(c), w_ref[...]) + b_ref[...]


def _ada_all(c_all, ada_w, ada_b):
    R, D = c_all.shape
    n = ada_w.shape[0]
    tn = 1024
    return pl.pallas_call(
        _ada_body,
        out_shape=jax.ShapeDtypeStruct((n, R, 3 * D), F32),
        grid=(n, 3 * D // tn),
        in_specs=[pl.BlockSpec((R, D), lambda i, j: (0, 0)),
                  pl.BlockSpec((None, D, tn), lambda i, j: (i, 0, j)),
                  pl.BlockSpec((None, 1, tn), lambda i, j: (i, 0, j))],
        out_specs=pl.BlockSpec((None, R, tn), lambda i, j: (i, 0, j)),
        compiler_params=_params(("arbitrary", "arbitrary")),
        name="ada_mod",
    )(c_all, ada_w, ada_b)


def _inproj_body(per_row, tiles_per_seq, prompt_row0, x_ref, mod_ref, g_ref, w_ref, pr_ref, q_ref, k_ref, v_ref):
    D = D_MODEL
    m = _mod_rows(mod_ref, per_row, tiles_per_seq, prompt_row0)
    h = _rms(x_ref[...], g_ref[...]) * (1.0 + m[:, D:2 * D]) + m[:, 0:D]
    proj = _dot(h, w_ref[...])
    pr_ref[...] = proj[:, 0:RWKV_COLS]
    q_ref[...] = proj[:, RWKV_COLS:RWKV_COLS + SWA_WIDTH]
    k_ref[...] = proj[:, RWKV_COLS + SWA_WIDTH:RWKV_COLS + SWA_WIDTH + KV_WIDTH]
    v_ref[...] = proj[:, RWKV_COLS + SWA_WIDTH + KV_WIDTH:IN_COLS]


def _mod_spec(mod, per_row, tm, nargs):
    R = mod.shape[0]
    if per_row:
        return pl.BlockSpec((tm, 3 * D_MODEL), lambda i, *_: (0, 0))
    return pl.BlockSpec((R, 3 * D_MODEL), lambda i, *_: (0, 0))


def _inproj(x, mod, g, w, per_row, tm, tiles_per_seq, prompt_row0):
    N, D = x.shape
    row = lambda i: (i, 0)
    fixed = lambda i: (0, 0)
    return pl.pallas_call(
        functools.partial(_inproj_body, per_row, tiles_per_seq, prompt_row0),
        out_shape=(jax.ShapeDtypeStruct((N, RWKV_COLS), F32), jax.ShapeDtypeStruct((N, SWA_WIDTH), F32),
                   jax.ShapeDtypeStruct((N, KV_WIDTH), F32), jax.ShapeDtypeStruct((N, KV_WIDTH), F32)),
        grid=(N // tm,),
        in_specs=[pl.BlockSpec((tm, D), row), _mod_spec(mod, per_row, tm, 1),
                  pl.BlockSpec((1, D), fixed), pl.BlockSpec((D, IN_COLS), fixed)],
        out_specs=(pl.BlockSpec((tm, RWKV_COLS), row), pl.BlockSpec((tm, SWA_WIDTH), row),
                   pl.BlockSpec((tm, KV_WIDTH), row), pl.BlockSpec((tm, KV_WIDTH), row)),
        compiler_params=_params(("arbitrary",)),
        name="in_proj",
    )(x, mod, g, w)


PREP_KEYS = ("mu", "wd", "wi", "wg", "dbase", "ibase", "kk", "ka", "rk", "headsum")
POST_KEYS = ("lnw", "lnb", "headmean")


def _prep_specs(fixed):
    W = RWKV_WIDTH
    vec = pl.BlockSpec((1, W), fixed)
    return [pl.BlockSpec((1, RWKV_COLS), fixed), pl.BlockSpec((LANES, W), fixed), pl.BlockSpec((LANES, W), fixed),
            pl.BlockSpec((GATE_LORA, W), fixed), vec, vec, vec, vec, vec, pl.BlockSpec((W, W), fixed)]


def _rwkv_features(pr, shifted, mu_ref, wd_ref, wi_ref, wg_ref, dbase_ref, ibase_ref, kk_ref, ka_ref, rk_ref, hs_ref):
    W = RWKV_WIDTH
    xs = pr + (shifted - pr) * mu_ref[...]
    r = xs[:, 0:W]
    k = xs[:, W:2 * W]
    v = xs[:, 2 * W:3 * W]
    lora = xs[:, LORA_OFF:GATE_OFF]
    gl = xs[:, GATE_OFF:RWKV_COLS]
    z = dbase_ref[...] + _dot3(jnp.tanh(lora), wd_ref[...])
    ld = -float(np.exp(-0.5)) * _sigmoid(z)
    iclr = _sigmoid(ibase_ref[...] + _dot3(lora, wi_ref[...]))
    g = _dot3(_sigmoid(gl), wg_ref[...])
    hs = hs_ref[...]
    kk = k * kk_ref[...]
    kk = kk / jnp.maximum(jnp.sqrt(_select_dot(kk * kk, hs, 2)), 1e-12)
    kh = k * (1.0 + (iclr - 1.0) * ka_ref[...])
    bonus = _select_dot(r * kh * rk_ref[...], hs, 2) * v
    return r, ld, kh, v, -kk, kk * iclr, g, bonus


def _rwkv_post(y, g, bonus, lnw, lnb, hm):
    mean = _select_dot(y, hm, 2)
    dv = y - mean
    var = _select_dot(dv * dv, hm, 2)
    return (dv * lax.rsqrt(var + LN_X_EPS) * lnw + lnb + bonus) * g


def _prep_body(pr_ref, prev_ref, *refs):
    outs = refs[len(PREP_KEYS):]
    for ref, val in zip(outs, _rwkv_features(pr_ref[...], prev_ref[...], *refs[:len(PREP_KEYS)])):
        ref[...] = val


def _prep_sample(pr, prev, lw):
    N = pr.shape[0]
    W = RWKV_WIDTH
    row = lambda i: (i, 0)
    fixed = lambda i: (0, 0)
    out = jax.ShapeDtypeStruct((N, W), F32)
    return pl.pallas_call(
        _prep_body,
        out_shape=(out,) * 8,
        grid=(1,),
        in_specs=[pl.BlockSpec((N, RWKV_COLS), row), pl.BlockSpec((N, RWKV_COLS), row)] + _prep_specs(fixed),
        out_specs=(pl.BlockSpec((N, W), row),) * 8,
        compiler_params=_params(("arbitrary",)),
        name="rwkv_prep",
    )(pr, prev, *[lw[k] for k in PREP_KEYS])


def _stack_heads(x, lane_head0):
    return jnp.concatenate([jnp.where(lane_head0, x, 0.0), jnp.where(lane_head0, 0.0, x)], axis=0)


def _fold_heads(x):
    c = x.shape[0] // 2
    return x[0:c] + x[c:2 * c]


def _dots(xs, ys, dims=NN):
    return [_dot(x, y, dims) for x, y in zip(xs, ys)]


def _unit_lower_inverse(ns, same16, eye, between=lambda: None):
    nd = [jnp.where(same16, n, 0.0) for n in ns]
    no = [n - d for n, d in zip(ns, nd)]
    n2 = _dots(nd, nd)
    between()
    n4 = _dots(n2, n2)
    n8 = _dots(n4, n4)
    between()
    td = [eye + d for d in nd]
    for pw in (n2, n4, n8):
        td = [t + u for t, u in zip(td, _dots(td, pw))]
    between()
    q = _dots(td, no)
    q2 = _dots(q, q)
    between()
    z = [eye + x for x in q]
    z = [t + u for t, u in zip(z, _dots(z, q2))]
    return _dots(z, td)


def _wkv_chunk_body(chunks, pr_ref, prev_ref, *refs):
    C = WKV_CHUNK
    n_pairs = RWKV_WIDTH // PAIR
    prep_refs = refs[:len(PREP_KEYS)]
    lnw_ref, lnb_ref, hm_ref, yr_ref, s_ref, h_ref, y_ref = refs[len(PREP_KEYS):len(PREP_KEYS) + 7]
    feat_refs = refs[len(PREP_KEYS) + 7:]

    @pl.when(pl.program_id(1) == 0)
    def _():
        h_ref[...] = jnp.zeros_like(h_ref)
        for ref in feat_refs:
            ref[...] = jnp.zeros_like(ref)

    r_all, ld_all, k_all, v_all_, a_all, b_all, g_all, bonus_all = [ref[...] for ref in feat_refs]

    ri = lax.broadcasted_iota(jnp.int32, (PAIR, PAIR), 0)
    ci = lax.broadcasted_iota(jnp.int32, (PAIR, PAIR), 1)
    same_head = (ri // C) == (ci // C)
    strict_lower = same_head & (ci < ri)
    incl_lower = same_head & (ci <= ri)
    same16 = (ri // 16) == (ci // 16)
    eye_b = ri == ci
    eye = jnp.where(eye_b, 1.0, 0.0)
    tri = jnp.where(lax.broadcasted_iota(jnp.int32, (C, C), 1) <= lax.broadcasted_iota(jnp.int32, (C, C), 0),
                    1.0, 0.0).astype(BF16)
    lane_head0 = lax.broadcasted_iota(jnp.int32, (C, PAIR), 1) < HEAD_DIM
    zeros = jnp.zeros((PAIR, PAIR), F32)

    cat0 = lambda x, y: jnp.concatenate([x, y], axis=0)
    cat1 = lambda x, y: jnp.concatenate([x, y], axis=1)

    xa, xr, v_st, bh_st, kh_st, yb, yk, p_all = [], [], [], [], [], [], [], []
    for c in range(chunks):
        rows = slice(c * C, (c + 1) * C)
        ld = ld_all[rows, :]
        cum = _select_dot(ld, tri, 3, sel_left=True)
        last = cum[C - 1:C, :]
        p_inv = jnp.exp(-cum)
        p_tail = jnp.exp(last - cum)
        p_end = jnp.exp(last)
        a_t = a_all[rows, :] * jnp.exp(cum - ld)
        r_t = r_all[rows, :] * jnp.exp(cum)
        b_raw = b_all[rows, :]
        k_raw = k_all[rows, :]
        b_t = b_raw * p_inv
        k_t = k_raw * p_inv
        b_h = b_raw * p_tail
        k_h = k_raw * p_tail
        v_all = v_all_[rows, :]
        for j in range(n_pairs):
            lanes = slice(j * PAIR, (j + 1) * PAIR)
            xa.append(_stack_heads(a_t[:, lanes], lane_head0))
            xr.append(_stack_heads(r_t[:, lanes], lane_head0))
            v_st.append(_stack_heads(v_all[:, lanes], lane_head0))
            bh_st.append(_stack_heads(b_h[:, lanes], lane_head0))
            kh_st.append(_stack_heads(k_h[:, lanes], lane_head0))
            yb.append(cat0(b_t[:, lanes], b_t[:, lanes]))
            yk.append(cat0(k_t[:, lanes], k_t[:, lanes]))
            p_all.append(p_end[:, lanes])

    gram = _dots([cat0(x, y) for x, y in zip(xa, xr)], [cat0(x, y) for x, y in zip(yb, yk)], NT)
    n_mat = [jnp.where(strict_lower, g[0:PAIR, 0:PAIR], 0.0) for g in gram]
    m_mat = [jnp.where(strict_lower, g[0:PAIR, PAIR:2 * PAIR], 0.0) for g in gram]
    a_rbk = [cat1(jnp.where(incl_lower, g[PAIR:2 * PAIR, 0:PAIR], 0.0),
                  jnp.where(incl_lower, g[PAIR:2 * PAIR, PAIR:2 * PAIR], 0.0)) for g in gram]
    n_pieces = 1
    rows_per = chunks * C // n_pieces
    pieces = list(range(n_pieces))

    def prepare_piece():
        if not pieces:
            return
        c = pieces.pop(0)
        rows = slice(c * rows_per, (c + 1) * rows_per)
        pr = pr_ref[rows, :]
        if c == 0:
            carry = jnp.where(pl.program_id(1) == 0, 0.0, prev_ref[SUBLANES - 1:SUBLANES, :])
            shifted = jnp.where(lax.broadcasted_iota(jnp.int32, pr.shape, 0) == 0, carry, pltpu.roll(pr, 1, 0))
        else:
            shifted = pr_ref[c * rows_per - 1:(c + 1) * rows_per - 1, :]
        for ref, val in zip(feat_refs, _rwkv_features(pr, shifted, *prep_refs)):
            ref[rows, :] = val

    t_inv = _unit_lower_inverse(n_mat, same16, eye, prepare_piece)
    while pieces:
        prepare_piece()
    mv = _dots(m_mat, v_st)
    tx = _dots(t_inv, [cat1(x, y) for x, y in zip(xa, mv)])
    rhs = [cat0(t, cat1(zeros, v)) for t, v in zip(tx, v_st)]
    ry = _dots(a_rbk, rhs)
    pp = _dots([cat0(x, y) for x, y in zip(bh_st, kh_st)], rhs, TN)

    for c in range(chunks):
        us = [c * n_pairs + j for j in range(n_pairs)]
        h0 = [h_ref[j] for j in range(n_pairs)]
        r_bar = [_fold_heads(xr[u] + ry[u][:, 0:PAIR]) for u in us]
        phi = [pp[u][:, 0:PAIR] + jnp.where(eye_b, p_all[u], 0.0) for u in us]
        ys = _dots(r_bar, h0)
        hs = _dots(phi, h0)
        for j, u in enumerate(us):
            y_ref[c * C:(c + 1) * C, j * PAIR:(j + 1) * PAIR] = ys[j] + _fold_heads(ry[u][:, PAIR:2 * PAIR])
            h_ref[j] = hs[j] + pp[u][:, PAIR:2 * PAIR]

    yr_ref[...] = _rwkv_post(y_ref[...], g_all, bonus_all, lnw_ref[...], lnb_ref[...], hm_ref[...])

    @pl.when(pl.program_id(1) == pl.num_programs(1) - 1)
    def _():
        s_ref[...] = h_ref[...]


def _wkv_prompt(pr, lw, batch, seq):
    N = pr.shape[0]
    W = RWKV_WIDTH
    chunks = 4
    tt = chunks * WKV_CHUNK
    steps = seq // tt
    n_pairs = W // PAIR
    prep_blk = lambda bb, s: bb * steps + jnp.minimum(s, steps - 1)
    row_in = lambda bb, s: (prep_blk(bb, s), 0)
    row_out = lambda bb, s: (bb * steps + jnp.maximum(s - 1, 0), 0)
    fixed = lambda bb, s: (0, 0)
    per = tt // SUBLANES
    prev = lambda bb, s: (jnp.maximum(prep_blk(bb, s) * per - 1, 0), 0)
    vec = pl.BlockSpec((1, W), fixed)
    return pl.pallas_call(
        functools.partial(_wkv_chunk_body, chunks),
        out_shape=(jax.ShapeDtypeStruct((N, W), F32), jax.ShapeDtypeStruct((batch, n_pairs, PAIR, PAIR), F32)),
        grid=(batch, steps + 1),
        in_specs=[pl.BlockSpec((tt, RWKV_COLS), row_in), pl.BlockSpec((SUBLANES, RWKV_COLS), prev)]
                 + _prep_specs(fixed) + [vec, vec, pl.BlockSpec((W, W), fixed)],
        out_specs=(pl.BlockSpec((tt, W), row_out),
                   pl.BlockSpec((None, n_pairs, PAIR, PAIR), lambda bb, s: (bb, 0, 0, 0))),
        scratch_shapes=[pltpu.VMEM((n_pairs, PAIR, PAIR), F32), pltpu.VMEM((tt, W), F32)]
                       + [pltpu.VMEM((tt, W), F32)] * 8,
        compiler_params=_params(("arbitrary", "arbitrary")),
        name="wkv_chunk_scan",
    )(pr, pr, *[lw[k] for k in PREP_KEYS + POST_KEYS])


def _wkv_step_body(s_ref, r_ref, ld_ref, k_ref, v_ref, a_ref, b_ref, g_ref, bonus_ref, lnw_ref, lnb_ref, hm_ref,
                   exp_ref, red_ref, y_ref, so_ref):
    HD = HEAD_DIM
    nrep = HD * HD // LANES
    lane = lax.broadcasted_iota(jnp.int32, r_ref.shape, 1)
    low = lane < HD
    y = jnp.zeros(r_ref.shape, F32)
    for hh in range(2):
        def tiled(ref, fn=None):
            x = ref[...]
            if fn is not None:
                x = fn(x)
            sw = pltpu.roll(x, HD, 1)
            both = jnp.where(low, x, sw) if hh == 0 else jnp.where(low, sw, x)
            return jnp.tile(both, (1, nrep))
        cols = slice(hh * HD * HD, (hh + 1) * HD * HD)
        s = s_ref[:, cols]
        expand = exp_ref[hh]
        reduce_ = red_ref[hh]
        sa = _select_dot(s * tiled(a_ref), reduce_)
        s_new = (s * tiled(ld_ref, jnp.exp) + _select_dot(sa, expand) * tiled(b_ref)
                 + _select_dot(v_ref[...], expand) * tiled(k_ref))
        so_ref[:, cols] = s_new
        y = y + _select_dot(s_new * tiled(r_ref), reduce_)
    y_ref[...] = _rwkv_post(y, g_ref[...], bonus_ref[...], lnw_ref[...], lnb_ref[...], hm_ref[...])


def _wkv_step_consts():
    HD = HEAD_DIM
    expand = np.zeros((2, PAIR, HD * HD), np.float32)
    for hh in range(2):
        for vv in range(HD):
            expand[hh, hh * HD + vv, vv * HD:(vv + 1) * HD] = 1.0
    return jnp.asarray(expand, BF16), jnp.asarray(expand.transpose(0, 2, 1), BF16)


def _wkv_sample(state, r, ld, k, v, a, b, g, bonus, lw):
    B = state.shape[0]
    W = RWKV_WIDTH
    HD2 = HEAD_DIM * HEAD_DIM
    expand, reduce_ = _wkv_step_consts()
    s2 = state.reshape(B, RWKV_HEADS * HD2)
    st_spec = pl.BlockSpec((B, 2 * HD2), lambda j: (0, j))
    vec = pl.BlockSpec((B, PAIR), lambda j: (0, j))
    y, s_new = pl.pallas_call(
        _wkv_step_body,
        out_shape=(jax.ShapeDtypeStruct((B, W), F32), jax.ShapeDtypeStruct((B, RWKV_HEADS * HD2), F32)),
        grid=(W // PAIR,),
        in_specs=[st_spec] + [vec] * 8 + [pl.BlockSpec((1, PAIR), lambda j: (0, j))] * 2
                 + [pl.BlockSpec((PAIR, PAIR), lambda j: (j, j)), pl.BlockSpec((2, PAIR, HD2), lambda j: (0, 0, 0)),
                    pl.BlockSpec((2, HD2, PAIR), lambda j: (0, 0, 0))],
        out_specs=(vec, st_spec),
        compiler_params=_params(("arbitrary",)),
        name="wkv_step",
    )(s2, r, ld, k, v, a, b, g, bonus, lw["lnw"], lw["lnb"], lw["headmean"], expand, reduce_)
    return y, s_new.reshape(state.shape)


def _q_perm():
    return np.array([(h * SWA_GROUP + g) * HEAD_DIM + d for g in range(SWA_GROUP) for h in range(SWA_KV_HEADS)
                     for d in range(HEAD_DIM)], np.int32)


def _rel_bucket_np(dist):
    max_exact = N_BUCKETS // 2
    d = np.maximum(dist, 0)
    ratio = np.log(np.maximum(d, 1).astype(np.float32) / np.float32(max_exact)) / np.float32(
        np.log(MAX_DISTANCE / max_exact))
    large = np.minimum(max_exact + (ratio.astype(np.float32) * np.float32(N_BUCKETS - max_exact)).astype(np.int32),
                       N_BUCKETS - 1)
    return np.where(d < max_exact, d, large).astype(np.int32)


def _bias_from_buckets(idx, rb_ref, head):
    acc = jnp.zeros(idx.shape, F32)
    for bk in range(N_BUCKETS):
        acc = jnp.where(idx == bk, rb_ref[bk, head], acc)
    return acc


def _swa_prompt_body(rb_ref, sink_ref, idx_ref, q_ref, kp_ref, kc_ref, vp_ref, vc_ref, o_ref, bias_ref):
    Q = WINDOW
    first = (pl.program_id(0) == 0) & (pl.program_id(1) == 0)

    @pl.when(first)
    def _():
        idx = idx_ref[...]
        for g in range(SWA_GROUP):
            for h in range(SWA_KV_HEADS):
                bias_ref[g * SWA_KV_HEADS + h] = _bias_from_buckets(idx, rb_ref, h * SWA_GROUP + g)

    n = pl.program_id(1)
    qi = lax.broadcasted_iota(jnp.int32, (Q, 2 * Q), 0)
    kj = lax.broadcasted_iota(jnp.int32, (Q, 2 * Q), 1)
    valid = ((kj < Q) & (kj >= qi) & (n > 0)) | ((kj >= Q) & ((kj - Q) <= qi))
    lane_kv0 = lax.broadcasted_iota(jnp.int32, (Q, LANES), 1) < HEAD_DIM
    kcat = jnp.concatenate([kp_ref[...], kc_ref[...]], axis=0).astype(BF16)
    vcat = jnp.concatenate([vp_ref[...], vc_ref[...]], axis=0).astype(BF16)
    scale = HEAD_DIM ** -0.5
    for g in range(SWA_GROUP):
        qg = q_ref[:, g * LANES:(g + 1) * LANES]
        outs = []
        for h in range(SWA_KV_HEADS):
            qm = jnp.where(lane_kv0, qg, 0.0) if h == 0 else jnp.where(lane_kv0, 0.0, qg)
            s = _dot(qm, kcat, NT) * scale + bias_ref[g * SWA_KV_HEADS + h]
            s = jnp.where(valid, s, NEG_INF)
            sink = sink_ref[h * SWA_GROUP + g]
            m = jnp.maximum(jnp.max(s, axis=-1, keepdims=True), sink)
            p = jnp.exp(s - m)
            den = jnp.sum(p, axis=-1, keepdims=True) + jnp.exp(sink - m)
            outs.append(_dot(p, vcat) / den)
        o_ref[:, g * LANES:(g + 1) * LANES] = jnp.where(lane_kv0, outs[0], outs[1])


def _swa_prompt(q, k, v, rel_bias, sinks, batch, seq):
    N = q.shape[0]
    Q = WINDOW
    nb = seq // Q
    qi = np.arange(Q)[:, None]
    kj = np.arange(2 * Q)[None, :]
    idx = jnp.asarray(_rel_bucket_np(qi + Q - kj))
    cur = lambda bb, n: (bb * nb + n, 0)
    prev = lambda bb, n: (bb * nb + jnp.maximum(n - 1, 0), 0)
    kv_c = pl.BlockSpec((Q, KV_WIDTH), cur)
    kv_p = pl.BlockSpec((Q, KV_WIDTH), prev)
    smem = pl.BlockSpec(memory_space=pltpu.SMEM)
    return pl.pallas_call(
        _swa_prompt_body,
        out_shape=jax.ShapeDtypeStruct((N, SWA_WIDTH), F32),
        grid=(batch, nb),
        in_specs=[smem, smem, pl.BlockSpec((Q, 2 * Q), lambda bb, n: (0, 0)),
                  pl.BlockSpec((Q, SWA_WIDTH), cur), kv_p, kv_c, kv_p, kv_c],
        out_specs=pl.BlockSpec((Q, SWA_WIDTH), cur),
        scratch_shapes=[pltpu.VMEM((SWA_HEADS, Q, 2 * Q), F32)],
        compiler_params=_params(("arbitrary", "arbitrary")),
        name="swa_prompt",
    )(rel_bias, sinks, idx, q, k, k, v, v)


def _swa_sample_body(rb_ref, sink_ref, idx_ref, q_ref, kn_ref, vn_ref, kb_ref, vb_ref, o_ref, ko_ref, vo_ref,
                     bias_ref, extra_ref):
    W = WINDOW
    G, KVH = SWA_GROUP, SWA_KV_HEADS

    @pl.when(pl.program_id(0) == 0)
    def _():
        idx = jnp.broadcast_to(idx_ref[...], (SUBLANES, W))
        row = lax.broadcasted_iota(jnp.int32, (SUBLANES, W), 0)
        acc = jnp.zeros((SUBLANES, W), F32)
        ext = jnp.zeros((SUBLANES, LANES), F32)
        lane = lax.broadcasted_iota(jnp.int32, (SUBLANES, LANES), 1)
        for h in range(KVH):
            for g in range(G):
                head = h * G + g
                r = h * G + g
                acc = jnp.where(row == r, _bias_from_buckets(idx, rb_ref, head), acc)
                ext = jnp.where((row == r) & (lane == 0), rb_ref[0, head], ext)
                ext = jnp.where((row == r) & (lane == 1), sink_ref[head], ext)
        bias_ref[...] = acc
        extra_ref[...] = ext

    TB = q_ref.shape[0]
    lane_kv0 = lax.broadcasted_iota(jnp.int32, (TB, G, LANES), 2) < HEAD_DIM
    q4 = q_ref[...]
    qrows = jnp.concatenate([jnp.where(lane_kv0, q4, 0.0), jnp.where(lane_kv0, 0.0, q4)], axis=1)
    kb = kb_ref[...]
    vb = vb_ref[...]
    kn = kn_ref[...]
    vn = vn_ref[...]
    scale = HEAD_DIM ** -0.5
    bdims = (((2,), (2,)), ((0,), (0,)))
    s = lax.dot_general(qrows.astype(BF16), kb.astype(BF16), bdims, preferred_element_type=F32) * scale
    s = s + bias_ref[...][None]
    s_self = jnp.sum(qrows * kn, axis=-1, keepdims=True) * scale + extra_ref[:, 0:1][None]
    sink = extra_ref[:, 1:2][None]
    m = jnp.maximum(jnp.maximum(jnp.max(s, axis=-1, keepdims=True), s_self), sink)
    p = jnp.exp(s - m)
    p_self = jnp.exp(s_self - m)
    den = jnp.sum(p, axis=-1, keepdims=True) + p_self + jnp.exp(sink - m)
    pv = lax.dot_general(p.astype(BF16), vb.astype(BF16), (((2,), (1,)), ((0,), (0,))), preferred_element_type=F32)
    o = (pv + p_self * vn) / den
    o_ref[...] = jnp.where(lane_kv0, o[:, 0:G], o[:, G:2 * G])

    rowmod = lax.broadcasted_iota(jnp.int32, (TB, W, KV_WIDTH), 1)
    for buf, new, out in ((kb, kn, ko_ref), (vb, vn, vo_ref)):
        rolled = pltpu.roll(buf.reshape(TB * W, KV_WIDTH), TB * W - 1, 0).reshape(TB, W, KV_WIDTH)
        out[...] = jnp.where(rowmod == W - 1, new, rolled)


def _swa_sample(q, k, v, kbuf, vbuf, rel_bias, sinks):
    B = q.shape[0]
    W = WINDOW
    tb = 16
    idx = jnp.asarray(_rel_bucket_np(W - np.arange(W))[None, :])
    q4 = q.reshape(B, SWA_GROUP, LANES)
    kn = k.reshape(B, 1, KV_WIDTH)
    vn = v.reshape(B, 1, KV_WIDTH)
    kb = kbuf.reshape(B, W, KV_WIDTH)
    vb = vbuf.reshape(B, W, KV_WIDTH)
    smem = pl.BlockSpec(memory_space=pltpu.SMEM)
    b3 = lambda i: (i, 0, 0)
    o, ko, vo = pl.pallas_call(
        _swa_sample_body,
        out_shape=(jax.ShapeDtypeStruct((B, SWA_GROUP, LANES), F32), jax.ShapeDtypeStruct((B, W, KV_WIDTH), F32),
                   jax.ShapeDtypeStruct((B, W, KV_WIDTH), F32)),
        grid=(B // tb,),
        in_specs=[smem, smem, pl.BlockSpec((1, W), lambda i: (0, 0)),
                  pl.BlockSpec((tb, SWA_GROUP, LANES), b3), pl.BlockSpec((tb, 1, KV_WIDTH), b3),
                  pl.BlockSpec((tb, 1, KV_WIDTH), b3), pl.BlockSpec((tb, W, KV_WIDTH), b3),
                  pl.BlockSpec((tb, W, KV_WIDTH), b3)],
        out_specs=(pl.BlockSpec((tb, SWA_GROUP, LANES), b3), pl.BlockSpec((tb, W, KV_WIDTH), b3),
                   pl.BlockSpec((tb, W, KV_WIDTH), b3)),
        scratch_shapes=[pltpu.VMEM((SUBLANES, W), F32), pltpu.VMEM((SUBLANES, LANES), F32)],
        compiler_params=_params(("arbitrary",)),
        name="swa_sample",
    )(rel_bias, sinks, idx, q4, kn, vn, kb, vb)
    return o.reshape(B, SWA_WIDTH), ko.reshape(kbuf.shape), vo.reshape(vbuf.shape)


def _outproj_body(per_row, tiles_per_seq, prompt_row0, x_ref, mod_ref, yr_ref, ya_ref, wr_ref, wa_ref, gpost_ref, o_ref):
    D = D_MODEL
    m = _mod_rows(mod_ref, per_row, tiles_per_seq, prompt_row0)
    mix = _dot(yr_ref[...], wr_ref[...]) + _dot(ya_ref[...], wa_ref[...])
    o_ref[...] = x_ref[...] + m[:, 2 * D:3 * D] * _rms(mix, gpost_ref[...])


def _outproj(x, mod, yr, ya, lw, per_row, tm, tiles_per_seq, prompt_row0):
    N, D = x.shape
    W = RWKV_WIDTH
    row = lambda i: (i, 0)
    fixed = lambda i: (0, 0)
    half = pl.BlockSpec((tm, W), row)
    return pl.pallas_call(
        functools.partial(_outproj_body, per_row, tiles_per_seq, prompt_row0),
        out_shape=jax.ShapeDtypeStruct((N, D), F32),
        grid=(N // tm,),
        in_specs=[pl.BlockSpec((tm, D), row), _mod_spec(mod, per_row, tm, 1), half, half,
                  pl.BlockSpec((W, D), fixed), pl.BlockSpec((W, D), fixed), pl.BlockSpec((1, D), fixed)],
        out_specs=pl.BlockSpec((tm, D), row),
        compiler_params=_params(("arbitrary",)),
        name="out_proj",
    )(x, mod, yr, ya, lw["w_out_r"], lw["w_out_a"], lw["g_post0"])


def _ffn_body(moe, per_row, tiles_per_seq, prompt_row0, x_ref, mod_ref, gpre_ref, gpost_ref, rw_ref, rb_ref,
              wg_ref, wu_ref, wd_ref, o_ref, h_ref, acc_ref, comb_ref):
    D = D_MODEL
    e = pl.program_id(1)
    f = pl.program_id(2)
    first = (e == 0) & (f == 0)
    last = (e == pl.num_programs(1) - 1) & (f == pl.num_programs(2) - 1)

    @pl.when(first)
    def _():
        m = _mod_rows(mod_ref, per_row, tiles_per_seq, prompt_row0)
        h = _rms(x_ref[...], gpre_ref[...]) * (1.0 + m[:, D:2 * D]) + m[:, 0:D]
        h_ref[...] = h.astype(BF16)
        acc_ref[...] = jnp.zeros_like(acc_ref)
        if moe:
            logits = _dot3(h, rw_ref[...]) + rb_ref[...]
            lane = lax.broadcasted_iota(jnp.int32, logits.shape, 1)
            m1 = jnp.max(logits, axis=-1, keepdims=True)
            i1 = jnp.min(jnp.where(logits == m1, lane, LANES), axis=-1, keepdims=True)
            rest = jnp.where(lane == i1, -jnp.inf, logits)
            m2 = jnp.max(rest, axis=-1, keepdims=True)
            i2 = jnp.min(jnp.where(rest == m2, lane, LANES), axis=-1, keepdims=True)
            e2 = jnp.exp(m2 - m1)
            comb_ref[...] = jnp.where(lane == i1, 1.0 / (1.0 + e2), 0.0) + jnp.where(lane == i2, e2 / (1.0 + e2), 0.0)

    h = h_ref[...]
    gate = _dot(h, wg_ref[...])
    up = _dot(h, wu_ref[...])
    act = gate * _sigmoid(gate) * up
    if moe:
        comb = comb_ref[...]
        lane = lax.broadcasted_iota(jnp.int32, comb.shape, 1)
        act = act * jnp.sum(jnp.where(lane == e, comb, 0.0), axis=-1, keepdims=True)
    acc_ref[...] += _dot(act, wd_ref[...])

    @pl.when(last)
    def _():
        m = _mod_rows(mod_ref, per_row, tiles_per_seq, prompt_row0)
        o_ref[...] = x_ref[...] + m[:, 2 * D:3 * D] * _rms(acc_ref[...], gpost_ref[...])


def _ffn(x, mod, gpre, gpost, router_w, router_b, wg, wu, wd, w0, E, moe, per_row, tm, tiles_per_seq, prompt_row0,
         tf):
    N, D = x.shape
    F = wg.shape[-1]
    row = lambda i, e, f: (i, 0)
    fixed = lambda i, e, f: (0, 0)
    return pl.pallas_call(
        functools.partial(_ffn_body, moe, per_row, tiles_per_seq, prompt_row0),
        out_shape=jax.ShapeDtypeStruct((N, D), F32),
        grid=(N // tm, E, F // tf),
        in_specs=[pl.BlockSpec((tm, D), row), _mod_spec(mod, per_row, tm, 3),
                  pl.BlockSpec((1, D), fixed), pl.BlockSpec((1, D), fixed),
                  pl.BlockSpec((D, LANES), fixed), pl.BlockSpec((1, LANES), fixed),
                  pl.BlockSpec((None, D, tf), lambda i, e, f: (w0 + e, 0, f)),
                  pl.BlockSpec((None, D, tf), lambda i, e, f: (w0 + e, 0, f)),
                  pl.BlockSpec((None, tf, D), lambda i, e, f: (w0 + e, f, 0))],
        out_specs=pl.BlockSpec((tm, D), row),
        scratch_shapes=[pltpu.VMEM((tm, D), BF16), pltpu.VMEM((tm, D), F32), pltpu.VMEM((tm, LANES), F32)],
        compiler_params=_params(("arbitrary", "arbitrary", "arbitrary")),
        name="moe_ffn" if moe else "dense_ffn",
    )(x, mod, gpre, gpost, router_w, router_b, wg, wu, wd)


ROW_TILE = D_MODEL // LANES
EXPERT_TILE = 512
INFO_E1, INFO_E2, INFO_P1, INFO_P2, INFO_POS1, INFO_POS2 = range(6)


def _top2(logits):
    lane = lax.broadcasted_iota(jnp.int32, logits.shape, 1)
    m1 = jnp.max(logits, axis=-1, keepdims=True)
    i1 = jnp.min(jnp.where(logits == m1, lane, LANES), axis=-1, keepdims=True)
    rest = jnp.where(lane == i1, -jnp.inf, logits)
    m2 = jnp.max(rest, axis=-1, keepdims=True)
    i2 = jnp.min(jnp.where(rest == m2, lane, LANES), axis=-1, keepdims=True)
    e2 = jnp.exp(m2 - m1)
    return lane, i1, i2, 1.0 / (1.0 + e2), e2 / (1.0 + e2)


def _to_row_tiles(ref, x):
    rows = x.shape[0]
    for c in range(ROW_TILE):
        ref[pl.ds(c, rows, stride=ROW_TILE), :] = x[:, c * LANES:(c + 1) * LANES]


def _from_row_tiles(ref, row0, rows):
    return jnp.concatenate([ref[pl.ds(row0 * ROW_TILE + c, rows, stride=ROW_TILE), :] for c in range(ROW_TILE)],
                           axis=1)


def _route_body(tiles_per_seq, prompt_row0, x_ref, mod_ref, gpre_ref, rw_ref, rb_ref, h_ref, info_ref, cnt_ref,
                base_ref):
    D = D_MODEL

    @pl.when(pl.program_id(0) == 0)
    def _():
        base_ref[...] = jnp.zeros_like(base_ref)

    m = _mod_rows(mod_ref, False, tiles_per_seq, prompt_row0)
    h = _rms(x_ref[...], gpre_ref[...]) * (1.0 + m[:, D:2 * D]) + m[:, 0:D]
    _to_row_tiles(h_ref, h)
    lane, i1, i2, p1, p2 = _top2(_dot3(h, rw_ref[...]) + rb_ref[...])
    onehot = jnp.where((lane == i1) | (lane == i2), 1.0, 0.0)
    T = h.shape[0]
    before = lax.broadcasted_iota(jnp.int32, (T, T), 1) < lax.broadcasted_iota(jnp.int32, (T, T), 0)
    rank = _dot(jnp.where(before, 1.0, 0.0), onehot) + base_ref[0:1, :]
    pos1 = jnp.sum(jnp.where(lane == i1, rank, 0.0), axis=-1, keepdims=True)
    pos2 = jnp.sum(jnp.where(lane == i2, rank, 0.0), axis=-1, keepdims=True)
    info = jnp.zeros(onehot.shape, F32)
    for col, val in ((INFO_E1, i1.astype(F32)), (INFO_E2, i2.astype(F32)), (INFO_P1, p1), (INFO_P2, p2),
                     (INFO_POS1, pos1), (INFO_POS2, pos2)):
        info = jnp.where(lane == col, val, info)
    info_ref[...] = info
    total = base_ref[...] + jnp.sum(onehot, axis=0, keepdims=True)
    base_ref[...] = total
    cnt_ref[...] = total


def _moe_route(x, mod, gpre, rw, rb, tm, tiles_per_seq, prompt_row0):
    N, D = x.shape
    row = lambda i: (i, 0)
    fixed = lambda i: (0, 0)
    return pl.pallas_call(
        functools.partial(_route_body, tiles_per_seq, prompt_row0),
        out_shape=(jax.ShapeDtypeStruct((N * ROW_TILE, LANES), F32), jax.ShapeDtypeStruct((N, LANES), F32),
                   jax.ShapeDtypeStruct((SUBLANES, LANES), F32)),
        grid=(N // tm,),
        in_specs=[pl.BlockSpec((tm, D), row), _mod_spec(mod, False, tm, 1), pl.BlockSpec((1, D), fixed),
                  pl.BlockSpec((D, LANES), fixed), pl.BlockSpec((1, LANES), fixed)],
        out_specs=(pl.BlockSpec((tm * ROW_TILE, LANES), row), pl.BlockSpec((tm, LANES), row),
                   pl.BlockSpec((SUBLANES, LANES), fixed)),
        scratch_shapes=[pltpu.VMEM((SUBLANES, LANES), F32)],
        compiler_params=_params(("arbitrary",)),
        name="moe_route",
    )(x, mod, gpre, rw, rb)


def _row_copy(src_hbm, dst_vmem, sem, src_row, dst_row):
    return pltpu.make_async_copy(src_hbm.at[pl.ds(pl.multiple_of(src_row * ROW_TILE, ROW_TILE), ROW_TILE)],
                                 dst_vmem.at[pl.ds(pl.multiple_of(dst_row * ROW_TILE, ROW_TILE), ROW_TILE)], sem)


def _start_rows(idx_ref, idx0, src_hbm, dst_vmem, sem, rows):
    def body(r, carry):
        _row_copy(src_hbm, dst_vmem, sem, idx_ref[idx0 + r], r).start()
        return carry
    lax.fori_loop(0, rows, body, 0, unroll=16)


def _wait_rows(src_hbm, dst_vmem, sem, rows):
    pltpu.make_async_copy(src_hbm.at[pl.ds(0, rows * ROW_TILE)], dst_vmem.at[pl.ds(0, rows * ROW_TILE)], sem).wait()


def _experts_body(te_ref, nv_ref, src_ref, h_hbm, wg_ref, wu_ref, wd_ref, o_ref, xbuf, hb_ref, acc_ref, sem):
    TM = EXPERT_TILE
    i = pl.program_id(0)
    f = pl.program_id(1)
    slot = i % 2
    n_valid = nv_ref[0]
    valid = i < n_valid

    @pl.when(f == 0)
    def _():
        @pl.when((i == 0) & valid)
        def _():
            _start_rows(src_ref, 0, h_hbm, xbuf.at[0], sem.at[0], TM)

        @pl.when(i + 1 < n_valid)
        def _():
            _start_rows(src_ref, (i + 1) * TM, h_hbm, xbuf.at[1 - slot], sem.at[1 - slot], TM)

        @pl.when(valid)
        def _():
            _wait_rows(h_hbm, xbuf.at[slot], sem.at[slot], TM)
            hb_ref[...] = _from_row_tiles(xbuf.at[slot], 0, TM).astype(BF16)
            acc_ref[...] = jnp.zeros_like(acc_ref)

    @pl.when(valid)
    def _():
        h = hb_ref[...]
        gate = _dot(h, wg_ref[...])
        up = _dot(h, wu_ref[...])
        acc_ref[...] += _dot(gate * _sigmoid(gate) * up, wd_ref[...])

    @pl.when(f == pl.num_programs(1) - 1)
    def _():
        @pl.when(valid)
        def _():
            _to_row_tiles(o_ref, acc_ref[...])

        @pl.when(jnp.logical_not(valid))
        def _():
            o_ref[...] = jnp.zeros_like(o_ref)


def _moe_experts(h_rows, tile_expert, n_valid, src_tok, wg, wu, wd, tf):
    TM = EXPERT_TILE
    P = src_tok.shape[0]
    _, D, F = wg.shape
    grid_spec = pltpu.PrefetchScalarGridSpec(
        num_scalar_prefetch=3,
        grid=(P // TM, F // tf),
        in_specs=[pl.BlockSpec(memory_space=pl.ANY),
                  pl.BlockSpec((None, D, tf), lambda i, f, te, nv, src: (te[i], 0, f)),
                  pl.BlockSpec((None, D, tf), lambda i, f, te, nv, src: (te[i], 0, f)),
                  pl.BlockSpec((None, tf, D), lambda i, f, te, nv, src: (te[i], f, 0))],
        out_specs=pl.BlockSpec((TM * ROW_TILE, LANES), lambda i, f, te, nv, src: (i, 0)),
        scratch_shapes=[pltpu.VMEM((2, TM * ROW_TILE, LANES), F32), pltpu.VMEM((TM, D), BF16),
                        pltpu.VMEM((TM, D), F32), pltpu.SemaphoreType.DMA((2,))],
    )
    return pl.pallas_call(
        _experts_body,
        out_shape=jax.ShapeDtypeStruct((P * ROW_TILE, LANES), F32),
        grid_spec=grid_spec,
        compiler_params=_params(("arbitrary", "arbitrary")),
        name="moe_experts",
    )(tile_expert, n_valid, src_tok, h_rows, wg, wu, wd)


def _combine_body(tiles_per_seq, prompt_row0, dest_ref, f_hbm, x_ref, mod_ref, info_ref, gpost_ref, o_ref, gbuf, sem):
    D = D_MODEL
    i = pl.program_id(0)
    T = x_ref.shape[0]
    slot = i % 2

    @pl.when(i == 0)
    def _():
        _start_rows(dest_ref, 0, f_hbm, gbuf.at[0], sem.at[0], 2 * T)

    @pl.when(i + 1 < pl.num_programs(0))
    def _():
        _start_rows(dest_ref, (i + 1) * 2 * T, f_hbm, gbuf.at[1 - slot], sem.at[1 - slot], 2 * T)

    _wait_rows(f_hbm, gbuf.at[slot], sem.at[slot], 2 * T)
    info = info_ref[...]
    f1 = _from_row_tiles(gbuf.at[slot], 0, T)
    f2 = _from_row_tiles(gbuf.at[slot], T, T)
    y = info[:, INFO_P1:INFO_P1 + 1] * f1 + info[:, INFO_P2:INFO_P2 + 1] * f2
    m = _mod_rows(mod_ref, False, tiles_per_seq, prompt_row0)
    o_ref[...] = x_ref[...] + m[:, 2 * D:3 * D] * _rms(y, gpost_ref[...])


def _moe_combine(dest, f_rows, x, mod, info, gpost, tm, tiles_per_seq, prompt_row0):
    N, D = x.shape
    row = lambda i, d: (i, 0)
    fixed = lambda i, d: (0, 0)
    grid_spec = pltpu.PrefetchScalarGridSpec(
        num_scalar_prefetch=1,
        grid=(N // tm,),
        in_specs=[pl.BlockSpec(memory_space=pl.ANY), pl.BlockSpec((tm, D), row),
                  pl.BlockSpec(mod.shape, fixed), pl.BlockSpec((tm, LANES), row), pl.BlockSpec((1, D), fixed)],
        out_specs=pl.BlockSpec((tm, D), row),
        scratch_shapes=[pltpu.VMEM((2, 2 * tm * ROW_TILE, LANES), F32), pltpu.SemaphoreType.DMA((2,))],
    )
    return pl.pallas_call(
        functools.partial(_combine_body, tiles_per_seq, prompt_row0),
        out_shape=jax.ShapeDtypeStruct((N, D), F32),
        grid_spec=grid_spec,
        compiler_params=_params(("arbitrary",)),
        name="moe_combine",
    )(dest, f_rows, x, mod, info, gpost)


def _moe_prompt(x, mod, gpre, gpost, fw, tm, tiles_per_seq, prompt_row0, tf):
    N = x.shape[0]
    TM = EXPERT_TILE
    n_tiles = (2 * N) // TM + N_EXPERTS
    h_rows, info, cnt = _moe_route(x, mod, gpre, fw["rw"], fw["rb"], tm, tiles_per_seq, prompt_row0)
    e1 = info[:, INFO_E1].astype(jnp.int32)
    e2 = info[:, INFO_E2].astype(jnp.int32)
    counts = cnt[0, :N_EXPERTS].astype(jnp.int32)
    padded = ((counts + TM - 1) // TM) * TM
    ends = jnp.cumsum(padded)
    starts = ends - padded
    dest1 = starts[e1] + info[:, INFO_POS1].astype(jnp.int32)
    dest2 = starts[e2] + info[:, INFO_POS2].astype(jnp.int32)
    tok = jnp.arange(N, dtype=jnp.int32)
    src_tok = jnp.zeros((n_tiles * TM,), jnp.int32).at[jnp.concatenate([dest1, dest2])].set(
        jnp.concatenate([tok, tok]), unique_indices=True)
    tile_start = jnp.arange(n_tiles, dtype=jnp.int32) * TM
    tile_expert = jnp.minimum(jnp.sum(ends[None, :] <= tile_start[:, None], axis=1), N_EXPERTS - 1).astype(jnp.int32)
    n_valid = (ends[-1:] // TM).astype(jnp.int32)
    f_rows = _moe_experts(h_rows, tile_expert + fw["w0"], n_valid, src_tok, fw["wg"], fw["wu"], fw["wd"], tf)
    dest = jnp.concatenate([dest1.reshape(-1, 1, tm), dest2.reshape(-1, 1, tm)], axis=1).reshape(-1)
    return _moe_combine(dest, f_rows, x, mod, info, gpost, tm, tiles_per_seq, prompt_row0)


def _layer_weights(p, l):
    W = RWKV_WIDTH
    heads = np.arange(W) // HEAD_DIM
    headsum = jnp.asarray((heads[:, None] == heads[None, :]).astype(np.float32))
    perm = _q_perm()
    w_in = p["w_in"][l]
    w_in = jnp.concatenate([w_in[:, :RWKV_COLS], w_in[:, RWKV_COLS + perm], w_in[:, RWKV_COLS + SWA_WIDTH:]], axis=1)
    w_out = p["w_out"][l]
    zeros = jnp.zeros((DECAY_LORA, W), F32)
    row = lambda t: t.reshape(1, -1)
    return {
        "w_in": w_in.astype(BF16),
        "w_out_r": w_out[:W].astype(BF16),
        "w_out_a": w_out[W + perm].astype(BF16),
        "mu": row(p["mu_shift"][l]),
        "wd": jnp.concatenate([p["w_decay_up"][l], zeros], axis=0),
        "wi": jnp.concatenate([zeros, p["w_iclr_up"][l]], axis=0),
        "wg": p["w_gate_up"][l],
        "dbase": row(p["decay_base"][l]), "ibase": row(p["iclr_base"][l]),
        "kk": row(p["k_k"][l]), "ka": row(p["k_a"][l]), "rk": row(p["r_k"][l]),
        "lnw": row(p["lnx_w"][l]), "lnb": row(p["lnx_b"][l]),
        "headsum": headsum.astype(BF16), "headmean": (headsum / HEAD_DIM).astype(BF16),
        "g_pre0": row(p["norm_pre"][l, 0]), "g_pre1": row(p["norm_pre"][l, 1]),
        "g_post0": row(p["norm_post"][l, 0]), "g_post1": row(p["norm_post"][l, 1]),
    }


def _stacked_bf16(w):
    return w.astype(BF16).reshape((-1,) + w.shape[-2:])


def _ffn_weights(p, stacks, l):
    i = l // 2
    if l % 2 == 0:
        wg, wu, wd = stacks["dense"]
        return dict(moe=False, rw=jnp.zeros((D_MODEL, LANES), F32), rb=jnp.zeros((1, LANES), F32),
                    wg=wg, wu=wu, wd=wd, w0=i, n=1)
    rw = jnp.zeros((D_MODEL, LANES), F32).at[:, :N_EXPERTS].set(p["router_w"][i])
    rb = jnp.full((1, LANES), NEG_INF, F32).at[0, :N_EXPERTS].set(p["router_b"][i])
    wg, wu, wd = stacks["moe"]
    return dict(moe=True, rw=rw, rb=rb, wg=wg, wu=wu, wd=wd, w0=i * N_EXPERTS, n=N_EXPERTS)


def _pick_tile(n, pref):
    t = min(pref, n)
    while n % t:
        t //= 2
    return t


def _ffn_tile(f):
    for t in (1408, 896, 512, 256, 128):
        if f % t == 0:
            return t
    return f


def _trunk(x3, mods, lws, fws, p, prompt, prompt_row0, state=None):
    B, T, D = x3.shape
    N = B * T
    x = x3.reshape(N, D)
    per_row = not prompt
    tm = _pick_tile(T if prompt else N, 512)
    tps = (T // tm) if prompt else 1
    depth = len(lws)
    wkv_out, shift_out, k_out, v_out = [], [], [], []
    for l in range(depth):
        lw, fw = lws[l], fws[l]
        mod0, mod1 = mods[2 * l], mods[2 * l + 1]
        pr, q, k, v = _inproj(x, mod0, lw["g_pre0"], lw["w_in"], per_row, tm, tps, prompt_row0)
        if prompt:
            yr, hbd = _wkv_prompt(pr, lw, B, T)
            n_pairs = RWKV_WIDTH // PAIR
            hb = hbd.reshape(B, n_pairs, 2, HEAD_DIM, 2, HEAD_DIM)
            s_kv = jnp.stack([hb[:, :, 0, :, 0, :], hb[:, :, 1, :, 1, :]], axis=2)
            s_new = jnp.swapaxes(s_kv.reshape(B, RWKV_HEADS, HEAD_DIM, HEAD_DIM), -1, -2)
            ya = _swa_prompt(q, k, v, p["rel_bias"], p["attn_sinks"][l], B, T)
            window = lambda t: t.reshape(B, T, KV_WIDTH)[:, -WINDOW:].reshape(B, WINDOW, SWA_KV_HEADS, HEAD_DIM)
            kb, vb = window(k), window(v)
            last = pr.reshape(B, T, RWKV_COLS)[:, -1]
        else:
            feats = _prep_sample(pr, state["shift"][l], lw)
            yr, s_new = _wkv_sample(state["wkv"][l], *feats, lw)
            ya, kb, vb = _swa_sample(q, k, v, state["k"][l], state["v"][l], p["rel_bias"], p["attn_sinks"][l])
            last = pr
        x = _outproj(x, mod0, yr, ya, lw, per_row, tm, tps, prompt_row0)
        tf = _ffn_tile(fw["wg"].shape[-1])
        if prompt and fw["moe"]:
            x = _moe_prompt(x, mod1, lw["g_pre1"], lw["g_post1"], fw, tm, tps, prompt_row0, tf)
        else:
            x = _ffn(x, mod1, lw["g_pre1"], lw["g_post1"], fw["rw"], fw["rb"], fw["wg"], fw["wu"], fw["wd"],
                     fw["w0"], fw["n"], fw["moe"], per_row, tm, tps, prompt_row0, tf)
        wkv_out.append(s_new)
        shift_out.append(last)
        k_out.append(kb)
        v_out.append(vb)
    return x.reshape(B, T, D), jnp.stack(wkv_out), jnp.stack(shift_out), jnp.stack(k_out), jnp.stack(v_out)


def _forward(x_prompt, x_sample, c_prompt, c_sample, state_wkv, state_shift, cache_swa_k, cache_swa_v, p):
    depth = p["w_in"].shape[0]
    Bp, Bs = c_prompt.shape[0], c_sample.shape[0]
    D = D_MODEL
    pad = (-(Bs + Bp)) % SUBLANES
    c_all = jnp.concatenate([c_sample, c_prompt, jnp.zeros((pad, D), F32)], axis=0)
    mods = _ada_all(c_all, p["ada_w"].reshape(2 * depth, D, 3 * D), p["ada_b"].reshape(2 * depth, 1, 3 * D))
    lws = [_layer_weights(p, l) for l in range(depth)]
    stacks = {"dense": tuple(_stacked_bf16(p[k]) for k in ("ffn_w_gate", "ffn_w_up", "ffn_w_down")),
              "moe": tuple(_stacked_bf16(p[k]) for k in ("moe_w_gate", "moe_w_up", "moe_w_down"))}
    fws = [_ffn_weights(p, stacks, l) for l in range(depth)]
    y_p, wkv_p, shift_p, k_p, v_p = _trunk(x_prompt, mods, lws, fws, p, True, Bs)
    state = {"wkv": state_wkv, "shift": state_shift, "k": cache_swa_k, "v": cache_swa_v}
    y_s, wkv_s, shift_s, k_s, v_s = _trunk(x_sample, mods, lws, fws, p, False, Bs, state)
    return (y_p, y_s, wkv_p, shift_p, k_p, v_p, wkv_s, shift_s, k_s, v_s)


def kernel(x_prompt, x_sample, c_prompt, c_sample, state_wkv, state_shift, cache_swa_k, cache_swa_v, rel_bias, ada_w, ada_b, norm_pre, norm_post, w_in, mu_shift, w_decay_up, decay_base, w_iclr_up, iclr_base, w_gate_up, k_k, k_a, r_k, lnx_w, lnx_b, attn_sinks, w_out, ffn_w_gate, ffn_w_up, ffn_w_down, router_w, router_b, moe_w_gate, moe_w_up, moe_w_down):
    p = {"rel_bias": rel_bias, "ada_w": ada_w, "ada_b": ada_b, "norm_pre": norm_pre, "norm_post": norm_post,
         "w_in": w_in, "mu_shift": mu_shift, "w_decay_up": w_decay_up, "decay_base": decay_base,
         "w_iclr_up": w_iclr_up, "iclr_base": iclr_base, "w_gate_up": w_gate_up, "k_k": k_k, "k_a": k_a,
         "r_k": r_k.reshape(r_k.shape[0], -1), "lnx_w": lnx_w, "lnx_b": lnx_b, "attn_sinks": attn_sinks,
         "w_out": w_out, "ffn_w_gate": ffn_w_gate, "ffn_w_up": ffn_w_up, "ffn_w_down": ffn_w_down,
         "router_w": router_w, "router_b": router_b, "moe_w_gate": moe_w_gate, "moe_w_up": moe_w_up,
         "moe_w_down": moe_w_down}
    return _forward(x_prompt, x_sample, c_prompt, c_sample, state_wkv, state_shift, cache_swa_k, cache_swa_v, p)
```

```python
import functools

import numpy as np
import jax
import jax.numpy as jnp
from jax import lax
from jax.experimental import pallas as pl
from jax.experimental.pallas import tpu as pltpu

F32 = jnp.float32
BF16 = jnp.bfloat16

D_MODEL = 1024
HEAD_DIM = 64
RWKV_WIDTH = 512
RWKV_HEADS = RWKV_WIDTH // HEAD_DIM
SWA_WIDTH = 512
SWA_HEADS = SWA_WIDTH // HEAD_DIM
SWA_KV_HEADS = 2
SWA_GROUP = SWA_HEADS // SWA_KV_HEADS
KV_WIDTH = SWA_KV_HEADS * HEAD_DIM
WINDOW = 128
DECAY_LORA = 64
ICLR_LORA = 64
GATE_LORA = 128
RWKV_COLS = 3 * RWKV_WIDTH + DECAY_LORA + ICLR_LORA + GATE_LORA
IN_COLS = RWKV_COLS + SWA_WIDTH + 2 * KV_WIDTH
LORA_OFF = 3 * RWKV_WIDTH
GATE_OFF = LORA_OFF + DECAY_LORA + ICLR_LORA
LN_X_EPS = 64e-5
RMS_EPS = 1e-6
N_BUCKETS = 32
MAX_DISTANCE = 128
N_EXPERTS = 8
NEG_INF = -1e30

LANES = 128
SUBLANES = 8
VMEM_LIMIT = 56 * 1024 * 1024

WKV_CHUNK = 64
PAIR = 2 * HEAD_DIM

NN = (((1,), (0,)), ((), ()))
NT = (((1,), (1,)), ((), ()))
TN = (((0,), (0,)), ((), ()))


def _dot(a, b, dims=NN):
    return lax.dot_general(a.astype(BF16), b.astype(BF16), dims, preferred_element_type=F32)


def _split(x, pieces):
    out = []
    for _ in range(pieces - 1):
        hi = x.astype(BF16)
        out.append(hi)
        x = x - hi.astype(F32)
    out.append(x.astype(BF16))
    return out


def _select_dot(x, sel, pieces=3, sel_left=False):
    d = lambda t: lax.dot_general(*((sel, t) if sel_left else (t, sel)), NN, preferred_element_type=F32)
    return sum(d(t) for t in _split(x, pieces))


def _dot3(a, b):
    a_hi, a_lo = _split(a, 2)
    b_hi, b_lo = _split(b, 2)
    d = lambda x, y: lax.dot_general(x, y, NN, preferred_element_type=F32)
    return d(a_hi, b_hi) + d(a_lo, b_hi) + d(a_hi, b_lo)


def _sigmoid(x):
    return 1.0 / (1.0 + jnp.exp(-x))


def _params(sem):
    return pltpu.CompilerParams(dimension_semantics=sem, vmem_limit_bytes=VMEM_LIMIT)


def _rms(x, g):
    return x * lax.rsqrt(jnp.mean(x * x, axis=-1, keepdims=True) + RMS_EPS) * g


def _mod_rows(mod_ref, per_row, tiles_per_seq, prompt_row0):
    if per_row:
        return mod_ref[...]
    b = pl.program_id(0) // tiles_per_seq
    return mod_ref[pl.ds(prompt_row0 + b, 1), :]


def _ada_body(c_ref, w_ref, b_ref, o_ref):
    c = c_ref[...]
    o_ref[...] = _dot(c * _sigmoid(c), w_ref[...]) + b_ref[...]


def _ada_all(c_all, ada_w, ada_b):
    R, D = c_all.shape
    n = ada_w.shape[0]
    tn = 1024
    return pl.pallas_call(
        _ada_body,
        out_shape=jax.ShapeDtypeStruct((n, R, 3 * D), F32),
        grid=(n, 3 * D // tn),
        in_specs=[pl.BlockSpec((R, D), lambda i, j: (0, 0)),
                  pl.BlockSpec((None, D, tn), lambda i, j: (i, 0, j)),
                  pl.BlockSpec((None, 1, tn), lambda i, j: (i, 0, j))],
        out_specs=pl.BlockSpec((None, R, tn), lambda i, j: (i, 0, j)),
        compiler_params=_params(("arbitrary", "arbitrary")),
        name="ada_mod",
    )(c_all, ada_w, ada_b)


def _inproj_body(per_row, tiles_per_seq, prompt_row0, x_ref, mod_ref, g_ref, w_ref, pr_ref, q_ref, k_ref, v_ref):
    D = D_MODEL
    m = _mod_rows(mod_ref, per_row, tiles_per_seq, prompt_row0)
    h = _rms(x_ref[...], g_ref[...]) * (1.0 + m[:, D:2 * D]) + m[:, 0:D]
    proj = _dot(h, w_ref[...])
    pr_ref[...] = proj[:, 0:RWKV_COLS]
    q_ref[...] = proj[:, RWKV_COLS:RWKV_COLS + SWA_WIDTH]
    k_ref[...] = proj[:, RWKV_COLS + SWA_WIDTH:RWKV_COLS + SWA_WIDTH + KV_WIDTH]
    v_ref[...] = proj[:, RWKV_COLS + SWA_WIDTH + KV_WIDTH:IN_COLS]


def _mod_spec(mod, per_row, tm, nargs):
    R = mod.shape[0]
    if per_row:
        return pl.BlockSpec((tm, 3 * D_MODEL), lambda i, *_: (0, 0))
    return pl.BlockSpec((R, 3 * D_MODEL), lambda i, *_: (0, 0))


def _inproj(x, mod, g, w, per_row, tm, tiles_per_seq, prompt_row0):
    N, D = x.shape
    row = lambda i: (i, 0)
    fixed = lambda i: (0, 0)
    return pl.pallas_call(
        functools.partial(_inproj_body, per_row, tiles_per_seq, prompt_row0),
        out_shape=(jax.ShapeDtypeStruct((N, RWKV_COLS), F32), jax.ShapeDtypeStruct((N, SWA_WIDTH), F32),
                   jax.ShapeDtypeStruct((N, KV_WIDTH), F32), jax.ShapeDtypeStruct((N, KV_WIDTH), F32)),
        grid=(N // tm,),
        in_specs=[pl.BlockSpec((tm, D), row), _mod_spec(mod, per_row, tm, 1),
                  pl.BlockSpec((1, D), fixed), pl.BlockSpec((D, IN_COLS), fixed)],
        out_specs=(pl.BlockSpec((tm, RWKV_COLS), row), pl.BlockSpec((tm, SWA_WIDTH), row),
                   pl.BlockSpec((tm, KV_WIDTH), row), pl.BlockSpec((tm, KV_WIDTH), row)),
        compiler_params=_params(("arbitrary",)),
        name="in_proj",
    )(x, mod, g, w)


PREP_KEYS = ("mu", "wd", "wi", "wg", "dbase", "ibase", "kk", "ka", "rk", "headsum")
POST_KEYS = ("lnw", "lnb", "headmean")


def _prep_specs(fixed):
    W = RWKV_WIDTH
    vec = pl.BlockSpec((1, W), fixed)
    return [pl.BlockSpec((1, RWKV_COLS), fixed), pl.BlockSpec((LANES, W), fixed), pl.BlockSpec((LANES, W), fixed),
            pl.BlockSpec((GATE_LORA, W), fixed), vec, vec, vec, vec, vec, pl.BlockSpec((W, W), fixed)]


def _rwkv_features(pr, shifted, mu_ref, wd_ref, wi_ref, wg_ref, dbase_ref, ibase_ref, kk_ref, ka_ref, rk_ref, hs_ref):
    W = RWKV_WIDTH
    xs = pr + (shifted - pr) * mu_ref[...]
    r = xs[:, 0:W]
    k = xs[:, W:2 * W]
    v = xs[:, 2 * W:3 * W]
    lora = xs[:, LORA_OFF:GATE_OFF]
    gl = xs[:, GATE_OFF:RWKV_COLS]
    z = dbase_ref[...] + _dot3(jnp.tanh(lora), wd_ref[...])
    ld = -float(np.exp(-0.5)) * _sigmoid(z)
    iclr = _sigmoid(ibase_ref[...] + _dot3(lora, wi_ref[...]))
    g = _dot3(_sigmoid(gl), wg_ref[...])
    hs = hs_ref[...]
    kk = k * kk_ref[...]
    kk = kk / jnp.maximum(jnp.sqrt(_select_dot(kk * kk, hs, 2)), 1e-12)
    kh = k * (1.0 + (iclr - 1.0) * ka_ref[...])
    bonus = _select_dot(r * kh * rk_ref[...], hs, 2) * v
    return r, ld, kh, v, -kk, kk * iclr, g, bonus


def _rwkv_post(y, g, bonus, lnw, lnb, hm):
    mean = _select_dot(y, hm, 2)
    dv = y - mean
    var = _select_dot(dv * dv, hm, 2)
    return (dv * lax.rsqrt(var + LN_X_EPS) * lnw + lnb + bonus) * g


def _prep_body(pr_ref, prev_ref, *refs):
    outs = refs[len(PREP_KEYS):]
    for ref, val in zip(outs, _rwkv_features(pr_ref[...], prev_ref[...], *refs[:len(PREP_KEYS)])):
        ref[...] = val


def _prep_prompt_body(tiles_per_seq, pr_ref, prev_ref, *refs):
    pr = pr_ref[...]
    first = (pl.program_id(0) % tiles_per_seq) == 0
    carry = jnp.where(first, 0.0, prev_ref[SUBLANES - 1:SUBLANES, :])
    shifted = jnp.where(lax.broadcasted_iota(jnp.int32, pr.shape, 0) == 0, carry, pltpu.roll(pr, 1, 0))
    for ref, val in zip(refs[len(PREP_KEYS):], _rwkv_features(pr, shifted, *refs[:len(PREP_KEYS)])):
        ref[...] = val


def _prep_prompt(pr, lw, tm, tiles_per_seq):
    N = pr.shape[0]
    W = RWKV_WIDTH
    row = lambda i: (i, 0)
    fixed = lambda i: (0, 0)
    per = tm // SUBLANES
    out = jax.ShapeDtypeStruct((N, W), F32)
    return pl.pallas_call(
        functools.partial(_prep_prompt_body, tiles_per_seq),
        out_shape=(out,) * 8,
        grid=(N // tm,),
        in_specs=[pl.BlockSpec((tm, RWKV_COLS), row),
                  pl.BlockSpec((SUBLANES, RWKV_COLS), lambda i: (jnp.maximum(i * per - 1, 0), 0))] + _prep_specs(fixed),
        out_specs=(pl.BlockSpec((tm, W), row),) * 8,
        compiler_params=_params(("arbitrary",)),
        name="rwkv_prep",
    )(pr, pr, *[lw[k] for k in PREP_KEYS])


def _prep_sample(pr, prev, lw):
    N = pr.shape[0]
    W = RWKV_WIDTH
    row = lambda i: (i, 0)
    fixed = lambda i: (0, 0)
    out = jax.ShapeDtypeStruct((N, W), F32)
    return pl.pallas_call(
        _prep_body,
        out_shape=(out,) * 8,
        grid=(1,),
        in_specs=[pl.BlockSpec((N, RWKV_COLS), row), pl.BlockSpec((N, RWKV_COLS), row)] + _prep_specs(fixed),
        out_specs=(pl.BlockSpec((N, W), row),) * 8,
        compiler_params=_params(("arbitrary",)),
        name="rwkv_prep",
    )(pr, prev, *[lw[k] for k in PREP_KEYS])


def _stack_heads(x, lane_head0):
    return jnp.concatenate([jnp.where(lane_head0, x, 0.0), jnp.where(lane_head0, 0.0, x)], axis=0)


def _fold_heads(x):
    c = x.shape[0] // 2
    return x[0:c] + x[c:2 * c]


def _dots(xs, ys, dims=NN):
    return [_dot(x, y, dims) for x, y in zip(xs, ys)]


def _unit_lower_inverse(ns, same16, eye, between=lambda: None):
    nd = [jnp.where(same16, n, 0.0) for n in ns]
    no = [n - d for n, d in zip(ns, nd)]
    n2 = _dots(nd, nd)
    between()
    n4 = _dots(n2, n2)
    n8 = _dots(n4, n4)
    between()
    td = [eye + d for d in nd]
    for pw in (n2, n4, n8):
        td = [t + u for t, u in zip(td, _dots(td, pw))]
    between()
    q = _dots(td, no)
    q2 = _dots(q, q)
    between()
    z = [eye + x for x in q]
    z = [t + u for t, u in zip(z, _dots(z, q2))]
    return _dots(z, td)


def _wkv_chunk_body(chunks, r_all, ld_all, k_all, v_all_, a_all, b_all, g_ref, bonus_ref, lnw_ref, lnb_ref, hm_ref,
                    yr_ref, s_ref, h_ref, y_ref):
    C = WKV_CHUNK
    n_pairs = RWKV_WIDTH // PAIR

    @pl.when(pl.program_id(1) == 0)
    def _():
        h_ref[...] = jnp.zeros_like(h_ref)

    ri = lax.broadcasted_iota(jnp.int32, (PAIR, PAIR), 0)
    ci = lax.broadcasted_iota(jnp.int32, (PAIR, PAIR), 1)
    same_head = (ri // C) == (ci // C)
    strict_lower = same_head & (ci < ri)
    incl_lower = same_head & (ci <= ri)
    same16 = (ri // 16) == (ci // 16)
    eye_b = ri == ci
    eye = jnp.where(eye_b, 1.0, 0.0)
    tri = jnp.where(lax.broadcasted_iota(jnp.int32, (C, C), 1) <= lax.broadcasted_iota(jnp.int32, (C, C), 0),
                    1.0, 0.0).astype(BF16)
    lane_head0 = lax.broadcasted_iota(jnp.int32, (C, PAIR), 1) < HEAD_DIM
    zeros = jnp.zeros((PAIR, PAIR), F32)

    cat0 = lambda x, y: jnp.concatenate([x, y], axis=0)
    cat1 = lambda x, y: jnp.concatenate([x, y], axis=1)

    xa, xr, v_st, bh_st, kh_st, yb, yk, p_all = [], [], [], [], [], [], [], []
    for c in range(chunks):
        rows = slice(c * C, (c + 1) * C)
        ld = ld_all[rows, :]
        cum = _select_dot(ld, tri, 3, sel_left=True)
        last = cum[C - 1:C, :]
        p_inv = jnp.exp(-cum)
        p_tail = jnp.exp(last - cum)
        p_end = jnp.exp(last)
        a_t = a_all[rows, :] * jnp.exp(cum - ld)
        r_t = r_all[rows, :] * jnp.exp(cum)
        b_raw = b_all[rows, :]
        k_raw = k_all[rows, :]
        b_t = b_raw * p_inv
        k_t = k_raw * p_inv
        b_h = b_raw * p_tail
        k_h = k_raw * p_tail
        v_all = v_all_[rows, :]
        for j in range(n_pairs):
            lanes = slice(j * PAIR, (j + 1) * PAIR)
            xa.append(_stack_heads(a_t[:, lanes], lane_head0))
            xr.append(_stack_heads(r_t[:, lanes], lane_head0))
            v_st.append(_stack_heads(v_all[:, lanes], lane_head0))
            bh_st.append(_stack_heads(b_h[:, lanes], lane_head0))
            kh_st.append(_stack_heads(k_h[:, lanes], lane_head0))
            yb.append(cat0(b_t[:, lanes], b_t[:, lanes]))
            yk.append(cat0(k_t[:, lanes], k_t[:, lanes]))
            p_all.append(p_end[:, lanes])

    gram = _dots([cat0(x, y) for x, y in zip(xa, xr)], [cat0(x, y) for x, y in zip(yb, yk)], NT)
    n_mat = [jnp.where(strict_lower, g[0:PAIR, 0:PAIR], 0.0) for g in gram]
    m_mat = [jnp.where(strict_lower, g[0:PAIR, PAIR:2 * PAIR], 0.0) for g in gram]
    a_rbk = [cat1(jnp.where(incl_lower, g[PAIR:2 * PAIR, 0:PAIR], 0.0),
                  jnp.where(incl_lower, g[PAIR:2 * PAIR, PAIR:2 * PAIR], 0.0)) for g in gram]
    t_inv = _unit_lower_inverse(n_mat, same16, eye)
    mv = _dots(m_mat, v_st)
    tx = _dots(t_inv, [cat1(x, y) for x, y in zip(xa, mv)])
    rhs = [cat0(t, cat1(zeros, v)) for t, v in zip(tx, v_st)]
    ry = _dots(a_rbk, rhs)
    pp = _dots([cat0(x, y) for x, y in zip(bh_st, kh_st)], rhs, TN)

    for c in range(chunks):
        us = [c * n_pairs + j for j in range(n_pairs)]
        h0 = [h_ref[j] for j in range(n_pairs)]
        r_bar = [_fold_heads(xr[u] + ry[u][:, 0:PAIR]) for u in us]
        phi = [pp[u][:, 0:PAIR] + jnp.where(eye_b, p_all[u], 0.0) for u in us]
        ys = _dots(r_bar, h0)
        hs = _dots(phi, h0)
        for j, u in enumerate(us):
            y_ref[c * C:(c + 1) * C, j * PAIR:(j + 1) * PAIR] = ys[j] + _fold_heads(ry[u][:, PAIR:2 * PAIR])
            h_ref[j] = hs[j] + pp[u][:, PAIR:2 * PAIR]

    yr_ref[...] = _rwkv_post(y_ref[...], g_ref[...], bonus_ref[...], lnw_ref[...], lnb_ref[...], hm_ref[...])

    @pl.when(pl.program_id(1) == pl.num_programs(1) - 1)
    def _():
        s_ref[...] = h_ref[...]


def _wkv_prompt(feats, lw, batch, seq):
    N, W = feats[0].shape
    chunks = 4
    tt = chunks * WKV_CHUNK
    steps = seq // tt
    n_pairs = W // PAIR
    row = lambda bb, t: (bb * steps + t, 0)
    fixed = lambda bb, t: (0, 0)
    spec = pl.BlockSpec((tt, W), row)
    vec = pl.BlockSpec((1, W), fixed)
    return pl.pallas_call(
        functools.partial(_wkv_chunk_body, chunks),
        out_shape=(jax.ShapeDtypeStruct((N, W), F32), jax.ShapeDtypeStruct((batch, n_pairs, PAIR, PAIR), F32)),
        grid=(batch, steps),
        in_specs=[spec] * 8 + [vec, vec, pl.BlockSpec((W, W), fixed)],
        out_specs=(spec, pl.BlockSpec((None, n_pairs, PAIR, PAIR), lambda bb, t: (bb, 0, 0, 0))),
        scratch_shapes=[pltpu.VMEM((n_pairs, PAIR, PAIR), F32), pltpu.VMEM((tt, W), F32)],
        compiler_params=_params(("arbitrary", "arbitrary")),
        name="wkv_chunk_scan",
    )(*feats, *[lw[k] for k in POST_KEYS])


def _wkv_step_body(s_ref, r_ref, ld_ref, k_ref, v_ref, a_ref, b_ref, g_ref, bonus_ref, lnw_ref, lnb_ref, hm_ref,
                   exp_ref, red_ref, y_ref, so_ref):
    HD = HEAD_DIM
    nrep = HD * HD // LANES
    lane = lax.broadcasted_iota(jnp.int32, r_ref.shape, 1)
    low = lane < HD
    y = jnp.zeros(r_ref.shape, F32)
    for hh in range(2):
        def tiled(ref, fn=None):
            x = ref[...]
            if fn is not None:
                x = fn(x)
            sw = pltpu.roll(x, HD, 1)
            both = jnp.where(low, x, sw) if hh == 0 else jnp.where(low, sw, x)
            return jnp.tile(both, (1, nrep))
        cols = slice(hh * HD * HD, (hh + 1) * HD * HD)
        s = s_ref[:, cols]
        expand = exp_ref[hh]
        reduce_ = red_ref[hh]
        sa = _select_dot(s * tiled(a_ref), reduce_)
        s_new = (s * tiled(ld_ref, jnp.exp) + _select_dot(sa, expand) * tiled(b_ref)
                 + _select_dot(v_ref[...], expand) * tiled(k_ref))
        so_ref[:, cols] = s_new
        y = y + _select_dot(s_new * tiled(r_ref), reduce_)
    y_ref[...] = _rwkv_post(y, g_ref[...], bonus_ref[...], lnw_ref[...], lnb_ref[...], hm_ref[...])


def _wkv_step_consts():
    HD = HEAD_DIM
    expand = np.zeros((2, PAIR, HD * HD), np.float32)
    for hh in range(2):
        for vv in range(HD):
            expand[hh, hh * HD + vv, vv * HD:(vv + 1) * HD] = 1.0
    return jnp.asarray(expand, BF16), jnp.asarray(expand.transpose(0, 2, 1), BF16)


def _wkv_sample(state, r, ld, k, v, a, b, g, bonus, lw):
    B = state.shape[0]
    W = RWKV_WIDTH
    HD2 = HEAD_DIM * HEAD_DIM
    expand, reduce_ = _wkv_step_consts()
    s2 = state.reshape(B, RWKV_HEADS * HD2)
    st_spec = pl.BlockSpec((B, 2 * HD2), lambda j: (0, j))
    vec = pl.BlockSpec((B, PAIR), lambda j: (0, j))
    y, s_new = pl.pallas_call(
        _wkv_step_body,
        out_shape=(jax.ShapeDtypeStruct((B, W), F32), jax.ShapeDtypeStruct((B, RWKV_HEADS * HD2), F32)),
        grid=(W // PAIR,),
        in_specs=[st_spec] + [vec] * 8 + [pl.BlockSpec((1, PAIR), lambda j: (0, j))] * 2
                 + [pl.BlockSpec((PAIR, PAIR), lambda j: (j, j)), pl.BlockSpec((2, PAIR, HD2), lambda j: (0, 0, 0)),
                    pl.BlockSpec((2, HD2, PAIR), lambda j: (0, 0, 0))],
        out_specs=(vec, st_spec),
        compiler_params=_params(("arbitrary",)),
        name="wkv_step",
    )(s2, r, ld, k, v, a, b, g, bonus, lw["lnw"], lw["lnb"], lw["headmean"], expand, reduce_)
    return y, s_new.reshape(state.shape)


def _q_perm():
    return np.array([(h * SWA_GROUP + g) * HEAD_DIM + d for g in range(SWA_GROUP) for h in range(SWA_KV_HEADS)
                     for d in range(HEAD_DIM)], np.int32)


def _rel_bucket_np(dist):
    max_exact = N_BUCKETS // 2
    d = np.maximum(dist, 0)
    ratio = np.log(np.maximum(d, 1).astype(np.float32) / np.float32(max_exact)) / np.float32(
        np.log(MAX_DISTANCE / max_exact))
    large = np.minimum(max_exact + (ratio.astype(np.float32) * np.float32(N_BUCKETS - max_exact)).astype(np.int32),
                       N_BUCKETS - 1)
    return np.where(d < max_exact, d, large).astype(np.int32)


def _bias_from_buckets(idx, rb_ref, head):
    acc = jnp.zeros(idx.shape, F32)
    for bk in range(N_BUCKETS):
        acc = jnp.where(idx == bk, rb_ref[bk, head], acc)
    return acc


def _swa_prompt_body(rb_ref, sink_ref, idx_ref, q_ref, kp_ref, kc_ref, vp_ref, vc_ref, o_ref, bias_ref):
    Q = WINDOW
    first = (pl.program_id(0) == 0) & (pl.program_id(1) == 0)

    @pl.when(first)
    def _():
        idx = idx_ref[...]
        for g in range(SWA_GROUP):
            for h in range(SWA_KV_HEADS):
                bias_ref[g * SWA_KV_HEADS + h] = _bias_from_buckets(idx, rb_ref, h * SWA_GROUP + g)

    n = pl.program_id(1)
    qi = lax.broadcasted_iota(jnp.int32, (Q, 2 * Q), 0)
    kj = lax.broadcasted_iota(jnp.int32, (Q, 2 * Q), 1)
    valid = ((kj < Q) & (kj >= qi) & (n > 0)) | ((kj >= Q) & ((kj - Q) <= qi))
    lane_kv0 = lax.broadcasted_iota(jnp.int32, (Q, LANES), 1) < HEAD_DIM
    kcat = jnp.concatenate([kp_ref[...], kc_ref[...]], axis=0).astype(BF16)
    vcat = jnp.concatenate([vp_ref[...], vc_ref[...]], axis=0).astype(BF16)
    scale = HEAD_DIM ** -0.5
    for g in range(SWA_GROUP):
        qg = q_ref[:, g * LANES:(g + 1) * LANES]
        outs = []
        for h in range(SWA_KV_HEADS):
            qm = jnp.where(lane_kv0, qg, 0.0) if h == 0 else jnp.where(lane_kv0, 0.0, qg)
            s = _dot(qm, kcat, NT) * scale + bias_ref[g * SWA_KV_HEADS + h]
            s = jnp.where(valid, s, NEG_INF)
            sink = sink_ref[h * SWA_GROUP + g]
            m = jnp.maximum(jnp.max(s, axis=-1, keepdims=True), sink)
            p = jnp.exp(s - m)
            den = jnp.sum(p, axis=-1, keepdims=True) + jnp.exp(sink - m)
            outs.append(_dot(p, vcat) / den)
        o_ref[:, g * LANES:(g + 1) * LANES] = jnp.where(lane_kv0, outs[0], outs[1])


def _swa_prompt(q, k, v, rel_bias, sinks, batch, seq):
    N = q.shape[0]
    Q = WINDOW
    nb = seq // Q
    qi = np.arange(Q)[:, None]
    kj = np.arange(2 * Q)[None, :]
    idx = jnp.asarray(_rel_bucket_np(qi + Q - kj))
    cur = lambda bb, n: (bb * nb + n, 0)
    prev = lambda bb, n: (bb * nb + jnp.maximum(n - 1, 0), 0)
    kv_c = pl.BlockSpec((Q, KV_WIDTH), cur)
    kv_p = pl.BlockSpec((Q, KV_WIDTH), prev)
    smem = pl.BlockSpec(memory_space=pltpu.SMEM)
    return pl.pallas_call(
        _swa_prompt_body,
        out_shape=jax.ShapeDtypeStruct((N, SWA_WIDTH), F32),
        grid=(batch, nb),
        in_specs=[smem, smem, pl.BlockSpec((Q, 2 * Q), lambda bb, n: (0, 0)),
                  pl.BlockSpec((Q, SWA_WIDTH), cur), kv_p, kv_c, kv_p, kv_c],
        out_specs=pl.BlockSpec((Q, SWA_WIDTH), cur),
        scratch_shapes=[pltpu.VMEM((SWA_HEADS, Q, 2 * Q), F32)],
        compiler_params=_params(("arbitrary", "arbitrary")),
        name="swa_prompt",
    )(rel_bias, sinks, idx, q, k, k, v, v)


def _swa_sample_body(rb_ref, sink_ref, idx_ref, q_ref, kn_ref, vn_ref, kb_ref, vb_ref, o_ref, ko_ref, vo_ref,
                     bias_ref, extra_ref):
    W = WINDOW
    G, KVH = SWA_GROUP, SWA_KV_HEADS

    @pl.when(pl.program_id(0) == 0)
    def _():
        idx = jnp.broadcast_to(idx_ref[...], (SUBLANES, W))
        row = lax.broadcasted_iota(jnp.int32, (SUBLANES, W), 0)
        acc = jnp.zeros((SUBLANES, W), F32)
        ext = jnp.zeros((SUBLANES, LANES), F32)
        lane = lax.broadcasted_iota(jnp.int32, (SUBLANES, LANES), 1)
        for h in range(KVH):
            for g in range(G):
                head = h * G + g
                r = h * G + g
                acc = jnp.where(row == r, _bias_from_buckets(idx, rb_ref, head), acc)
                ext = jnp.where((row == r) & (lane == 0), rb_ref[0, head], ext)
                ext = jnp.where((row == r) & (lane == 1), sink_ref[head], ext)
        bias_ref[...] = acc
        extra_ref[...] = ext

    TB = q_ref.shape[0]
    lane_kv0 = lax.broadcasted_iota(jnp.int32, (TB, G, LANES), 2) < HEAD_DIM
    q4 = q_ref[...]
    qrows = jnp.concatenate([jnp.where(lane_kv0, q4, 0.0), jnp.where(lane_kv0, 0.0, q4)], axis=1)
    kb = kb_ref[...]
    vb = vb_ref[...]
    kn = kn_ref[...]
    vn = vn_ref[...]
    scale = HEAD_DIM ** -0.5
    bdims = (((2,), (2,)), ((0,), (0,)))
    s = lax.dot_general(qrows.astype(BF16), kb.astype(BF16), bdims, preferred_element_type=F32) * scale
    s = s + bias_ref[...][None]
    s_self = jnp.sum(qrows * kn, axis=-1, keepdims=True) * scale + extra_ref[:, 0:1][None]
    sink = extra_ref[:, 1:2][None]
    m = jnp.maximum(jnp.maximum(jnp.max(s, axis=-1, keepdims=True), s_self), sink)
    p = jnp.exp(s - m)
    p_self = jnp.exp(s_self - m)
    den = jnp.sum(p, axis=-1, keepdims=True) + p_self + jnp.exp(sink - m)
    pv = lax.dot_general(p.astype(BF16), vb.astype(BF16), (((2,), (1,)), ((0,), (0,))), preferred_element_type=F32)
    o = (pv + p_self * vn) / den
    o_ref[...] = jnp.where(lane_kv0, o[:, 0:G], o[:, G:2 * G])

    rowmod = lax.broadcasted_iota(jnp.int32, (TB, W, KV_WIDTH), 1)
    for buf, new, out in ((kb, kn, ko_ref), (vb, vn, vo_ref)):
        rolled = pltpu.roll(buf.reshape(TB * W, KV_WIDTH), TB * W - 1, 0).reshape(TB, W, KV_WIDTH)
        out[...] = jnp.where(rowmod == W - 1, new, rolled)


def _swa_sample(q, k, v, kbuf, vbuf, rel_bias, sinks):
    B = q.shape[0]
    W = WINDOW
    tb = 16
    idx = jnp.asarray(_rel_bucket_np(W - np.arange(W))[None, :])
    q4 = q.reshape(B, SWA_GROUP, LANES)
    kn = k.reshape(B, 1, KV_WIDTH)
    vn = v.reshape(B, 1, KV_WIDTH)
    kb = kbuf.reshape(B, W, KV_WIDTH)
    vb = vbuf.reshape(B, W, KV_WIDTH)
    smem = pl.BlockSpec(memory_space=pltpu.SMEM)
    b3 = lambda i: (i, 0, 0)
    o, ko, vo = pl.pallas_call(
        _swa_sample_body,
        out_shape=(jax.ShapeDtypeStruct((B, SWA_GROUP, LANES), F32), jax.ShapeDtypeStruct((B, W, KV_WIDTH), F32),
                   jax.ShapeDtypeStruct((B, W, KV_WIDTH), F32)),
        grid=(B // tb,),
        in_specs=[smem, smem, pl.BlockSpec((1, W), lambda i: (0, 0)),
                  pl.BlockSpec((tb, SWA_GROUP, LANES), b3), pl.BlockSpec((tb, 1, KV_WIDTH), b3),
                  pl.BlockSpec((tb, 1, KV_WIDTH), b3), pl.BlockSpec((tb, W, KV_WIDTH), b3),
                  pl.BlockSpec((tb, W, KV_WIDTH), b3)],
        out_specs=(pl.BlockSpec((tb, SWA_GROUP, LANES), b3), pl.BlockSpec((tb, W, KV_WIDTH), b3),
                   pl.BlockSpec((tb, W, KV_WIDTH), b3)),
        scratch_shapes=[pltpu.VMEM((SUBLANES, W), F32), pltpu.VMEM((SUBLANES, LANES), F32)],
        compiler_params=_params(("arbitrary",)),
        name="swa_sample",
    )(rel_bias, sinks, idx, q4, kn, vn, kb, vb)
    return o.reshape(B, SWA_WIDTH), ko.reshape(kbuf.shape), vo.reshape(vbuf.shape)


def _outproj_body(per_row, tiles_per_seq, prompt_row0, x_ref, mod_ref, yr_ref, ya_ref, wr_ref, wa_ref, gpost_ref, o_ref):
    D = D_MODEL
    m = _mod_rows(mod_ref, per_row, tiles_per_seq, prompt_row0)
    mix = _dot(yr_ref[...], wr_ref[...]) + _dot(ya_ref[...], wa_ref[...])
    o_ref[...] = x_ref[...] + m[:, 2 * D:3 * D] * _rms(mix, gpost_ref[...])


def _outproj(x, mod, yr, ya, lw, per_row, tm, tiles_per_seq, prompt_row0):
    N, D = x.shape
    W = RWKV_WIDTH
    row = lambda i: (i, 0)
    fixed = lambda i: (0, 0)
    half = pl.BlockSpec((tm, W), row)
    return pl.pallas_call(
        functools.partial(_outproj_body, per_row, tiles_per_seq, prompt_row0),
        out_shape=jax.ShapeDtypeStruct((N, D), F32),
        grid=(N // tm,),
        in_specs=[pl.BlockSpec((tm, D), row), _mod_spec(mod, per_row, tm, 1), half, half,
                  pl.BlockSpec((W, D), fixed), pl.BlockSpec((W, D), fixed), pl.BlockSpec((1, D), fixed)],
        out_specs=pl.BlockSpec((tm, D), row),
        compiler_params=_params(("arbitrary",)),
        name="out_proj",
    )(x, mod, yr, ya, lw["w_out_r"], lw["w_out_a"], lw["g_post0"])


def _ffn_body(moe, per_row, tiles_per_seq, prompt_row0, x_ref, mod_ref, gpre_ref, gpost_ref, rw_ref, rb_ref,
              wg_ref, wu_ref, wd_ref, o_ref, h_ref, acc_ref, comb_ref):
    D = D_MODEL
    e = pl.program_id(1)
    f = pl.program_id(2)
    first = (e == 0) & (f == 0)
    last = (e == pl.num_programs(1) - 1) & (f == pl.num_programs(2) - 1)

    @pl.when(first)
    def _():
        m = _mod_rows(mod_ref, per_row, tiles_per_seq, prompt_row0)
        h = _rms(x_ref[...], gpre_ref[...]) * (1.0 + m[:, D:2 * D]) + m[:, 0:D]
        h_ref[...] = h.astype(BF16)
        acc_ref[...] = jnp.zeros_like(acc_ref)
        if moe:
            logits = _dot3(h, rw_ref[...]) + rb_ref[...]
            lane = lax.broadcasted_iota(jnp.int32, logits.shape, 1)
            m1 = jnp.max(logits, axis=-1, keepdims=True)
            i1 = jnp.min(jnp.where(logits == m1, lane, LANES), axis=-1, keepdims=True)
            rest = jnp.where(lane == i1, -jnp.inf, logits)
            m2 = jnp.max(rest, axis=-1, keepdims=True)
            i2 = jnp.min(jnp.where(rest == m2, lane, LANES), axis=-1, keepdims=True)
            e2 = jnp.exp(m2 - m1)
            comb_ref[...] = jnp.where(lane == i1, 1.0 / (1.0 + e2), 0.0) + jnp.where(lane == i2, e2 / (1.0 + e2), 0.0)

    h = h_ref[...]
    gate = _dot(h, wg_ref[...])
    up = _dot(h, wu_ref[...])
    act = gate * _sigmoid(gate) * up
    if moe:
        comb = comb_ref[...]
        lane = lax.broadcasted_iota(jnp.int32, comb.shape, 1)
        act = act * jnp.sum(jnp.where(lane == e, comb, 0.0), axis=-1, keepdims=True)
    acc_ref[...] += _dot(act, wd_ref[...])

    @pl.when(last)
    def _():
        m = _mod_rows(mod_ref, per_row, tiles_per_seq, prompt_row0)
        o_ref[...] = x_ref[...] + m[:, 2 * D:3 * D] * _rms(acc_ref[...], gpost_ref[...])


def _ffn(x, mod, gpre, gpost, router_w, router_b, wg, wu, wd, w0, E, moe, per_row, tm, tiles_per_seq, prompt_row0,
         tf):
    N, D = x.shape
    F = wg.shape[-1]
    row = lambda i, e, f: (i, 0)
    fixed = lambda i, e, f: (0, 0)
    return pl.pallas_call(
        functools.partial(_ffn_body, moe, per_row, tiles_per_seq, prompt_row0),
        out_shape=jax.ShapeDtypeStruct((N, D), F32),
        grid=(N // tm, E, F // tf),
        in_specs=[pl.BlockSpec((tm, D), row), _mod_spec(mod, per_row, tm, 3),
                  pl.BlockSpec((1, D), fixed), pl.BlockSpec((1, D), fixed),
                  pl.BlockSpec((D, LANES), fixed), pl.BlockSpec((1, LANES), fixed),
                  pl.BlockSpec((None, D, tf), lambda i, e, f: (w0 + e, 0, f)),
                  pl.BlockSpec((None, D, tf), lambda i, e, f: (w0 + e, 0, f)),
                  pl.BlockSpec((None, tf, D), lambda i, e, f: (w0 + e, f, 0))],
        out_specs=pl.BlockSpec((tm, D), row),
        scratch_shapes=[pltpu.VMEM((tm, D), BF16), pltpu.VMEM((tm, D), F32), pltpu.VMEM((tm, LANES), F32)],
        compiler_params=_params(("arbitrary", "arbitrary", "arbitrary")),
        name="moe_ffn" if moe else "dense_ffn",
    )(x, mod, gpre, gpost, router_w, router_b, wg, wu, wd)


ROW_TILE = D_MODEL // LANES
EXPERT_TILE = 1024
EXPERT_F_TILE = 512
INFO_E1, INFO_E2, INFO_P1, INFO_P2, INFO_POS1, INFO_POS2 = range(6)


def _top2(logits):
    lane = lax.broadcasted_iota(jnp.int32, logits.shape, 1)
    m1 = jnp.max(logits, axis=-1, keepdims=True)
    i1 = jnp.min(jnp.where(logits == m1, lane, LANES), axis=-1, keepdims=True)
    rest = jnp.where(lane == i1, -jnp.inf, logits)
    m2 = jnp.max(rest, axis=-1, keepdims=True)
    i2 = jnp.min(jnp.where(rest == m2, lane, LANES), axis=-1, keepdims=True)
    e2 = jnp.exp(m2 - m1)
    return lane, i1, i2, 1.0 / (1.0 + e2), e2 / (1.0 + e2)


def _to_row_tiles(ref, x):
    rows = x.shape[0]
    for c in range(ROW_TILE):
        ref[pl.ds(c, rows, stride=ROW_TILE), :] = x[:, c * LANES:(c + 1) * LANES]


def _from_row_tiles(ref, row0, rows):
    return jnp.concatenate([ref[pl.ds(row0 * ROW_TILE + c, rows, stride=ROW_TILE), :] for c in range(ROW_TILE)],
                           axis=1)


def _route_body(tiles_per_seq, prompt_row0, x_ref, mod_ref, gpre_ref, rw_ref, rb_ref, h_ref, info_ref, cnt_ref,
                base_ref):
    D = D_MODEL

    @pl.when(pl.program_id(0) == 0)
    def _():
        base_ref[...] = jnp.zeros_like(base_ref)

    m = _mod_rows(mod_ref, False, tiles_per_seq, prompt_row0)
    h = _rms(x_ref[...], gpre_ref[...]) * (1.0 + m[:, D:2 * D]) + m[:, 0:D]
    _to_row_tiles(h_ref, h)
    lane, i1, i2, p1, p2 = _top2(_dot3(h, rw_ref[...]) + rb_ref[...])
    onehot = jnp.where((lane == i1) | (lane == i2), 1.0, 0.0)
    T = h.shape[0]
    before = lax.broadcasted_iota(jnp.int32, (T, T), 1) < lax.broadcasted_iota(jnp.int32, (T, T), 0)
    rank = _dot(jnp.where(before, 1.0, 0.0), onehot) + base_ref[0:1, :]
    pos1 = jnp.sum(jnp.where(lane == i1, rank, 0.0), axis=-1, keepdims=True)
    pos2 = jnp.sum(jnp.where(lane == i2, rank, 0.0), axis=-1, keepdims=True)
    info = jnp.zeros(onehot.shape, F32)
    for col, val in ((INFO_E1, i1.astype(F32)), (INFO_E2, i2.astype(F32)), (INFO_P1, p1), (INFO_P2, p2),
                     (INFO_POS1, pos1), (INFO_POS2, pos2)):
        info = jnp.where(lane == col, val, info)
    info_ref[...] = info
    total = base_ref[...] + jnp.sum(onehot, axis=0, keepdims=True)
    base_ref[...] = total
    cnt_ref[...] = total


def _moe_route(x, mod, gpre, rw, rb, tm, tiles_per_seq, prompt_row0):
    N, D = x.shape
    row = lambda i: (i, 0)
    fixed = lambda i: (0, 0)
    return pl.pallas_call(
        functools.partial(_route_body, tiles_per_seq, prompt_row0),
        out_shape=(jax.ShapeDtypeStruct((N * ROW_TILE, LANES), F32), jax.ShapeDtypeStruct((N, LANES), F32),
                   jax.ShapeDtypeStruct((SUBLANES, LANES), F32)),
        grid=(N // tm,),
        in_specs=[pl.BlockSpec((tm, D), row), _mod_spec(mod, False, tm, 1), pl.BlockSpec((1, D), fixed),
                  pl.BlockSpec((D, LANES), fixed), pl.BlockSpec((1, LANES), fixed)],
        out_specs=(pl.BlockSpec((tm * ROW_TILE, LANES), row), pl.BlockSpec((tm, LANES), row),
                   pl.BlockSpec((SUBLANES, LANES), fixed)),
        scratch_shapes=[pltpu.VMEM((SUBLANES, LANES), F32)],
        compiler_params=_params(("arbitrary",)),
        name="moe_route",
    )(x, mod, gpre, rw, rb)


def _row_copy(src_hbm, dst_vmem, sem, src_row, dst_row):
    return pltpu.make_async_copy(src_hbm.at[pl.ds(pl.multiple_of(src_row * ROW_TILE, ROW_TILE), ROW_TILE)],
                                 dst_vmem.at[pl.ds(pl.multiple_of(dst_row * ROW_TILE, ROW_TILE), ROW_TILE)], sem)


def _start_rows(idx_ref, idx0, src_hbm, dst_vmem, sem, rows):
    def body(r, carry):
        _row_copy(src_hbm, dst_vmem, sem, idx_ref[idx0 + r], r).start()
        return carry
    lax.fori_loop(0, rows, body, 0, unroll=16)


def _wait_rows(src_hbm, dst_vmem, sem, rows):
    pltpu.make_async_copy(src_hbm.at[pl.ds(0, rows * ROW_TILE)], dst_vmem.at[pl.ds(0, rows * ROW_TILE)], sem).wait()


def _experts_body(te_ref, nv_ref, src_ref, h_hbm, wg_ref, wu_ref, wd_ref, o_ref, xbuf, hb_ref, acc_ref, sem):
    TM = EXPERT_TILE
    i = pl.program_id(0)
    f = pl.program_id(1)
    slot = i % 2
    n_valid = nv_ref[0]
    valid = i < n_valid

    @pl.when(f == 0)
    def _():
        @pl.when((i == 0) & valid)
        def _():
            _start_rows(src_ref, 0, h_hbm, xbuf.at[0], sem.at[0], TM)

        @pl.when(i + 1 < n_valid)
        def _():
            _start_rows(src_ref, (i + 1) * TM, h_hbm, xbuf.at[1 - slot], sem.at[1 - slot], TM)

        @pl.when(valid)
        def _():
            _wait_rows(h_hbm, xbuf.at[slot], sem.at[slot], TM)
            hb_ref[...] = _from_row_tiles(xbuf.at[slot], 0, TM).astype(BF16)
            acc_ref[...] = jnp.zeros_like(acc_ref)

    @pl.when(valid)
    def _():
        h = hb_ref[...]
        gate = _dot(h, wg_ref[...])
        up = _dot(h, wu_ref[...])
        acc_ref[...] += _dot(gate * _sigmoid(gate) * up, wd_ref[...])

    @pl.when(f == pl.num_programs(1) - 1)
    def _():
        @pl.when(valid)
        def _():
            _to_row_tiles(o_ref, acc_ref[...])

        @pl.when(jnp.logical_not(valid))
        def _():
            o_ref[...] = jnp.zeros_like(o_ref)


def _moe_experts(h_rows, tile_expert, n_valid, src_tok, wg, wu, wd, tf):
    TM = EXPERT_TILE
    P = src_tok.shape[0]
    _, D, F = wg.shape
    grid_spec = pltpu.PrefetchScalarGridSpec(
        num_scalar_prefetch=3,
        grid=(P // TM, F // tf),
        in_specs=[pl.BlockSpec(memory_space=pl.ANY),
                  pl.BlockSpec((None, D, tf), lambda i, f, te, nv, src: (te[i], 0, f)),
                  pl.BlockSpec((None, D, tf), lambda i, f, te, nv, src: (te[i], 0, f)),
                  pl.BlockSpec((None, tf, D), lambda i, f, te, nv, src: (te[i], f, 0))],
        out_specs=pl.BlockSpec((TM * ROW_TILE, LANES), lambda i, f, te, nv, src: (i, 0)),
        scratch_shapes=[pltpu.VMEM((2, TM * ROW_TILE, LANES), F32), pltpu.VMEM((TM, D), BF16),
                        pltpu.VMEM((TM, D), F32), pltpu.SemaphoreType.DMA((2,))],
    )
    return pl.pallas_call(
        _experts_body,
        out_shape=jax.ShapeDtypeStruct((P * ROW_TILE, LANES), F32),
        grid_spec=grid_spec,
        compiler_params=_params(("arbitrary", "arbitrary")),
        name="moe_experts",
    )(tile_expert, n_valid, src_tok, h_rows, wg, wu, wd)


def _combine_body(tiles_per_seq, prompt_row0, dest_ref, f_hbm, x_ref, mod_ref, info_ref, gpost_ref, o_ref, gbuf, sem):
    D = D_MODEL
    i = pl.program_id(0)
    T = x_ref.shape[0]
    slot = i % 2

    @pl.when(i == 0)
    def _():
        _start_rows(dest_ref, 0, f_hbm, gbuf.at[0], sem.at[0], 2 * T)

    @pl.when(i + 1 < pl.num_programs(0))
    def _():
        _start_rows(dest_ref, (i + 1) * 2 * T, f_hbm, gbuf.at[1 - slot], sem.at[1 - slot], 2 * T)

    _wait_rows(f_hbm, gbuf.at[slot], sem.at[slot], 2 * T)
    info = info_ref[...]
    f1 = _from_row_tiles(gbuf.at[slot], 0, T)
    f2 = _from_row_tiles(gbuf.at[slot], T, T)
    y = info[:, INFO_P1:INFO_P1 + 1] * f1 + info[:, INFO_P2:INFO_P2 + 1] * f2
    m = _mod_rows(mod_ref, False, tiles_per_seq, prompt_row0)
    o_ref[...] = x_ref[...] + m[:, 2 * D:3 * D] * _rms(y, gpost_ref[...])


def _moe_combine(dest, f_rows, x, mod, info, gpost, tm, tiles_per_seq, prompt_row0):
    N, D = x.shape
    row = lambda i, d: (i, 0)
    fixed = lambda i, d: (0, 0)
    grid_spec = pltpu.PrefetchScalarGridSpec(
        num_scalar_prefetch=1,
        grid=(N // tm,),
        in_specs=[pl.BlockSpec(memory_space=pl.ANY), pl.BlockSpec((tm, D), row),
                  pl.BlockSpec(mod.shape, fixed), pl.BlockSpec((tm, LANES), row), pl.BlockSpec((1, D), fixed)],
        out_specs=pl.BlockSpec((tm, D), row),
        scratch_shapes=[pltpu.VMEM((2, 2 * tm * ROW_TILE, LANES), F32), pltpu.SemaphoreType.DMA((2,))],
    )
    return pl.pallas_call(
        functools.partial(_combine_body, tiles_per_seq, prompt_row0),
        out_shape=jax.ShapeDtypeStruct((N, D), F32),
        grid_spec=grid_spec,
        compiler_params=_params(("arbitrary",)),
        name="moe_combine",
    )(dest, f_rows, x, mod, info, gpost)


def _moe_prompt(x, mod, gpre, gpost, fw, tm, tiles_per_seq, prompt_row0, tf):
    N = x.shape[0]
    TM = EXPERT_TILE
    n_tiles = (2 * N) // TM + N_EXPERTS
    h_rows, info, cnt = _moe_route(x, mod, gpre, fw["rw"], fw["rb"], tm, tiles_per_seq, prompt_row0)
    e1 = info[:, INFO_E1].astype(jnp.int32)
    e2 = info[:, INFO_E2].astype(jnp.int32)
    counts = cnt[0, :N_EXPERTS].astype(jnp.int32)
    padded = ((counts + TM - 1) // TM) * TM
    ends = jnp.cumsum(padded)
    starts = ends - padded
    dest1 = starts[e1] + info[:, INFO_POS1].astype(jnp.int32)
    dest2 = starts[e2] + info[:, INFO_POS2].astype(jnp.int32)
    tok = jnp.arange(N, dtype=jnp.int32)
    src_tok = jnp.zeros((n_tiles * TM,), jnp.int32).at[jnp.concatenate([dest1, dest2])].set(
        jnp.concatenate([tok, tok]), unique_indices=True)
    tile_start = jnp.arange(n_tiles, dtype=jnp.int32) * TM
    tile_expert = jnp.minimum(jnp.sum(ends[None, :] <= tile_start[:, None], axis=1), N_EXPERTS - 1).astype(jnp.int32)
    n_valid = (ends[-1:] // TM).astype(jnp.int32)
    f_rows = _moe_experts(h_rows, tile_expert + fw["w0"], n_valid, src_tok, fw["wg"], fw["wu"], fw["wd"],
                          EXPERT_F_TILE)
    dest = jnp.concatenate([dest1.reshape(-1, 1, tm), dest2.reshape(-1, 1, tm)], axis=1).reshape(-1)
    return _moe_combine(dest, f_rows, x, mod, info, gpost, tm, tiles_per_seq, prompt_row0)


def _layer_weights(p, l):
    W = RWKV_WIDTH
    heads = np.arange(W) // HEAD_DIM
    headsum = jnp.asarray((heads[:, None] == heads[None, :]).astype(np.float32))
    perm = _q_perm()
    w_in = p["w_in"][l]
    w_in = jnp.concatenate([w_in[:, :RWKV_COLS], w_in[:, RWKV_COLS + perm], w_in[:, RWKV_COLS + SWA_WIDTH:]], axis=1)
    w_out = p["w_out"][l]
    zeros = jnp.zeros((DECAY_LORA, W), F32)
    row = lambda t: t.reshape(1, -1)
    return {
        "w_in": w_in.astype(BF16),
        "w_out_r": w_out[:W].astype(BF16),
        "w_out_a": w_out[W + perm].astype(BF16),
        "mu": row(p["mu_shift"][l]),
        "wd": jnp.concatenate([p["w_decay_up"][l], zeros], axis=0),
        "wi": jnp.concatenate([zeros, p["w_iclr_up"][l]], axis=0),
        "wg": p["w_gate_up"][l],
        "dbase": row(p["decay_base"][l]), "ibase": row(p["iclr_base"][l]),
        "kk": row(p["k_k"][l]), "ka": row(p["k_a"][l]), "rk": row(p["r_k"][l]),
        "lnw": row(p["lnx_w"][l]), "lnb": row(p["lnx_b"][l]),
        "headsum": headsum.astype(BF16), "headmean": (headsum / HEAD_DIM).astype(BF16),
        "g_pre0": row(p["norm_pre"][l, 0]), "g_pre1": row(p["norm_pre"][l, 1]),
        "g_post0": row(p["norm_post"][l, 0]), "g_post1": row(p["norm_post"][l, 1]),
    }


def _stacked(w, dtype):
    return w.astype(dtype).reshape((-1,) + w.shape[-2:])


def _ffn_weights(p, stacks, l):
    i = l // 2
    if l % 2 == 0:
        wg, wu, wd = stacks["dense"]
        return dict(moe=False, rw=jnp.zeros((D_MODEL, LANES), F32), rb=jnp.zeros((1, LANES), F32),
                    wg=wg, wu=wu, wd=wd, w0=i, n=1)
    rw = jnp.zeros((D_MODEL, LANES), F32).at[:, :N_EXPERTS].set(p["router_w"][i])
    rb = jnp.full((1, LANES), NEG_INF, F32).at[0, :N_EXPERTS].set(p["router_b"][i])
    wg, wu, wd = stacks["moe"]
    return dict(moe=True, rw=rw, rb=rb, wg=wg, wu=wu, wd=wd, w0=i * N_EXPERTS, n=N_EXPERTS)


def _pick_tile(n, pref):
    t = min(pref, n)
    while n % t:
        t //= 2
    return t


def _ffn_tile(f):
    for t in (1408, 896, 512, 256, 128):
        if f % t == 0:
            return t
    return f


def _trunk(x3, mods, lws, fws, p, prompt, prompt_row0, state=None):
    B, T, D = x3.shape
    N = B * T
    x = x3.reshape(N, D)
    per_row = not prompt
    tm = _pick_tile(T if prompt else N, 512)
    tps = (T // tm) if prompt else 1
    depth = len(lws)
    wkv_out, shift_out, k_out, v_out = [], [], [], []
    for l in range(depth):
        lw, fw = lws[l], fws[l]
        mod0, mod1 = mods[2 * l], mods[2 * l + 1]
        pr, q, k, v = _inproj(x, mod0, lw["g_pre0"], lw["w_in"], per_row, tm, tps, prompt_row0)
        if prompt:
            yr, hbd = _wkv_prompt(_prep_prompt(pr, lw, tm, tps), lw, B, T)
            n_pairs = RWKV_WIDTH // PAIR
            hb = hbd.reshape(B, n_pairs, 2, HEAD_DIM, 2, HEAD_DIM)
            s_kv = jnp.stack([hb[:, :, 0, :, 0, :], hb[:, :, 1, :, 1, :]], axis=2)
            s_new = jnp.swapaxes(s_kv.reshape(B, RWKV_HEADS, HEAD_DIM, HEAD_DIM), -1, -2)
            ya = _swa_prompt(q, k, v, p["rel_bias"], p["attn_sinks"][l], B, T)
            window = lambda t: t.reshape(B, T, KV_WIDTH)[:, -WINDOW:].reshape(B, WINDOW, SWA_KV_HEADS, HEAD_DIM)
            kb, vb = window(k), window(v)
            last = pr.reshape(B, T, RWKV_COLS)[:, -1]
        else:
            feats = _prep_sample(pr, state["shift"][l], lw)
            yr, s_new = _wkv_sample(state["wkv"][l], *feats, lw)
            ya, kb, vb = _swa_sample(q, k, v, state["k"][l], state["v"][l], p["rel_bias"], p["attn_sinks"][l])
            last = pr
        x = _outproj(x, mod0, yr, ya, lw, per_row, tm, tps, prompt_row0)
        tf = _ffn_tile(fw["wg"].shape[-1])
        if prompt and fw["moe"]:
            x = _moe_prompt(x, mod1, lw["g_pre1"], lw["g_post1"], fw, tm, tps, prompt_row0, tf)
        else:
            x = _ffn(x, mod1, lw["g_pre1"], lw["g_post1"], fw["rw"], fw["rb"], fw["wg"], fw["wu"], fw["wd"],
                     fw["w0"], fw["n"], fw["moe"], per_row, tm, tps, prompt_row0, tf)
        wkv_out.append(s_new)
        shift_out.append(last)
        k_out.append(kb)
        v_out.append(vb)
    return x.reshape(B, T, D), jnp.stack(wkv_out), jnp.stack(shift_out), jnp.stack(k_out), jnp.stack(v_out)


def _forward(x_prompt, x_sample, c_prompt, c_sample, state_wkv, state_shift, cache_swa_k, cache_swa_v, p):
    depth = p["w_in"].shape[0]
    Bp, Bs = c_prompt.shape[0], c_sample.shape[0]
    D = D_MODEL
    pad = (-(Bs + Bp)) % SUBLANES
    c_all = jnp.concatenate([c_sample, c_prompt, jnp.zeros((pad, D), F32)], axis=0)
    mods = _ada_all(c_all, p["ada_w"].reshape(2 * depth, D, 3 * D), p["ada_b"].reshape(2 * depth, 1, 3 * D))
    lws = [_layer_weights(p, l) for l in range(depth)]
    stacks = {"dense": tuple(_stacked(p[k], BF16) for k in ("ffn_w_gate", "ffn_w_up", "ffn_w_down")),
              "moe": tuple(_stacked(p[k], F32) for k in ("moe_w_gate", "moe_w_up", "moe_w_down"))}
    fws = [_ffn_weights(p, stacks, l) for l in range(depth)]
    y_p, wkv_p, shift_p, k_p, v_p = _trunk(x_prompt, mods, lws, fws, p, True, Bs)
    state = {"wkv": state_wkv, "shift": state_shift, "k": cache_swa_k, "v": cache_swa_v}
    y_s, wkv_s, shift_s, k_s, v_s = _trunk(x_sample, mods, lws, fws, p, False, Bs, state)
    return (y_p, y_s, wkv_p, shift_p, k_p, v_p, wkv_s, shift_s, k_s, v_s)


def kernel(x_prompt, x_sample, c_prompt, c_sample, state_wkv, state_shift, cache_swa_k, cache_swa_v, rel_bias, ada_w, ada_b, norm_pre, norm_post, w_in, mu_shift, w_decay_up, decay_base, w_iclr_up, iclr_base, w_gate_up, k_k, k_a, r_k, lnx_w, lnx_b, attn_sinks, w_out, ffn_w_gate, ffn_w_up, ffn_w_down, router_w, router_b, moe_w_gate, moe_w_up, moe_w_down):
    p = {"rel_bias": rel_bias, "ada_w": ada_w, "ada_b": ada_b, "norm_pre": norm_pre, "norm_post": norm_post,
         "w_in": w_in, "mu_shift": mu_shift, "w_decay_up": w_decay_up, "decay_base": decay_base,
         "w_iclr_up": w_iclr_up, "iclr_base": iclr_base, "w_gate_up": w_gate_up, "k_k": k_k, "k_a": k_a,
         "r_k": r_k.reshape(r_k.shape[0], -1), "lnx_w": lnx_w, "lnx_b": lnx_b, "attn_sinks": attn_sinks,
         "w_out": w_out, "ffn_w_gate": ffn_w_gate, "ffn_w_up": ffn_w_up, "ffn_w_down": ffn_w_down,
         "router_w": router_w, "router_b": router_b, "moe_w_gate": moe_w_gate, "moe_w_up": moe_w_up,
         "moe_w_down": moe_w_down}
    return _forward(x_prompt, x_sample, c_prompt, c_sample, state_wkv, state_shift, cache_swa_k, cache_swa_v, p)
```

```python
import functools

import numpy as np
import jax
import jax.numpy as jnp
from jax import lax
from jax.experimental import pallas as pl
from jax.experimental.pallas import tpu as pltpu

F32 = jnp.float32
BF16 = jnp.bfloat16

D_MODEL = 1024
HEAD_DIM = 64
RWKV_WIDTH = 512
RWKV_HEADS = RWKV_WIDTH // HEAD_DIM
SWA_WIDTH = 512
SWA_HEADS = SWA_WIDTH // HEAD_DIM
SWA_KV_HEADS = 2
SWA_GROUP = SWA_HEADS // SWA_KV_HEADS
KV_WIDTH = SWA_KV_HEADS * HEAD_DIM
WINDOW = 128
DECAY_LORA = 64
ICLR_LORA = 64
GATE_LORA = 128
RWKV_COLS = 3 * RWKV_WIDTH + DECAY_LORA + ICLR_LORA + GATE_LORA
IN_COLS = RWKV_COLS + SWA_WIDTH + 2 * KV_WIDTH
LORA_OFF = 3 * RWKV_WIDTH
GATE_OFF = LORA_OFF + DECAY_LORA + ICLR_LORA
LN_X_EPS = 64e-5
RMS_EPS = 1e-6
N_BUCKETS = 32
MAX_DISTANCE = 128
N_EXPERTS = 8
NEG_INF = -1e30

LANES = 128
SUBLANES = 8
VMEM_LIMIT = 56 * 1024 * 1024

WKV_CHUNK = 64
PAIR = 2 * HEAD_DIM

NN = (((1,), (0,)), ((), ()))
NT = (((1,), (1,)), ((), ()))
TN = (((0,), (0,)), ((), ()))


def _dot(a, b, dims=NN):
    return lax.dot_general(a.astype(BF16), b.astype(BF16), dims, preferred_element_type=F32)


def _split(x, pieces):
    out = []
    for _ in range(pieces - 1):
        hi = x.astype(BF16)
        out.append(hi)
        x = x - hi.astype(F32)
    out.append(x.astype(BF16))
    return out


def _select_dot(x, sel, pieces=3, sel_left=False):
    d = lambda t: lax.dot_general(*((sel, t) if sel_left else (t, sel)), NN, preferred_element_type=F32)
    return sum(d(t) for t in _split(x, pieces))


def _dot3(a, b):
    a_hi, a_lo = _split(a, 2)
    b_hi, b_lo = _split(b, 2)
    d = lambda x, y: lax.dot_general(x, y, NN, preferred_element_type=F32)
    return d(a_hi, b_hi) + d(a_lo, b_hi) + d(a_hi, b_lo)


def _sigmoid(x):
    return 1.0 / (1.0 + jnp.exp(-x))


def _params(sem):
    return pltpu.CompilerParams(dimension_semantics=sem, vmem_limit_bytes=VMEM_LIMIT)


def _rms(x, g):
    return x * lax.rsqrt(jnp.mean(x * x, axis=-1, keepdims=True) + RMS_EPS) * g


def _mod_rows(mod_ref, per_row, tiles_per_seq, prompt_row0):
    if per_row:
        return mod_ref[...]
    b = pl.program_id(0) // tiles_per_seq
    return mod_ref[pl.ds(prompt_row0 + b, 1), :]


def _ada_body(c_ref, w_ref, b_ref, o_ref):
    c = c_ref[...]
    o_ref[...] = _dot(c * _sigmoid(c), w_ref[...]) + b_ref[...]


def _ada_all(c_all, ada_w, ada_b):
    R, D = c_all.shape
    n = ada_w.shape[0]
    tn = 1024
    return pl.pallas_call(
        _ada_body,
        out_shape=jax.ShapeDtypeStruct((n, R, 3 * D), F32),
        grid=(n, 3 * D // tn),
        in_specs=[pl.BlockSpec((R, D), lambda i, j: (0, 0)),
                  pl.BlockSpec((None, D, tn), lambda i, j: (i, 0, j)),
                  pl.BlockSpec((None, 1, tn), lambda i, j: (i, 0, j))],
        out_specs=pl.BlockSpec((None, R, tn), lambda i, j: (i, 0, j)),
        compiler_params=_params(("arbitrary", "arbitrary")),
        name="ada_mod",
    )(c_all, ada_w, ada_b)


def _inproj_body(per_row, tiles_per_seq, prompt_row0, x_ref, mod_ref, g_ref, w_ref, pr_ref, q_ref, k_ref, v_ref):
    D = D_MODEL
    m = _mod_rows(mod_ref, per_row, tiles_per_seq, prompt_row0)
    h = _rms(x_ref[...], g_ref[...]) * (1.0 + m[:, D:2 * D]) + m[:, 0:D]
    proj = _dot(h, w_ref[...])
    pr_ref[...] = proj[:, 0:RWKV_COLS]
    q_ref[...] = proj[:, RWKV_COLS:RWKV_COLS + SWA_WIDTH]
    k_ref[...] = proj[:, RWKV_COLS + SWA_WIDTH:RWKV_COLS + SWA_WIDTH + KV_WIDTH]
    v_ref[...] = proj[:, RWKV_COLS + SWA_WIDTH + KV_WIDTH:IN_COLS]


def _mod_spec(mod, per_row, tm, nargs):
    R = mod.shape[0]
    if per_row:
        return pl.BlockSpec((tm, 3 * D_MODEL), lambda i, *_: (0, 0))
    return pl.BlockSpec((R, 3 * D_MODEL), lambda i, *_: (0, 0))


def _inproj(x, mod, g, w, per_row, tm, tiles_per_seq, prompt_row0):
    N, D = x.shape
    row = lambda i: (i, 0)
    fixed = lambda i: (0, 0)
    return pl.pallas_call(
        functools.partial(_inproj_body, per_row, tiles_per_seq, prompt_row0),
        out_shape=(jax.ShapeDtypeStruct((N, RWKV_COLS), F32), jax.ShapeDtypeStruct((N, SWA_WIDTH), F32),
                   jax.ShapeDtypeStruct((N, KV_WIDTH), F32), jax.ShapeDtypeStruct((N, KV_WIDTH), F32)),
        grid=(N // tm,),
        in_specs=[pl.BlockSpec((tm, D), row), _mod_spec(mod, per_row, tm, 1),
                  pl.BlockSpec((1, D), fixed), pl.BlockSpec((D, IN_COLS), fixed)],
        out_specs=(pl.BlockSpec((tm, RWKV_COLS), row), pl.BlockSpec((tm, SWA_WIDTH), row),
                   pl.BlockSpec((tm, KV_WIDTH), row), pl.BlockSpec((tm, KV_WIDTH), row)),
        compiler_params=_params(("arbitrary",)),
        name="in_proj",
    )(x, mod, g, w)


PREP_KEYS = ("mu", "wd", "wi", "wg", "dbase", "ibase", "kk", "ka", "rk", "headsum")
POST_KEYS = ("lnw", "lnb", "headmean")


def _prep_specs(fixed):
    W = RWKV_WIDTH
    vec = pl.BlockSpec((1, W), fixed)
    return [pl.BlockSpec((1, RWKV_COLS), fixed), pl.BlockSpec((LANES, W), fixed), pl.BlockSpec((LANES, W), fixed),
            pl.BlockSpec((GATE_LORA, W), fixed), vec, vec, vec, vec, vec, pl.BlockSpec((W, W), fixed)]


def _rwkv_features(pr, shifted, mu_ref, wd_ref, wi_ref, wg_ref, dbase_ref, ibase_ref, kk_ref, ka_ref, rk_ref, hs_ref):
    W = RWKV_WIDTH
    xs = pr + (shifted - pr) * mu_ref[...]
    r = xs[:, 0:W]
    k = xs[:, W:2 * W]
    v = xs[:, 2 * W:3 * W]
    lora = xs[:, LORA_OFF:GATE_OFF]
    gl = xs[:, GATE_OFF:RWKV_COLS]
    z = dbase_ref[...] + _dot3(jnp.tanh(lora), wd_ref[...])
    ld = -float(np.exp(-0.5)) * _sigmoid(z)
    iclr = _sigmoid(ibase_ref[...] + _dot3(lora, wi_ref[...]))
    g = _dot3(_sigmoid(gl), wg_ref[...])
    hs = hs_ref[...]
    kk = k * kk_ref[...]
    kk = kk / jnp.maximum(jnp.sqrt(_select_dot(kk * kk, hs, 2)), 1e-12)
    kh = k * (1.0 + (iclr - 1.0) * ka_ref[...])
    bonus = _select_dot(r * kh * rk_ref[...], hs, 2) * v
    return r, ld, kh, v, -kk, kk * iclr, g, bonus


def _rwkv_post(y, g, bonus, lnw, lnb, hm):
    mean = _select_dot(y, hm, 2)
    dv = y - mean
    var = _select_dot(dv * dv, hm, 2)
    return (dv * lax.rsqrt(var + LN_X_EPS) * lnw + lnb + bonus) * g


def _prep_body(pr_ref, prev_ref, *refs):
    outs = refs[len(PREP_KEYS):]
    for ref, val in zip(outs, _rwkv_features(pr_ref[...], prev_ref[...], *refs[:len(PREP_KEYS)])):
        ref[...] = val


def _prep_prompt_body(tiles_per_seq, pr_ref, prev_ref, *refs):
    pr = pr_ref[...]
    first = (pl.program_id(0) % tiles_per_seq) == 0
    carry = jnp.where(first, 0.0, prev_ref[SUBLANES - 1:SUBLANES, :])
    shifted = jnp.where(lax.broadcasted_iota(jnp.int32, pr.shape, 0) == 0, carry, pltpu.roll(pr, 1, 0))
    for ref, val in zip(refs[len(PREP_KEYS):], _rwkv_features(pr, shifted, *refs[:len(PREP_KEYS)])):
        ref[...] = val


def _prep_prompt(pr, lw, tm, tiles_per_seq):
    N = pr.shape[0]
    W = RWKV_WIDTH
    row = lambda i: (i, 0)
    fixed = lambda i: (0, 0)
    per = tm // SUBLANES
    out = jax.ShapeDtypeStruct((N, W), F32)
    return pl.pallas_call(
        functools.partial(_prep_prompt_body, tiles_per_seq),
        out_shape=(out,) * 8,
        grid=(N // tm,),
        in_specs=[pl.BlockSpec((tm, RWKV_COLS), row),
                  pl.BlockSpec((SUBLANES, RWKV_COLS), lambda i: (jnp.maximum(i * per - 1, 0), 0))] + _prep_specs(fixed),
        out_specs=(pl.BlockSpec((tm, W), row),) * 8,
        compiler_params=_params(("arbitrary",)),
        name="rwkv_prep",
    )(pr, pr, *[lw[k] for k in PREP_KEYS])


def _prep_sample(pr, prev, lw):
    N = pr.shape[0]
    W = RWKV_WIDTH
    row = lambda i: (i, 0)
    fixed = lambda i: (0, 0)
    out = jax.ShapeDtypeStruct((N, W), F32)
    return pl.pallas_call(
        _prep_body,
        out_shape=(out,) * 8,
        grid=(1,),
        in_specs=[pl.BlockSpec((N, RWKV_COLS), row), pl.BlockSpec((N, RWKV_COLS), row)] + _prep_specs(fixed),
        out_specs=(pl.BlockSpec((N, W), row),) * 8,
        compiler_params=_params(("arbitrary",)),
        name="rwkv_prep",
    )(pr, prev, *[lw[k] for k in PREP_KEYS])


def _stack_heads(x, lane_head0):
    return jnp.concatenate([jnp.where(lane_head0, x, 0.0), jnp.where(lane_head0, 0.0, x)], axis=0)


def _fold_heads(x):
    c = x.shape[0] // 2
    return x[0:c] + x[c:2 * c]


def _dots(xs, ys, dims=NN):
    return [_dot(x, y, dims) for x, y in zip(xs, ys)]


def _unit_lower_inverse(ns, same16, eye):
    nd = [jnp.where(same16, n, 0.0) for n in ns]
    no = [n - d for n, d in zip(ns, nd)]
    n2 = _dots(nd, nd)
    n4 = _dots(n2, n2)
    n8 = _dots(n4, n4)
    td = [eye + d for d in nd]
    for pw in (n2, n4, n8):
        td = [t + u for t, u in zip(td, _dots(td, pw))]
    q = _dots(td, no)
    q2 = _dots(q, q)
    z = [eye + x for x in q]
    z = [t + u for t, u in zip(z, _dots(z, q2))]
    return _dots(z, td)


def _wkv_chunk_body(chunks, r_all, ld_all, k_all, v_all_, a_all, b_all, y_ref, s_ref, h_ref):
    C = WKV_CHUNK
    n_pairs = RWKV_WIDTH // PAIR

    @pl.when(pl.program_id(1) == 0)
    def _():
        h_ref[...] = jnp.zeros_like(h_ref)

    ri = lax.broadcasted_iota(jnp.int32, (PAIR, PAIR), 0)
    ci = lax.broadcasted_iota(jnp.int32, (PAIR, PAIR), 1)
    same_head = (ri // C) == (ci // C)
    strict_lower = same_head & (ci < ri)
    incl_lower = same_head & (ci <= ri)
    same16 = (ri // 16) == (ci // 16)
    eye_b = ri == ci
    eye = jnp.where(eye_b, 1.0, 0.0)
    tri = jnp.where(lax.broadcasted_iota(jnp.int32, (C, C), 1) <= lax.broadcasted_iota(jnp.int32, (C, C), 0),
                    1.0, 0.0).astype(BF16)
    lane_head0 = lax.broadcasted_iota(jnp.int32, (C, PAIR), 1) < HEAD_DIM
    zeros = jnp.zeros((PAIR, PAIR), F32)

    cat0 = lambda x, y: jnp.concatenate([x, y], axis=0)
    cat1 = lambda x, y: jnp.concatenate([x, y], axis=1)

    xa, xr, v_st, bh_st, kh_st, yb, yk, p_all = [], [], [], [], [], [], [], []
    for c in range(chunks):
        rows = slice(c * C, (c + 1) * C)
        ld = ld_all[rows, :]
        cum = _select_dot(ld, tri, 3, sel_left=True)
        last = cum[C - 1:C, :]
        p_inv = jnp.exp(-cum)
        p_tail = jnp.exp(last - cum)
        p_end = jnp.exp(last)
        a_t = a_all[rows, :] * jnp.exp(cum - ld)
        r_t = r_all[rows, :] * jnp.exp(cum)
        b_raw = b_all[rows, :]
        k_raw = k_all[rows, :]
        b_t = b_raw * p_inv
        k_t = k_raw * p_inv
        b_h = b_raw * p_tail
        k_h = k_raw * p_tail
        v_all = v_all_[rows, :]
        for j in range(n_pairs):
            lanes = slice(j * PAIR, (j + 1) * PAIR)
            xa.append(_stack_heads(a_t[:, lanes], lane_head0))
            xr.append(_stack_heads(r_t[:, lanes], lane_head0))
            v_st.append(_stack_heads(v_all[:, lanes], lane_head0))
            bh_st.append(_stack_heads(b_h[:, lanes], lane_head0))
            kh_st.append(_stack_heads(k_h[:, lanes], lane_head0))
            yb.append(cat0(b_t[:, lanes], b_t[:, lanes]))
            yk.append(cat0(k_t[:, lanes], k_t[:, lanes]))
            p_all.append(p_end[:, lanes])

    gram = _dots([cat0(x, y) for x, y in zip(xa, xr)], [cat0(x, y) for x, y in zip(yb, yk)], NT)
    n_mat = [jnp.where(strict_lower, g[0:PAIR, 0:PAIR], 0.0) for g in gram]
    m_mat = [jnp.where(strict_lower, g[0:PAIR, PAIR:2 * PAIR], 0.0) for g in gram]
    a_rbk = [cat1(jnp.where(incl_lower, g[PAIR:2 * PAIR, 0:PAIR], 0.0),
                  jnp.where(incl_lower, g[PAIR:2 * PAIR, PAIR:2 * PAIR], 0.0)) for g in gram]
    t_inv = _unit_lower_inverse(n_mat, same16, eye)
    mv = _dots(m_mat, v_st)
    tx = _dots(t_inv, [cat1(x, y) for x, y in zip(xa, mv)])
    rhs = [cat0(t, cat1(zeros, v)) for t, v in zip(tx, v_st)]
    ry = _dots(a_rbk, rhs)
    pp = _dots([cat0(x, y) for x, y in zip(bh_st, kh_st)], rhs, TN)

    for c in range(chunks):
        us = [c * n_pairs + j for j in range(n_pairs)]
        h0 = [h_ref[j] for j in range(n_pairs)]
        r_bar = [_fold_heads(xr[u] + ry[u][:, 0:PAIR]) for u in us]
        phi = [pp[u][:, 0:PAIR] + jnp.where(eye_b, p_all[u], 0.0) for u in us]
        ys = _dots(r_bar, h0)
        hs = _dots(phi, h0)
        for j, u in enumerate(us):
            y_ref[c * C:(c + 1) * C, j * PAIR:(j + 1) * PAIR] = ys[j] + _fold_heads(ry[u][:, PAIR:2 * PAIR])
            h_ref[j] = hs[j] + pp[u][:, PAIR:2 * PAIR]

    @pl.when(pl.program_id(1) == pl.num_programs(1) - 1)
    def _():
        s_ref[...] = h_ref[...]


def _wkv_prompt(r, ld, k, v, a, b, batch, seq):
    N, W = r.shape
    chunks = 4
    tt = chunks * WKV_CHUNK
    steps = seq // tt
    n_pairs = W // PAIR
    row = lambda bb, t: (bb * steps + t, 0)
    spec = pl.BlockSpec((tt, W), row)
    return pl.pallas_call(
        functools.partial(_wkv_chunk_body, chunks),
        out_shape=(jax.ShapeDtypeStruct((N, W), F32), jax.ShapeDtypeStruct((batch, n_pairs, PAIR, PAIR), F32)),
        grid=(batch, steps),
        in_specs=[spec] * 6,
        out_specs=(spec, pl.BlockSpec((None, n_pairs, PAIR, PAIR), lambda bb, t: (bb, 0, 0, 0))),
        scratch_shapes=[pltpu.VMEM((n_pairs, PAIR, PAIR), F32)],
        compiler_params=_params(("arbitrary", "arbitrary")),
        name="wkv_chunk_scan",
    )(r, ld, k, v, a, b)


def _wkv_step_body(s_ref, r_ref, ld_ref, k_ref, v_ref, a_ref, b_ref, exp_ref, red_ref, y_ref, so_ref):
    HD = HEAD_DIM
    nrep = HD * HD // LANES
    lane = lax.broadcasted_iota(jnp.int32, r_ref.shape, 1)
    low = lane < HD
    y = jnp.zeros(r_ref.shape, F32)
    for hh in range(2):
        def tiled(ref, fn=None):
            x = ref[...]
            if fn is not None:
                x = fn(x)
            sw = pltpu.roll(x, HD, 1)
            both = jnp.where(low, x, sw) if hh == 0 else jnp.where(low, sw, x)
            return jnp.tile(both, (1, nrep))
        cols = slice(hh * HD * HD, (hh + 1) * HD * HD)
        s = s_ref[:, cols]
        expand = exp_ref[hh]
        reduce_ = red_ref[hh]
        sa = _select_dot(s * tiled(a_ref), reduce_)
        s_new = (s * tiled(ld_ref, jnp.exp) + _select_dot(sa, expand) * tiled(b_ref)
                 + _select_dot(v_ref[...], expand) * tiled(k_ref))
        so_ref[:, cols] = s_new
        y = y + _select_dot(s_new * tiled(r_ref), reduce_)
    y_ref[...] = y


def _wkv_step_consts():
    HD = HEAD_DIM
    expand = np.zeros((2, PAIR, HD * HD), np.float32)
    for hh in range(2):
        for vv in range(HD):
            expand[hh, hh * HD + vv, vv * HD:(vv + 1) * HD] = 1.0
    return jnp.asarray(expand, BF16), jnp.asarray(expand.transpose(0, 2, 1), BF16)


def _wkv_sample(state, r, ld, k, v, a, b):
    B = state.shape[0]
    W = RWKV_WIDTH
    HD2 = HEAD_DIM * HEAD_DIM
    expand, reduce_ = _wkv_step_consts()
    s2 = state.reshape(B, RWKV_HEADS * HD2)
    st_spec = pl.BlockSpec((B, 2 * HD2), lambda j: (0, j))
    vec = pl.BlockSpec((B, PAIR), lambda j: (0, j))
    y, s_new = pl.pallas_call(
        _wkv_step_body,
        out_shape=(jax.ShapeDtypeStruct((B, W), F32), jax.ShapeDtypeStruct((B, RWKV_HEADS * HD2), F32)),
        grid=(W // PAIR,),
        in_specs=[st_spec] + [vec] * 6 + [pl.BlockSpec((2, PAIR, HD2), lambda j: (0, 0, 0)),
                                          pl.BlockSpec((2, HD2, PAIR), lambda j: (0, 0, 0))],
        out_specs=(vec, st_spec),
        compiler_params=_params(("arbitrary",)),
        name="wkv_step",
    )(s2, r, ld, k, v, a, b, expand, reduce_)
    return y, s_new.reshape(state.shape)


def _q_perm():
    return np.array([(h * SWA_GROUP + g) * HEAD_DIM + d for g in range(SWA_GROUP) for h in range(SWA_KV_HEADS)
                     for d in range(HEAD_DIM)], np.int32)


def _rel_bucket_np(dist):
    max_exact = N_BUCKETS // 2
    d = np.maximum(dist, 0)
    ratio = np.log(np.maximum(d, 1).astype(np.float32) / np.float32(max_exact)) / np.float32(
        np.log(MAX_DISTANCE / max_exact))
    large = np.minimum(max_exact + (ratio.astype(np.float32) * np.float32(N_BUCKETS - max_exact)).astype(np.int32),
                       N_BUCKETS - 1)
    return np.where(d < max_exact, d, large).astype(np.int32)


def _bias_from_buckets(idx, rb_ref, head):
    acc = jnp.zeros(idx.shape, F32)
    for bk in range(N_BUCKETS):
        acc = jnp.where(idx == bk, rb_ref[bk, head], acc)
    return acc


def _swa_prompt_body(rb_ref, sink_ref, idx_ref, q_ref, kp_ref, kc_ref, vp_ref, vc_ref, o_ref, bias_ref):
    Q = WINDOW
    first = (pl.program_id(0) == 0) & (pl.program_id(1) == 0)

    @pl.when(first)
    def _():
        idx = idx_ref[...]
        qi = lax.broadcasted_iota(jnp.int32, (Q, 2 * Q), 0)
        kj = lax.broadcasted_iota(jnp.int32, (Q, 2 * Q), 1)
        in_window = ((kj < Q) & (kj >= qi)) | ((kj >= Q) & ((kj - Q) <= qi))
        for g in range(SWA_GROUP):
            for h in range(SWA_KV_HEADS):
                bias = jnp.where(in_window, _bias_from_buckets(idx, rb_ref, h * SWA_GROUP + g), NEG_INF)
                bias_ref[0, g * SWA_KV_HEADS + h] = bias
                bias_ref[1, g * SWA_KV_HEADS + h] = jnp.where(kj < Q, NEG_INF, bias)

    table = jnp.where(pl.program_id(1) > 0, 0, 1)
    lane_kv0 = lax.broadcasted_iota(jnp.int32, (Q, LANES), 1) < HEAD_DIM
    kcat = jnp.concatenate([kp_ref[...], kc_ref[...]], axis=0).astype(BF16)
    vcat = jnp.concatenate([vp_ref[...], vc_ref[...]], axis=0).astype(BF16)
    scale = HEAD_DIM ** -0.5
    for g in range(SWA_GROUP):
        qg = q_ref[:, g * LANES:(g + 1) * LANES] * scale
        outs = []
        for h in range(SWA_KV_HEADS):
            qm = jnp.where(lane_kv0, qg, 0.0) if h == 0 else jnp.where(lane_kv0, 0.0, qg)
            s = _dot(qm, kcat, NT) + bias_ref[table, g * SWA_KV_HEADS + h]
            sink = sink_ref[h * SWA_GROUP + g]
            m = jnp.maximum(jnp.max(s, axis=-1, keepdims=True), sink)
            p = jnp.exp(s - m)
            den = jnp.sum(p, axis=-1, keepdims=True) + jnp.exp(sink - m)
            outs.append(_dot(p, vcat) / den)
        o_ref[:, g * LANES:(g + 1) * LANES] = jnp.where(lane_kv0, outs[0], outs[1])


def _swa_prompt(q, k, v, rel_bias, sinks, batch, seq):
    N = q.shape[0]
    Q = WINDOW
    nb = seq // Q
    qi = np.arange(Q)[:, None]
    kj = np.arange(2 * Q)[None, :]
    idx = jnp.asarray(_rel_bucket_np(qi + Q - kj))
    cur = lambda bb, n: (bb * nb + n, 0)
    prev = lambda bb, n: (bb * nb + jnp.maximum(n - 1, 0), 0)
    kv_c = pl.BlockSpec((Q, KV_WIDTH), cur)
    kv_p = pl.BlockSpec((Q, KV_WIDTH), prev)
    smem = pl.BlockSpec(memory_space=pltpu.SMEM)
    return pl.pallas_call(
        _swa_prompt_body,
        out_shape=jax.ShapeDtypeStruct((N, SWA_WIDTH), F32),
        grid=(batch, nb),
        in_specs=[smem, smem, pl.BlockSpec((Q, 2 * Q), lambda bb, n: (0, 0)),
                  pl.BlockSpec((Q, SWA_WIDTH), cur), kv_p, kv_c, kv_p, kv_c],
        out_specs=pl.BlockSpec((Q, SWA_WIDTH), cur),
        scratch_shapes=[pltpu.VMEM((2, SWA_HEADS, Q, 2 * Q), F32)],
        compiler_params=_params(("arbitrary", "arbitrary")),
        name="swa_prompt",
    )(rel_bias, sinks, idx, q, k, k, v, v)


def _swa_sample_body(rb_ref, sink_ref, idx_ref, q_ref, kn_ref, vn_ref, kb_ref, vb_ref, o_ref, ko_ref, vo_ref,
                     bias_ref, extra_ref):
    W = WINDOW
    G, KVH = SWA_GROUP, SWA_KV_HEADS

    @pl.when(pl.program_id(0) == 0)
    def _():
        idx = jnp.broadcast_to(idx_ref[...], (SUBLANES, W))
        row = lax.broadcasted_iota(jnp.int32, (SUBLANES, W), 0)
        acc = jnp.zeros((SUBLANES, W), F32)
        ext = jnp.zeros((SUBLANES, LANES), F32)
        lane = lax.broadcasted_iota(jnp.int32, (SUBLANES, LANES), 1)
        for h in range(KVH):
            for g in range(G):
                head = h * G + g
                r = h * G + g
                acc = jnp.where(row == r, _bias_from_buckets(idx, rb_ref, head), acc)
                ext = jnp.where((row == r) & (lane == 0), rb_ref[0, head], ext)
                ext = jnp.where((row == r) & (lane == 1), sink_ref[head], ext)
        bias_ref[...] = acc
        extra_ref[...] = ext

    TB = q_ref.shape[0]
    lane_kv0 = lax.broadcasted_iota(jnp.int32, (TB, G, LANES), 2) < HEAD_DIM
    q4 = q_ref[...]
    qrows = jnp.concatenate([jnp.where(lane_kv0, q4, 0.0), jnp.where(lane_kv0, 0.0, q4)], axis=1)
    kb = kb_ref[...]
    vb = vb_ref[...]
    kn = kn_ref[...]
    vn = vn_ref[...]
    scale = HEAD_DIM ** -0.5
    bdims = (((2,), (2,)), ((0,), (0,)))
    s = lax.dot_general(qrows.astype(BF16), kb.astype(BF16), bdims, preferred_element_type=F32) * scale
    s = s + bias_ref[...][None]
    s_self = jnp.sum(qrows * kn, axis=-1, keepdims=True) * scale + extra_ref[:, 0:1][None]
    sink = extra_ref[:, 1:2][None]
    m = jnp.maximum(jnp.maximum(jnp.max(s, axis=-1, keepdims=True), s_self), sink)
    p = jnp.exp(s - m)
    p_self = jnp.exp(s_self - m)
    den = jnp.sum(p, axis=-1, keepdims=True) + p_self + jnp.exp(sink - m)
    pv = lax.dot_general(p.astype(BF16), vb.astype(BF16), (((2,), (1,)), ((0,), (0,))), preferred_element_type=F32)
    o = (pv + p_self * vn) / den
    o_ref[...] = jnp.where(lane_kv0, o[:, 0:G], o[:, G:2 * G])

    rowmod = lax.broadcasted_iota(jnp.int32, (TB, W, KV_WIDTH), 1)
    for buf, new, out in ((kb, kn, ko_ref), (vb, vn, vo_ref)):
        rolled = pltpu.roll(buf.reshape(TB * W, KV_WIDTH), TB * W - 1, 0).reshape(TB, W, KV_WIDTH)
        out[...] = jnp.where(rowmod == W - 1, new, rolled)


def _swa_sample(q, k, v, kbuf, vbuf, rel_bias, sinks):
    B = q.shape[0]
    W = WINDOW
    tb = 16
    idx = jnp.asarray(_rel_bucket_np(W - np.arange(W))[None, :])
    q4 = q.reshape(B, SWA_GROUP, LANES)
    kn = k.reshape(B, 1, KV_WIDTH)
    vn = v.reshape(B, 1, KV_WIDTH)
    kb = kbuf.reshape(B, W, KV_WIDTH)
    vb = vbuf.reshape(B, W, KV_WIDTH)
    smem = pl.BlockSpec(memory_space=pltpu.SMEM)
    b3 = lambda i: (i, 0, 0)
    o, ko, vo = pl.pallas_call(
        _swa_sample_body,
        out_shape=(jax.ShapeDtypeStruct((B, SWA_GROUP, LANES), F32), jax.ShapeDtypeStruct((B, W, KV_WIDTH), F32),
                   jax.ShapeDtypeStruct((B, W, KV_WIDTH), F32)),
        grid=(B // tb,),
        in_specs=[smem, smem, pl.BlockSpec((1, W), lambda i: (0, 0)),
                  pl.BlockSpec((tb, SWA_GROUP, LANES), b3), pl.BlockSpec((tb, 1, KV_WIDTH), b3),
                  pl.BlockSpec((tb, 1, KV_WIDTH), b3), pl.BlockSpec((tb, W, KV_WIDTH), b3),
                  pl.BlockSpec((tb, W, KV_WIDTH), b3)],
        out_specs=(pl.BlockSpec((tb, SWA_GROUP, LANES), b3), pl.BlockSpec((tb, W, KV_WIDTH), b3),
                   pl.BlockSpec((tb, W, KV_WIDTH), b3)),
        scratch_shapes=[pltpu.VMEM((SUBLANES, W), F32), pltpu.VMEM((SUBLANES, LANES), F32)],
        compiler_params=_params(("arbitrary",)),
        name="swa_sample",
    )(rel_bias, sinks, idx, q4, kn, vn, kb, vb)
    return o.reshape(B, SWA_WIDTH), ko.reshape(kbuf.shape), vo.reshape(vbuf.shape)


def _outproj_body(per_row, tiles_per_seq, prompt_row0, x_ref, mod_ref, yw_ref, g_ref, bonus_ref, ya_ref,
                  lnw_ref, lnb_ref, hm_ref, wr_ref, wa_ref, gpost_ref, o_ref):
    D = D_MODEL
    m = _mod_rows(mod_ref, per_row, tiles_per_seq, prompt_row0)
    yr = _rwkv_post(yw_ref[...], g_ref[...], bonus_ref[...], lnw_ref[...], lnb_ref[...], hm_ref[...])
    mix = _dot(yr, wr_ref[...]) + _dot(ya_ref[...], wa_ref[...])
    o_ref[...] = x_ref[...] + m[:, 2 * D:3 * D] * _rms(mix, gpost_ref[...])


def _outproj(x, mod, yw, g, bonus, ya, lw, per_row, tm, tiles_per_seq, prompt_row0):
    N, D = x.shape
    W = RWKV_WIDTH
    row = lambda i: (i, 0)
    fixed = lambda i: (0, 0)
    half = pl.BlockSpec((tm, W), row)
    vec = pl.BlockSpec((1, W), fixed)
    return pl.pallas_call(
        functools.partial(_outproj_body, per_row, tiles_per_seq, prompt_row0),
        out_shape=jax.ShapeDtypeStruct((N, D), F32),
        grid=(N // tm,),
        in_specs=[pl.BlockSpec((tm, D), row), _mod_spec(mod, per_row, tm, 1), half, half, half, half, vec, vec,
                  pl.BlockSpec((W, W), fixed), pl.BlockSpec((W, D), fixed), pl.BlockSpec((W, D), fixed),
                  pl.BlockSpec((1, D), fixed)],
        out_specs=pl.BlockSpec((tm, D), row),
        compiler_params=_params(("arbitrary",)),
        name="out_proj",
    )(x, mod, yw, g, bonus, ya, lw["lnw"], lw["lnb"], lw["headmean"], lw["w_out_r"], lw["w_out_a"], lw["g_post0"])


def _ffn_body(moe, per_row, tiles_per_seq, prompt_row0, x_ref, mod_ref, gpre_ref, gpost_ref, rw_ref, rb_ref,
              wg_ref, wu_ref, wd_ref, o_ref, h_ref, acc_ref, comb_ref):
    D = D_MODEL
    e = pl.program_id(1)
    f = pl.program_id(2)
    first = (e == 0) & (f == 0)
    last = (e == pl.num_programs(1) - 1) & (f == pl.num_programs(2) - 1)

    @pl.when(first)
    def _():
        m = _mod_rows(mod_ref, per_row, tiles_per_seq, prompt_row0)
        h = _rms(x_ref[...], gpre_ref[...]) * (1.0 + m[:, D:2 * D]) + m[:, 0:D]
        h_ref[...] = h.astype(BF16)
        acc_ref[...] = jnp.zeros_like(acc_ref)
        if moe:
            logits = _dot3(h, rw_ref[...]) + rb_ref[...]
            lane = lax.broadcasted_iota(jnp.int32, logits.shape, 1)
            m1 = jnp.max(logits, axis=-1, keepdims=True)
            i1 = jnp.min(jnp.where(logits == m1, lane, LANES), axis=-1, keepdims=True)
            rest = jnp.where(lane == i1, -jnp.inf, logits)
            m2 = jnp.max(rest, axis=-1, keepdims=True)
            i2 = jnp.min(jnp.where(rest == m2, lane, LANES), axis=-1, keepdims=True)
            e2 = jnp.exp(m2 - m1)
            comb_ref[...] = jnp.where(lane == i1, 1.0 / (1.0 + e2), 0.0) + jnp.where(lane == i2, e2 / (1.0 + e2), 0.0)

    h = h_ref[...]
    gate = _dot(h, wg_ref[...])
    up = _dot(h, wu_ref[...])
    act = gate * _sigmoid(gate) * up
    if moe:
        comb = comb_ref[...]
        lane = lax.broadcasted_iota(jnp.int32, comb.shape, 1)
        act = act * jnp.sum(jnp.where(lane == e, comb, 0.0), axis=-1, keepdims=True)
    acc_ref[...] += _dot(act, wd_ref[...])

    @pl.when(last)
    def _():
        m = _mod_rows(mod_ref, per_row, tiles_per_seq, prompt_row0)
        o_ref[...] = x_ref[...] + m[:, 2 * D:3 * D] * _rms(acc_ref[...], gpost_ref[...])


def _ffn(x, mod, gpre, gpost, router_w, router_b, wg, wu, wd, w0, E, moe, per_row, tm, tiles_per_seq, prompt_row0,
         tf):
    N, D = x.shape
    F = wg.shape[-1]
    row = lambda i, e, f: (i, 0)
    fixed = lambda i, e, f: (0, 0)
    return pl.pallas_call(
        functools.partial(_ffn_body, moe, per_row, tiles_per_seq, prompt_row0),
        out_shape=jax.ShapeDtypeStruct((N, D), F32),
        grid=(N // tm, E, F // tf),
        in_specs=[pl.BlockSpec((tm, D), row), _mod_spec(mod, per_row, tm, 3),
                  pl.BlockSpec((1, D), fixed), pl.BlockSpec((1, D), fixed),
                  pl.BlockSpec((D, LANES), fixed), pl.BlockSpec((1, LANES), fixed),
                  pl.BlockSpec((None, D, tf), lambda i, e, f: (w0 + e, 0, f)),
                  pl.BlockSpec((None, D, tf), lambda i, e, f: (w0 + e, 0, f)),
                  pl.BlockSpec((None, tf, D), lambda i, e, f: (w0 + e, f, 0))],
        out_specs=pl.BlockSpec((tm, D), row),
        scratch_shapes=[pltpu.VMEM((tm, D), BF16), pltpu.VMEM((tm, D), F32), pltpu.VMEM((tm, LANES), F32)],
        compiler_params=_params(("arbitrary", "arbitrary", "arbitrary")),
        name="moe_ffn" if moe else "dense_ffn",
    )(x, mod, gpre, gpost, router_w, router_b, wg, wu, wd)


ROW_TILE = D_MODEL // LANES
EXPERT_TILE = 512
INFO_E1, INFO_E2, INFO_P1, INFO_P2, INFO_POS1, INFO_POS2 = range(6)


def _top2(logits):
    lane = lax.broadcasted_iota(jnp.int32, logits.shape, 1)
    m1 = jnp.max(logits, axis=-1, keepdims=True)
    i1 = jnp.min(jnp.where(logits == m1, lane, LANES), axis=-1, keepdims=True)
    rest = jnp.where(lane == i1, -jnp.inf, logits)
    m2 = jnp.max(rest, axis=-1, keepdims=True)
    i2 = jnp.min(jnp.where(rest == m2, lane, LANES), axis=-1, keepdims=True)
    e2 = jnp.exp(m2 - m1)
    return lane, i1, i2, 1.0 / (1.0 + e2), e2 / (1.0 + e2)


def _to_row_tiles(ref, x):
    rows = x.shape[0]
    for c in range(ROW_TILE):
        ref[pl.ds(c, rows, stride=ROW_TILE), :] = x[:, c * LANES:(c + 1) * LANES]


def _from_row_tiles(ref, row0, rows):
    return jnp.concatenate([ref[pl.ds(row0 * ROW_TILE + c, rows, stride=ROW_TILE), :] for c in range(ROW_TILE)],
                           axis=1)


def _route_body(tiles_per_seq, prompt_row0, x_ref, mod_ref, gpre_ref, rw_ref, rb_ref, h_ref, info_ref, cnt_ref,
                base_ref):
    D = D_MODEL

    @pl.when(pl.program_id(0) == 0)
    def _():
        base_ref[...] = jnp.zeros_like(base_ref)

    m = _mod_rows(mod_ref, False, tiles_per_seq, prompt_row0)
    h = _rms(x_ref[...], gpre_ref[...]) * (1.0 + m[:, D:2 * D]) + m[:, 0:D]
    _to_row_tiles(h_ref, h)
    lane, i1, i2, p1, p2 = _top2(_dot3(h, rw_ref[...]) + rb_ref[...])
    onehot = jnp.where((lane == i1) | (lane == i2), 1.0, 0.0)
    T = h.shape[0]
    before = lax.broadcasted_iota(jnp.int32, (T, T), 1) < lax.broadcasted_iota(jnp.int32, (T, T), 0)
    rank = _dot(jnp.where(before, 1.0, 0.0), onehot) + base_ref[0:1, :]
    pos1 = jnp.sum(jnp.where(lane == i1, rank, 0.0), axis=-1, keepdims=True)
    pos2 = jnp.sum(jnp.where(lane == i2, rank, 0.0), axis=-1, keepdims=True)
    info = jnp.zeros(onehot.shape, F32)
    for col, val in ((INFO_E1, i1.astype(F32)), (INFO_E2, i2.astype(F32)), (INFO_P1, p1), (INFO_P2, p2),
                     (INFO_POS1, pos1), (INFO_POS2, pos2)):
        info = jnp.where(lane == col, val, info)
    info_ref[...] = info
    total = base_ref[...] + jnp.sum(onehot, axis=0, keepdims=True)
    base_ref[...] = total
    cnt_ref[...] = total


def _moe_route(x, mod, gpre, rw, rb, tm, tiles_per_seq, prompt_row0):
    N, D = x.shape
    row = lambda i: (i, 0)
    fixed = lambda i: (0, 0)
    return pl.pallas_call(
        functools.partial(_route_body, tiles_per_seq, prompt_row0),
        out_shape=(jax.ShapeDtypeStruct((N * ROW_TILE, LANES), F32), jax.ShapeDtypeStruct((N, LANES), F32),
                   jax.ShapeDtypeStruct((SUBLANES, LANES), F32)),
        grid=(N // tm,),
        in_specs=[pl.BlockSpec((tm, D), row), _mod_spec(mod, False, tm, 1), pl.BlockSpec((1, D), fixed),
                  pl.BlockSpec((D, LANES), fixed), pl.BlockSpec((1, LANES), fixed)],
        out_specs=(pl.BlockSpec((tm * ROW_TILE, LANES), row), pl.BlockSpec((tm, LANES), row),
                   pl.BlockSpec((SUBLANES, LANES), fixed)),
        scratch_shapes=[pltpu.VMEM((SUBLANES, LANES), F32)],
        compiler_params=_params(("arbitrary",)),
        name="moe_route",
    )(x, mod, gpre, rw, rb)


def _invert_body(n_tokens, dest_ref, src_ref):
    def clear(i, carry):
        src_ref[i] = 0
        return carry
    lax.fori_loop(0, src_ref.shape[0], clear, 0, unroll=16)

    for choice in range(2):
        def put(t, carry):
            src_ref[dest_ref[choice * n_tokens + t]] = t
            return carry
        lax.fori_loop(0, n_tokens, put, 0, unroll=16)


def _moe_invert(dest12, n_rows):
    smem = pl.BlockSpec(memory_space=pltpu.SMEM)
    return pl.pallas_call(
        functools.partial(_invert_body, dest12.shape[0] // 2),
        out_shape=jax.ShapeDtypeStruct((n_rows,), jnp.int32),
        in_specs=[smem],
        out_specs=smem,
        name="moe_invert",
    )(dest12)


def _row_copy(src_hbm, dst_vmem, sem, src_row, dst_row):
    return pltpu.make_async_copy(src_hbm.at[pl.ds(pl.multiple_of(src_row * ROW_TILE, ROW_TILE), ROW_TILE)],
                                 dst_vmem.at[pl.ds(pl.multiple_of(dst_row * ROW_TILE, ROW_TILE), ROW_TILE)], sem)


def _start_rows(idx_ref, idx0, src_hbm, dst_vmem, sem, rows):
    def body(r, carry):
        _row_copy(src_hbm, dst_vmem, sem, idx_ref[idx0 + r], r).start()
        return carry
    lax.fori_loop(0, rows, body, 0, unroll=16)


def _wait_rows(src_hbm, dst_vmem, sem, rows):
    pltpu.make_async_copy(src_hbm.at[pl.ds(0, rows * ROW_TILE)], dst_vmem.at[pl.ds(0, rows * ROW_TILE)], sem).wait()


def _experts_body(te_ref, nv_ref, src_ref, h_hbm, wg_ref, wu_ref, wd_ref, o_ref, xbuf, hb_ref, acc_ref, sem):
    TM = EXPERT_TILE
    i = pl.program_id(0)
    f = pl.program_id(1)
    slot = i % 2
    n_valid = nv_ref[0]
    valid = i < n_valid

    @pl.when(f == 0)
    def _():
        @pl.when((i == 0) & valid)
        def _():
            _start_rows(src_ref, 0, h_hbm, xbuf.at[0], sem.at[0], TM)

        @pl.when(i + 1 < n_valid)
        def _():
            _start_rows(src_ref, (i + 1) * TM, h_hbm, xbuf.at[1 - slot], sem.at[1 - slot], TM)

        @pl.when(valid)
        def _():
            _wait_rows(h_hbm, xbuf.at[slot], sem.at[slot], TM)
            hb_ref[...] = _from_row_tiles(xbuf.at[slot], 0, TM).astype(BF16)
            acc_ref[...] = jnp.zeros_like(acc_ref)

    @pl.when(valid)
    def _():
        h = hb_ref[...]
        gate = _dot(h, wg_ref[...])
        up = _dot(h, wu_ref[...])
        acc_ref[...] += _dot(gate * _sigmoid(gate) * up, wd_ref[...])

    @pl.when(f == pl.num_programs(1) - 1)
    def _():
        @pl.when(valid)
        def _():
            _to_row_tiles(o_ref, acc_ref[...])

        @pl.when(jnp.logical_not(valid))
        def _():
            o_ref[...] = jnp.zeros_like(o_ref)


def _moe_experts(h_rows, tile_expert, n_valid, src_tok, wg, wu, wd, tf):
    TM = EXPERT_TILE
    P = src_tok.shape[0]
    _, D, F = wg.shape
    grid_spec = pltpu.PrefetchScalarGridSpec(
        num_scalar_prefetch=3,
        grid=(P // TM, F // tf),
        in_specs=[pl.BlockSpec(memory_space=pl.ANY),
                  pl.BlockSpec((None, D, tf), lambda i, f, te, nv, src: (te[i], 0, f)),
                  pl.BlockSpec((None, D, tf), lambda i, f, te, nv, src: (te[i], 0, f)),
                  pl.BlockSpec((None, tf, D), lambda i, f, te, nv, src: (te[i], f, 0))],
        out_specs=pl.BlockSpec((TM * ROW_TILE, LANES), lambda i, f, te, nv, src: (i, 0)),
        scratch_shapes=[pltpu.VMEM((2, TM * ROW_TILE, LANES), F32), pltpu.VMEM((TM, D), BF16),
                        pltpu.VMEM((TM, D), F32), pltpu.SemaphoreType.DMA((2,))],
    )
    return pl.pallas_call(
        _experts_body,
        out_shape=jax.ShapeDtypeStruct((P * ROW_TILE, LANES), F32),
        grid_spec=grid_spec,
        compiler_params=_params(("arbitrary", "arbitrary")),
        name="moe_experts",
    )(tile_expert, n_valid, src_tok, h_rows, wg, wu, wd)


def _combine_body(tiles_per_seq, prompt_row0, dest_ref, f_hbm, x_ref, mod_ref, info_ref, gpost_ref, o_ref, gbuf, sem):
    D = D_MODEL
    i = pl.program_id(0)
    T = x_ref.shape[0]
    slot = i % 2

    @pl.when(i == 0)
    def _():
        _start_rows(dest_ref, 0, f_hbm, gbuf.at[0], sem.at[0], 2 * T)

    @pl.when(i + 1 < pl.num_programs(0))
    def _():
        _start_rows(dest_ref, (i + 1) * 2 * T, f_hbm, gbuf.at[1 - slot], sem.at[1 - slot], 2 * T)

    _wait_rows(f_hbm, gbuf.at[slot], sem.at[slot], 2 * T)
    info = info_ref[...]
    f1 = _from_row_tiles(gbuf.at[slot], 0, T)
    f2 = _from_row_tiles(gbuf.at[slot], T, T)
    y = info[:, INFO_P1:INFO_P1 + 1] * f1 + info[:, INFO_P2:INFO_P2 + 1] * f2
    m = _mod_rows(mod_ref, False, tiles_per_seq, prompt_row0)
    o_ref[...] = x_ref[...] + m[:, 2 * D:3 * D] * _rms(y, gpost_ref[...])


def _moe_combine(dest, f_rows, x, mod, info, gpost, tm, tiles_per_seq, prompt_row0):
    N, D = x.shape
    row = lambda i, d: (i, 0)
    fixed = lambda i, d: (0, 0)
    grid_spec = pltpu.PrefetchScalarGridSpec(
        num_scalar_prefetch=1,
        grid=(N // tm,),
        in_specs=[pl.BlockSpec(memory_space=pl.ANY), pl.BlockSpec((tm, D), row),
                  pl.BlockSpec(mod.shape, fixed), pl.BlockSpec((tm, LANES), row), pl.BlockSpec((1, D), fixed)],
        out_specs=pl.BlockSpec((tm, D), row),
        scratch_shapes=[pltpu.VMEM((2, 2 * tm * ROW_TILE, LANES), F32), pltpu.SemaphoreType.DMA((2,))],
    )
    return pl.pallas_call(
        functools.partial(_combine_body, tiles_per_seq, prompt_row0),
        out_shape=jax.ShapeDtypeStruct((N, D), F32),
        grid_spec=grid_spec,
        compiler_params=_params(("arbitrary",)),
        name="moe_combine",
    )(dest, f_rows, x, mod, info, gpost)


def _moe_prompt(x, mod, gpre, gpost, fw, tm, tiles_per_seq, prompt_row0, tf):
    N = x.shape[0]
    TM = EXPERT_TILE
    n_tiles = (2 * N) // TM + N_EXPERTS
    h_rows, info, cnt = _moe_route(x, mod, gpre, fw["rw"], fw["rb"], tm, tiles_per_seq, prompt_row0)
    e1 = info[:, INFO_E1].astype(jnp.int32)
    e2 = info[:, INFO_E2].astype(jnp.int32)
    counts = cnt[0, :N_EXPERTS].astype(jnp.int32)
    padded = ((counts + TM - 1) // TM) * TM
    ends = jnp.cumsum(padded)
    starts = ends - padded
    dest1 = starts[e1] + info[:, INFO_POS1].astype(jnp.int32)
    dest2 = starts[e2] + info[:, INFO_POS2].astype(jnp.int32)
    src_tok = _moe_invert(jnp.concatenate([dest1, dest2]), n_tiles * TM)
    tile_start = jnp.arange(n_tiles, dtype=jnp.int32) * TM
    tile_expert = jnp.minimum(jnp.sum(ends[None, :] <= tile_start[:, None], axis=1), N_EXPERTS - 1).astype(jnp.int32)
    n_valid = (ends[-1:] // TM).astype(jnp.int32)
    f_rows = _moe_experts(h_rows, tile_expert + fw["w0"], n_valid, src_tok, fw["wg"], fw["wu"], fw["wd"], tf)
    dest = jnp.concatenate([dest1.reshape(-1, 1, tm), dest2.reshape(-1, 1, tm)], axis=1).reshape(-1)
    return _moe_combine(dest, f_rows, x, mod, info, gpost, tm, tiles_per_seq, prompt_row0)


def _layer_weights(p, l):
    W = RWKV_WIDTH
    heads = np.arange(W) // HEAD_DIM
    headsum = jnp.asarray((heads[:, None] == heads[None, :]).astype(np.float32))
    perm = _q_perm()
    w_in = p["w_in"][l]
    w_in = jnp.concatenate([w_in[:, :RWKV_COLS], w_in[:, RWKV_COLS + perm], w_in[:, RWKV_COLS + SWA_WIDTH:]], axis=1)
    w_out = p["w_out"][l]
    zeros = jnp.zeros((DECAY_LORA, W), F32)
    row = lambda t: t.reshape(1, -1)
    return {
        "w_in": w_in.astype(BF16),
        "w_out_r": w_out[:W].astype(BF16),
        "w_out_a": w_out[W + perm].astype(BF16),
        "mu": row(p["mu_shift"][l]),
        "wd": jnp.concatenate([p["w_decay_up"][l], zeros], axis=0),
        "wi": jnp.concatenate([zeros, p["w_iclr_up"][l]], axis=0),
        "wg": p["w_gate_up"][l],
        "dbase": row(p["decay_base"][l]), "ibase": row(p["iclr_base"][l]),
        "kk": row(p["k_k"][l]), "ka": row(p["k_a"][l]), "rk": row(p["r_k"][l]),
        "lnw": row(p["lnx_w"][l]), "lnb": row(p["lnx_b"][l]),
        "headsum": headsum.astype(BF16), "headmean": (headsum / HEAD_DIM).astype(BF16),
        "g_pre0": row(p["norm_pre"][l, 0]), "g_pre1": row(p["norm_pre"][l, 1]),
        "g_post0": row(p["norm_post"][l, 0]), "g_post1": row(p["norm_post"][l, 1]),
    }


def _stacked(w, dtype):
    return w.astype(dtype).reshape((-1,) + w.shape[-2:])


def _ffn_weights(p, stacks, l):
    i = l // 2
    if l % 2 == 0:
        wg, wu, wd = stacks["dense"]
        return dict(moe=False, rw=jnp.zeros((D_MODEL, LANES), F32), rb=jnp.zeros((1, LANES), F32),
                    wg=wg, wu=wu, wd=wd, w0=i, n=1)
    rw = jnp.zeros((D_MODEL, LANES), F32).at[:, :N_EXPERTS].set(p["router_w"][i])
    rb = jnp.full((1, LANES), NEG_INF, F32).at[0, :N_EXPERTS].set(p["router_b"][i])
    wg, wu, wd = stacks["moe"]
    return dict(moe=True, rw=rw, rb=rb, wg=wg, wu=wu, wd=wd, w0=i * N_EXPERTS, n=N_EXPERTS)


def _pick_tile(n, pref):
    t = min(pref, n)
    while n % t:
        t //= 2
    return t


def _ffn_tile(f):
    for t in (1408, 896, 512, 256, 128):
        if f % t == 0:
            return t
    return f


def _trunk(x3, mods, lws, fws, p, prompt, prompt_row0, state=None):
    B, T, D = x3.shape
    N = B * T
    x = x3.reshape(N, D)
    per_row = not prompt
    tm = _pick_tile(T if prompt else N, 512)
    tps = (T // tm) if prompt else 1
    depth = len(lws)
    wkv_out, shift_out, k_out, v_out = [], [], [], []
    for l in range(depth):
        lw, fw = lws[l], fws[l]
        mod0, mod1 = mods[2 * l], mods[2 * l + 1]
        pr, q, k, v = _inproj(x, mod0, lw["g_pre0"], lw["w_in"], per_row, tm, tps, prompt_row0)
        if prompt:
            r, ld, kh, vv, a, b, g, bonus = _prep_prompt(pr, lw, tm, tps)
            yw, hbd = _wkv_prompt(r, ld, kh, vv, a, b, B, T)
            n_pairs = RWKV_WIDTH // PAIR
            hb = hbd.reshape(B, n_pairs, 2, HEAD_DIM, 2, HEAD_DIM)
            s_kv = jnp.stack([hb[:, :, 0, :, 0, :], hb[:, :, 1, :, 1, :]], axis=2)
            s_new = jnp.swapaxes(s_kv.reshape(B, RWKV_HEADS, HEAD_DIM, HEAD_DIM), -1, -2)
            ya = _swa_prompt(q, k, v, p["rel_bias"], p["attn_sinks"][l], B, T)
            window = lambda t: t.reshape(B, T, KV_WIDTH)[:, -WINDOW:].reshape(B, WINDOW, SWA_KV_HEADS, HEAD_DIM)
            kb, vb = window(k), window(v)
            last = pr.reshape(B, T, RWKV_COLS)[:, -1]
        else:
            r, ld, kh, vv, a, b, g, bonus = _prep_sample(pr, state["shift"][l], lw)
            yw, s_new = _wkv_sample(state["wkv"][l], r, ld, kh, vv, a, b)
            ya, kb, vb = _swa_sample(q, k, v, state["k"][l], state["v"][l], p["rel_bias"], p["attn_sinks"][l])
            last = pr
        x = _outproj(x, mod0, yw, g, bonus, ya, lw, per_row, tm, tps, prompt_row0)
        tf = _ffn_tile(fw["wg"].shape[-1])
        if prompt and fw["moe"]:
            x = _moe_prompt(x, mod1, lw["g_pre1"], lw["g_post1"], fw, tm, tps, prompt_row0, 2 * tf)
        else:
            x = _ffn(x, mod1, lw["g_pre1"], lw["g_post1"], fw["rw"], fw["rb"], fw["wg"], fw["wu"], fw["wd"],
                     fw["w0"], fw["n"], fw["moe"], per_row, tm, tps, prompt_row0, tf)
        wkv_out.append(s_new)
        shift_out.append(last)
        k_out.append(kb)
        v_out.append(vb)
    return x.reshape(B, T, D), jnp.stack(wkv_out), jnp.stack(shift_out), jnp.stack(k_out), jnp.stack(v_out)


def _forward(x_prompt, x_sample, c_prompt, c_sample, state_wkv, state_shift, cache_swa_k, cache_swa_v, p):
    depth = p["w_in"].shape[0]
    Bp, Bs = c_prompt.shape[0], c_sample.shape[0]
    D = D_MODEL
    pad = (-(Bs + Bp)) % SUBLANES
    c_all = jnp.concatenate([c_sample, c_prompt, jnp.zeros((pad, D), F32)], axis=0)
    mods = _ada_all(c_all, p["ada_w"].reshape(2 * depth, D, 3 * D), p["ada_b"].reshape(2 * depth, 1, 3 * D))
    lws = [_layer_weights(p, l) for l in range(depth)]
    stacks = {"dense": tuple(_stacked(p[k], BF16) for k in ("ffn_w_gate", "ffn_w_up", "ffn_w_down")),
              "moe": tuple(_stacked(p[k], BF16) for k in ("moe_w_gate", "moe_w_up", "moe_w_down"))}
    fws = [_ffn_weights(p, stacks, l) for l in range(depth)]
    y_p, wkv_p, shift_p, k_p, v_p = _trunk(x_prompt, mods, lws, fws, p, True, Bs)
    state = {"wkv": state_wkv, "shift": state_shift, "k": cache_swa_k, "v": cache_swa_v}
    y_s, wkv_s, shift_s, k_s, v_s = _trunk(x_sample, mods, lws, fws, p, False, Bs, state)
    return (y_p, y_s, wkv_p, shift_p, k_p, v_p, wkv_s, shift_s, k_s, v_s)


def kernel(x_prompt, x_sample, c_prompt, c_sample, state_wkv, state_shift, cache_swa_k, cache_swa_v, rel_bias, ada_w, ada_b, norm_pre, norm_post, w_in, mu_shift, w_decay_up, decay_base, w_iclr_up, iclr_base, w_gate_up, k_k, k_a, r_k, lnx_w, lnx_b, attn_sinks, w_out, ffn_w_gate, ffn_w_up, ffn_w_down, router_w, router_b, moe_w_gate, moe_w_up, moe_w_down):
    p = {"rel_bias": rel_bias, "ada_w": ada_w, "ada_b": ada_b, "norm_pre": norm_pre, "norm_post": norm_post,
         "w_in": w_in, "mu_shift": mu_shift, "w_decay_up": w_decay_up, "decay_base": decay_base,
         "w_iclr_up": w_iclr_up, "iclr_base": iclr_base, "w_gate_up": w_gate_up, "k_k": k_k, "k_a": k_a,
         "r_k": r_k.reshape(r_k.shape[0], -1), "lnx_w": lnx_w, "lnx_b": lnx_b, "attn_sinks": attn_sinks,
         "w_out": w_out, "ffn_w_gate": ffn_w_gate, "ffn_w_up": ffn_w_up, "ffn_w_down": ffn_w_down,
         "router_w": router_w, "router_b": router_b, "moe_w_gate": moe_w_gate, "moe_w_up": moe_w_up,
         "moe_w_down": moe_w_down}
    return _forward(x_prompt, x_sample, c_prompt, c_sample, state_wkv, state_shift, cache_swa_k, cache_swa_v, p)
```

```python
import functools

import numpy as np
import jax
import jax.numpy as jnp
from jax import lax
from jax.experimental import pallas as pl
from jax.experimental.pallas import tpu as pltpu

F32 = jnp.float32
BF16 = jnp.bfloat16

D_MODEL = 1024
HEAD_DIM = 64
RWKV_WIDTH = 512
RWKV_HEADS = RWKV_WIDTH // HEAD_DIM
SWA_WIDTH = 512
SWA_HEADS = SWA_WIDTH // HEAD_DIM
SWA_KV_HEADS = 2
SWA_GROUP = SWA_HEADS // SWA_KV_HEADS
KV_WIDTH = SWA_KV_HEADS * HEAD_DIM
WINDOW = 128
DECAY_LORA = 64
ICLR_LORA = 64
GATE_LORA = 128
RWKV_COLS = 3 * RWKV_WIDTH + DECAY_LORA + ICLR_LORA + GATE_LORA
IN_COLS = RWKV_COLS + SWA_WIDTH + 2 * KV_WIDTH
LORA_OFF = 3 * RWKV_WIDTH
GATE_OFF = LORA_OFF + DECAY_LORA + ICLR_LORA
LN_X_EPS = 64e-5
RMS_EPS = 1e-6
N_BUCKETS = 32
MAX_DISTANCE = 128
N_EXPERTS = 8
NEG_INF = -1e30

LANES = 128
SUBLANES = 8
VMEM_LIMIT = 56 * 1024 * 1024

WKV_CHUNK = 64
PAIR = 2 * HEAD_DIM

NN = (((1,), (0,)), ((), ()))
NT = (((1,), (1,)), ((), ()))
TN = (((0,), (0,)), ((), ()))


def _dot(a, b, dims=NN):
    return lax.dot_general(a.astype(BF16), b.astype(BF16), dims, preferred_element_type=F32)


def _split(x, pieces):
    out = []
    for _ in range(pieces - 1):
        hi = x.astype(BF16)
        out.append(hi)
        x = x - hi.astype(F32)
    out.append(x.astype(BF16))
    return out


def _select_dot(x, sel, pieces=3, sel_left=False):
    d = lambda t: lax.dot_general(*((sel, t) if sel_left else (t, sel)), NN, preferred_element_type=F32)
    return sum(d(t) for t in _split(x, pieces))


def _dot3(a, b):
    a_hi, a_lo = _split(a, 2)
    b_hi, b_lo = _split(b, 2)
    d = lambda x, y: lax.dot_general(x, y, NN, preferred_element_type=F32)
    return d(a_hi, b_hi) + d(a_lo, b_hi) + d(a_hi, b_lo)


def _sigmoid(x):
    return 1.0 / (1.0 + jnp.exp(-x))


def _params(sem):
    return pltpu.CompilerParams(dimension_semantics=sem, vmem_limit_bytes=VMEM_LIMIT)


def _rms(x, g):
    return x * lax.rsqrt(jnp.mean(x * x, axis=-1, keepdims=True) + RMS_EPS) * g


def _mod_rows(mod_ref, per_row, tiles_per_seq, prompt_row0):
    if per_row:
        return mod_ref[...]
    b = pl.program_id(0) // tiles_per_seq
    return mod_ref[pl.ds(prompt_row0 + b, 1), :]


def _ada_body(c_ref, w_ref, b_ref, o_ref):
    c = c_ref[...]
    o_ref[...] = _dot(c * _sigmoid(c), w_ref[...]) + b_ref[...]


def _ada_all(c_all, ada_w, ada_b):
    R, D = c_all.shape
    n = ada_w.shape[0]
    tn = 1024
    return pl.pallas_call(
        _ada_body,
        out_shape=jax.ShapeDtypeStruct((n, R, 3 * D), F32),
        grid=(n, 3 * D // tn),
        in_specs=[pl.BlockSpec((R, D), lambda i, j: (0, 0)),
                  pl.BlockSpec((None, D, tn), lambda i, j: (i, 0, j)),
                  pl.BlockSpec((None, 1, tn), lambda i, j: (i, 0, j))],
        out_specs=pl.BlockSpec((None, R, tn), lambda i, j: (i, 0, j)),
        compiler_params=_params(("arbitrary", "arbitrary")),
        name="ada_mod",
    )(c_all, ada_w, ada_b)


def _inproj_body(per_row, tiles_per_seq, prompt_row0, x_ref, mod_ref, g_ref, w_ref, pr_ref, q_ref, k_ref, v_ref):
    D = D_MODEL
    m = _mod_rows(mod_ref, per_row, tiles_per_seq, prompt_row0)
    h = _rms(x_ref[...], g_ref[...]) * (1.0 + m[:, D:2 * D]) + m[:, 0:D]
    proj = _dot(h, w_ref[...])
    pr_ref[...] = proj[:, 0:RWKV_COLS]
    q_ref[...] = proj[:, RWKV_COLS:RWKV_COLS + SWA_WIDTH]
    k_ref[...] = proj[:, RWKV_COLS + SWA_WIDTH:RWKV_COLS + SWA_WIDTH + KV_WIDTH]
    v_ref[...] = proj[:, RWKV_COLS + SWA_WIDTH + KV_WIDTH:IN_COLS]


def _mod_spec(mod, per_row, tm, nargs):
    R = mod.shape[0]
    if per_row:
        return pl.BlockSpec((tm, 3 * D_MODEL), lambda i, *_: (0, 0))
    return pl.BlockSpec((R, 3 * D_MODEL), lambda i, *_: (0, 0))


def _inproj(x, mod, g, w, per_row, tm, tiles_per_seq, prompt_row0):
    N, D = x.shape
    row = lambda i: (i, 0)
    fixed = lambda i: (0, 0)
    return pl.pallas_call(
        functools.partial(_inproj_body, per_row, tiles_per_seq, prompt_row0),
        out_shape=(jax.ShapeDtypeStruct((N, RWKV_COLS), F32), jax.ShapeDtypeStruct((N, SWA_WIDTH), F32),
                   jax.ShapeDtypeStruct((N, KV_WIDTH), F32), jax.ShapeDtypeStruct((N, KV_WIDTH), F32)),
        grid=(N // tm,),
        in_specs=[pl.BlockSpec((tm, D), row), _mod_spec(mod, per_row, tm, 1),
                  pl.BlockSpec((1, D), fixed), pl.BlockSpec((D, IN_COLS), fixed)],
        out_specs=(pl.BlockSpec((tm, RWKV_COLS), row), pl.BlockSpec((tm, SWA_WIDTH), row),
                   pl.BlockSpec((tm, KV_WIDTH), row), pl.BlockSpec((tm, KV_WIDTH), row)),
        compiler_params=_params(("arbitrary",)),
        name="in_proj",
    )(x, mod, g, w)


PREP_KEYS = ("mu", "wd", "wi", "wg", "dbase", "ibase", "kk", "ka", "rk", "headsum")
POST_KEYS = ("lnw", "lnb", "headmean")


def _prep_specs(fixed):
    W = RWKV_WIDTH
    vec = pl.BlockSpec((1, W), fixed)
    return [pl.BlockSpec((1, RWKV_COLS), fixed), pl.BlockSpec((LANES, W), fixed), pl.BlockSpec((LANES, W), fixed),
            pl.BlockSpec((GATE_LORA, W), fixed), vec, vec, vec, vec, vec, pl.BlockSpec((W, W), fixed)]


def _rwkv_features(pr, shifted, mu_ref, wd_ref, wi_ref, wg_ref, dbase_ref, ibase_ref, kk_ref, ka_ref, rk_ref, hs_ref):
    W = RWKV_WIDTH
    xs = pr + (shifted - pr) * mu_ref[...]
    r = xs[:, 0:W]
    k = xs[:, W:2 * W]
    v = xs[:, 2 * W:3 * W]
    lora = xs[:, LORA_OFF:GATE_OFF]
    gl = xs[:, GATE_OFF:RWKV_COLS]
    z = dbase_ref[...] + _dot3(jnp.tanh(lora), wd_ref[...])
    ld = -float(np.exp(-0.5)) * _sigmoid(z)
    iclr = _sigmoid(ibase_ref[...] + _dot3(lora, wi_ref[...]))
    g = _dot3(_sigmoid(gl), wg_ref[...])
    hs = hs_ref[...]
    kk = k * kk_ref[...]
    kk = kk / jnp.maximum(jnp.sqrt(_select_dot(kk * kk, hs, 2)), 1e-12)
    kh = k * (1.0 + (iclr - 1.0) * ka_ref[...])
    bonus = _select_dot(r * kh * rk_ref[...], hs, 2) * v
    return r, ld, kh, v, -kk, kk * iclr, g, bonus


def _rwkv_post(y, g, bonus, lnw, lnb, hm):
    mean = _select_dot(y, hm, 2)
    dv = y - mean
    var = _select_dot(dv * dv, hm, 2)
    return (dv * lax.rsqrt(var + LN_X_EPS) * lnw + lnb + bonus) * g


def _prep_body(pr_ref, prev_ref, *refs):
    outs = refs[len(PREP_KEYS):]
    for ref, val in zip(outs, _rwkv_features(pr_ref[...], prev_ref[...], *refs[:len(PREP_KEYS)])):
        ref[...] = val


def _prep_prompt_body(tiles_per_seq, pr_ref, prev_ref, *refs):
    pr = pr_ref[...]
    first = (pl.program_id(0) % tiles_per_seq) == 0
    carry = jnp.where(first, 0.0, prev_ref[SUBLANES - 1:SUBLANES, :])
    shifted = jnp.where(lax.broadcasted_iota(jnp.int32, pr.shape, 0) == 0, carry, pltpu.roll(pr, 1, 0))
    for ref, val in zip(refs[len(PREP_KEYS):], _rwkv_features(pr, shifted, *refs[:len(PREP_KEYS)])):
        ref[...] = val


def _prep_prompt(pr, lw, tm, tiles_per_seq):
    N = pr.shape[0]
    W = RWKV_WIDTH
    row = lambda i: (i, 0)
    fixed = lambda i: (0, 0)
    per = tm // SUBLANES
    out = jax.ShapeDtypeStruct((N, W), F32)
    return pl.pallas_call(
        functools.partial(_prep_prompt_body, tiles_per_seq),
        out_shape=(out,) * 8,
        grid=(N // tm,),
        in_specs=[pl.BlockSpec((tm, RWKV_COLS), row),
                  pl.BlockSpec((SUBLANES, RWKV_COLS), lambda i: (jnp.maximum(i * per - 1, 0), 0))] + _prep_specs(fixed),
        out_specs=(pl.BlockSpec((tm, W), row),) * 8,
        compiler_params=_params(("arbitrary",)),
        name="rwkv_prep",
    )(pr, pr, *[lw[k] for k in PREP_KEYS])


def _prep_sample(pr, prev, lw):
    N = pr.shape[0]
    W = RWKV_WIDTH
    row = lambda i: (i, 0)
    fixed = lambda i: (0, 0)
    out = jax.ShapeDtypeStruct((N, W), F32)
    return pl.pallas_call(
        _prep_body,
        out_shape=(out,) * 8,
        grid=(1,),
        in_specs=[pl.BlockSpec((N, RWKV_COLS), row), pl.BlockSpec((N, RWKV_COLS), row)] + _prep_specs(fixed),
        out_specs=(pl.BlockSpec((N, W), row),) * 8,
        compiler_params=_params(("arbitrary",)),
        name="rwkv_prep",
    )(pr, prev, *[lw[k] for k in PREP_KEYS])


def _stack_heads(x, lane_head0):
    return jnp.concatenate([jnp.where(lane_head0, x, 0.0), jnp.where(lane_head0, 0.0, x)], axis=0)


def _fold_heads(x):
    c = x.shape[0] // 2
    return x[0:c] + x[c:2 * c]


def _dots(xs, ys, dims=NN):
    return [_dot(x, y, dims) for x, y in zip(xs, ys)]


def _unit_lower_inverse(ns, same16, eye):
    nd = [jnp.where(same16, n, 0.0) for n in ns]
    no = [n - d for n, d in zip(ns, nd)]
    n2 = _dots(nd, nd)
    n4 = _dots(n2, n2)
    n8 = _dots(n4, n4)
    td = [eye + d for d in nd]
    for pw in (n2, n4, n8):
        td = [t + u for t, u in zip(td, _dots(td, pw))]
    q = _dots(td, no)
    q2 = _dots(q, q)
    z = [eye + x for x in q]
    z = [t + u for t, u in zip(z, _dots(z, q2))]
    return _dots(z, td)


def _wkv_chunk_body(chunks, r_all, ld_all, k_all, v_all_, a_all, b_all, y_ref, s_ref, h_ref):
    C = WKV_CHUNK
    n_pairs = RWKV_WIDTH // PAIR

    @pl.when(pl.program_id(1) == 0)
    def _():
        h_ref[...] = jnp.zeros_like(h_ref)

    ri = lax.broadcasted_iota(jnp.int32, (PAIR, PAIR), 0)
    ci = lax.broadcasted_iota(jnp.int32, (PAIR, PAIR), 1)
    same_head = (ri // C) == (ci // C)
    strict_lower = same_head & (ci < ri)
    incl_lower = same_head & (ci <= ri)
    same16 = (ri // 16) == (ci // 16)
    eye_b = ri == ci
    eye = jnp.where(eye_b, 1.0, 0.0)
    tri = jnp.where(lax.broadcasted_iota(jnp.int32, (C, C), 1) <= lax.broadcasted_iota(jnp.int32, (C, C), 0),
                    1.0, 0.0).astype(BF16)
    lane_head0 = lax.broadcasted_iota(jnp.int32, (C, PAIR), 1) < HEAD_DIM
    zeros = jnp.zeros((PAIR, PAIR), F32)

    cat0 = lambda x, y: jnp.concatenate([x, y], axis=0)
    cat1 = lambda x, y: jnp.concatenate([x, y], axis=1)

    xa, xr, v_st, bh_st, kh_st, yb, yk, p_all = [], [], [], [], [], [], [], []
    for c in range(chunks):
        rows = slice(c * C, (c + 1) * C)
        ld = ld_all[rows, :]
        cum = _select_dot(ld, tri, 3, sel_left=True)
        last = cum[C - 1:C, :]
        p_inv = jnp.exp(-cum)
        p_tail = jnp.exp(last - cum)
        p_end = jnp.exp(last)
        a_t = a_all[rows, :] * jnp.exp(cum - ld)
        r_t = r_all[rows, :] * jnp.exp(cum)
        b_raw = b_all[rows, :]
        k_raw = k_all[rows, :]
        b_t = b_raw * p_inv
        k_t = k_raw * p_inv
        b_h = b_raw * p_tail
        k_h = k_raw * p_tail
        v_all = v_all_[rows, :]
        for j in range(n_pairs):
            lanes = slice(j * PAIR, (j + 1) * PAIR)
            xa.append(_stack_heads(a_t[:, lanes], lane_head0))
            xr.append(_stack_heads(r_t[:, lanes], lane_head0))
            v_st.append(_stack_heads(v_all[:, lanes], lane_head0))
            bh_st.append(_stack_heads(b_h[:, lanes], lane_head0))
            kh_st.append(_stack_heads(k_h[:, lanes], lane_head0))
            yb.append(cat0(b_t[:, lanes], b_t[:, lanes]))
            yk.append(cat0(k_t[:, lanes], k_t[:, lanes]))
            p_all.append(p_end[:, lanes])

    gram = _dots([cat0(x, y) for x, y in zip(xa, xr)], [cat0(x, y) for x, y in zip(yb, yk)], NT)
    n_mat = [jnp.where(strict_lower, g[0:PAIR, 0:PAIR], 0.0) for g in gram]
    m_mat = [jnp.where(strict_lower, g[0:PAIR, PAIR:2 * PAIR], 0.0) for g in gram]
    a_rbk = [cat1(jnp.where(incl_lower, g[PAIR:2 * PAIR, 0:PAIR], 0.0),
                  jnp.where(incl_lower, g[PAIR:2 * PAIR, PAIR:2 * PAIR], 0.0)) for g in gram]
    t_inv = _unit_lower_inverse(n_mat, same16, eye)
    mv = _dots(m_mat, v_st)
    tx = _dots(t_inv, [cat1(x, y) for x, y in zip(xa, mv)])
    rhs = [cat0(t, cat1(zeros, v)) for t, v in zip(tx, v_st)]
    ry = _dots(a_rbk, rhs)
    pp = _dots([cat0(x, y) for x, y in zip(bh_st, kh_st)], rhs, TN)

    for c in range(chunks):
        us = [c * n_pairs + j for j in range(n_pairs)]
        h0 = [h_ref[j] for j in range(n_pairs)]
        r_bar = [_fold_heads(xr[u] + ry[u][:, 0:PAIR]) for u in us]
        phi = [pp[u][:, 0:PAIR] + jnp.where(eye_b, p_all[u], 0.0) for u in us]
        ys = _dots(r_bar, h0)
        hs = _dots(phi, h0)
        for j, u in enumerate(us):
            y_ref[c * C:(c + 1) * C, j * PAIR:(j + 1) * PAIR] = ys[j] + _fold_heads(ry[u][:, PAIR:2 * PAIR])
            h_ref[j] = hs[j] + pp[u][:, PAIR:2 * PAIR]

    @pl.when(pl.program_id(1) == pl.num_programs(1) - 1)
    def _():
        s_ref[...] = h_ref[...]


def _wkv_prompt(r, ld, k, v, a, b, batch, seq):
    N, W = r.shape
    chunks = 4
    tt = chunks * WKV_CHUNK
    steps = seq // tt
    n_pairs = W // PAIR
    row = lambda bb, t: (bb * steps + t, 0)
    spec = pl.BlockSpec((tt, W), row)
    return pl.pallas_call(
        functools.partial(_wkv_chunk_body, chunks),
        out_shape=(jax.ShapeDtypeStruct((N, W), F32), jax.ShapeDtypeStruct((batch, n_pairs, PAIR, PAIR), F32)),
        grid=(batch, steps),
        in_specs=[spec] * 6,
        out_specs=(spec, pl.BlockSpec((None, n_pairs, PAIR, PAIR), lambda bb, t: (bb, 0, 0, 0))),
        scratch_shapes=[pltpu.VMEM((n_pairs, PAIR, PAIR), F32)],
        compiler_params=_params(("arbitrary", "arbitrary")),
        name="wkv_chunk_scan",
    )(r, ld, k, v, a, b)


def _wkv_step_body(s_ref, r_ref, ld_ref, k_ref, v_ref, a_ref, b_ref, exp_ref, red_ref, y_ref, so_ref):
    HD = HEAD_DIM
    nrep = HD * HD // LANES
    lane = lax.broadcasted_iota(jnp.int32, r_ref.shape, 1)
    low = lane < HD
    y = jnp.zeros(r_ref.shape, F32)
    for hh in range(2):
        def tiled(ref, fn=None):
            x = ref[...]
            if fn is not None:
                x = fn(x)
            sw = pltpu.roll(x, HD, 1)
            both = jnp.where(low, x, sw) if hh == 0 else jnp.where(low, sw, x)
            return jnp.tile(both, (1, nrep))
        cols = slice(hh * HD * HD, (hh + 1) * HD * HD)
        s = s_ref[:, cols]
        expand = exp_ref[hh]
        reduce_ = red_ref[hh]
        sa = _select_dot(s * tiled(a_ref), reduce_)
        s_new = (s * tiled(ld_ref, jnp.exp) + _select_dot(sa, expand) * tiled(b_ref)
                 + _select_dot(v_ref[...], expand) * tiled(k_ref))
        so_ref[:, cols] = s_new
        y = y + _select_dot(s_new * tiled(r_ref), reduce_)
    y_ref[...] = y


def _wkv_step_consts():
    HD = HEAD_DIM
    expand = np.zeros((2, PAIR, HD * HD), np.float32)
    for hh in range(2):
        for vv in range(HD):
            expand[hh, hh * HD + vv, vv * HD:(vv + 1) * HD] = 1.0
    return jnp.asarray(expand, BF16), jnp.asarray(expand.transpose(0, 2, 1), BF16)


def _wkv_sample(state, r, ld, k, v, a, b):
    B = state.shape[0]
    W = RWKV_WIDTH
    HD2 = HEAD_DIM * HEAD_DIM
    expand, reduce_ = _wkv_step_consts()
    s2 = state.reshape(B, RWKV_HEADS * HD2)
    st_spec = pl.BlockSpec((B, 2 * HD2), lambda j: (0, j))
    vec = pl.BlockSpec((B, PAIR), lambda j: (0, j))
    y, s_new = pl.pallas_call(
        _wkv_step_body,
        out_shape=(jax.ShapeDtypeStruct((B, W), F32), jax.ShapeDtypeStruct((B, RWKV_HEADS * HD2), F32)),
        grid=(W // PAIR,),
        in_specs=[st_spec] + [vec] * 6 + [pl.BlockSpec((2, PAIR, HD2), lambda j: (0, 0, 0)),
                                          pl.BlockSpec((2, HD2, PAIR), lambda j: (0, 0, 0))],
        out_specs=(vec, st_spec),
        compiler_params=_params(("arbitrary",)),
        name="wkv_step",
    )(s2, r, ld, k, v, a, b, expand, reduce_)
    return y, s_new.reshape(state.shape)


def _q_perm():
    return np.array([(h * SWA_GROUP + g) * HEAD_DIM + d for g in range(SWA_GROUP) for h in range(SWA_KV_HEADS)
                     for d in range(HEAD_DIM)], np.int32)


def _rel_bucket_np(dist):
    max_exact = N_BUCKETS // 2
    d = np.maximum(dist, 0)
    ratio = np.log(np.maximum(d, 1).astype(np.float32) / np.float32(max_exact)) / np.float32(
        np.log(MAX_DISTANCE / max_exact))
    large = np.minimum(max_exact + (ratio.astype(np.float32) * np.float32(N_BUCKETS - max_exact)).astype(np.int32),
                       N_BUCKETS - 1)
    return np.where(d < max_exact, d, large).astype(np.int32)


def _bias_from_buckets(idx, rb_ref, head):
    acc = jnp.zeros(idx.shape, F32)
    for bk in range(N_BUCKETS):
        acc = jnp.where(idx == bk, rb_ref[bk, head], acc)
    return acc


def _swa_prompt_body(rb_ref, sink_ref, idx_ref, q_ref, kp_ref, kc_ref, vp_ref, vc_ref, o_ref, bias_ref):
    Q = WINDOW
    first = (pl.program_id(0) == 0) & (pl.program_id(1) == 0)

    @pl.when(first)
    def _():
        idx = idx_ref[...]
        qi = lax.broadcasted_iota(jnp.int32, (Q, 2 * Q), 0)
        kj = lax.broadcasted_iota(jnp.int32, (Q, 2 * Q), 1)
        in_window = ((kj < Q) & (kj >= qi)) | ((kj >= Q) & ((kj - Q) <= qi))
        for g in range(SWA_GROUP):
            for h in range(SWA_KV_HEADS):
                bias = jnp.where(in_window, _bias_from_buckets(idx, rb_ref, h * SWA_GROUP + g), NEG_INF)
                bias_ref[0, g * SWA_KV_HEADS + h] = bias
                bias_ref[1, g * SWA_KV_HEADS + h] = jnp.where(kj < Q, NEG_INF, bias)

    lane_kv0 = lax.broadcasted_iota(jnp.int32, (Q, LANES), 1) < HEAD_DIM
    scale = HEAD_DIM ** -0.5
    blocks = q_ref.shape[0] // Q
    k_all = jnp.concatenate([kp_ref[...], kc_ref[...]], axis=0).astype(BF16)
    v_all = jnp.concatenate([vp_ref[...], vc_ref[...]], axis=0).astype(BF16)
    for j in range(blocks):
        rows = slice(j * Q, (j + 1) * Q)
        table = jnp.where(pl.program_id(1) > 0, 0, 1) if j == 0 else 0
        kcat = k_all[j * Q:(j + 2) * Q]
        vcat = v_all[j * Q:(j + 2) * Q]
        for g in range(SWA_GROUP):
            qg = q_ref[rows, g * LANES:(g + 1) * LANES] * scale
            outs = []
            for h in range(SWA_KV_HEADS):
                qm = jnp.where(lane_kv0, qg, 0.0) if h == 0 else jnp.where(lane_kv0, 0.0, qg)
                s = _dot(qm, kcat, NT) + bias_ref[table, g * SWA_KV_HEADS + h]
                sink = sink_ref[h * SWA_GROUP + g]
                m = jnp.maximum(jnp.max(s, axis=-1, keepdims=True), sink)
                p = jnp.exp(s - m)
                den = jnp.sum(p, axis=-1, keepdims=True) + jnp.exp(sink - m)
                outs.append(_dot(p, vcat) / den)
            o_ref[rows, g * LANES:(g + 1) * LANES] = jnp.where(lane_kv0, outs[0], outs[1])


def _swa_prompt(q, k, v, rel_bias, sinks, batch, seq):
    N = q.shape[0]
    Q = WINDOW
    per_step = _pick_tile(seq // Q, 4)
    nb = seq // (Q * per_step)
    qi = np.arange(Q)[:, None]
    kj = np.arange(2 * Q)[None, :]
    idx = jnp.asarray(_rel_bucket_np(qi + Q - kj))
    cur = lambda bb, n: (bb * nb + n, 0)
    prev = lambda bb, n: (jnp.maximum((bb * nb + n) * per_step - 1, 0), 0)
    kv_c = pl.BlockSpec((per_step * Q, KV_WIDTH), cur)
    kv_p = pl.BlockSpec((Q, KV_WIDTH), prev)
    smem = pl.BlockSpec(memory_space=pltpu.SMEM)
    return pl.pallas_call(
        _swa_prompt_body,
        out_shape=jax.ShapeDtypeStruct((N, SWA_WIDTH), F32),
        grid=(batch, nb),
        in_specs=[smem, smem, pl.BlockSpec((Q, 2 * Q), lambda bb, n: (0, 0)),
                  pl.BlockSpec((per_step * Q, SWA_WIDTH), cur), kv_p, kv_c, kv_p, kv_c],
        out_specs=pl.BlockSpec((per_step * Q, SWA_WIDTH), cur),
        scratch_shapes=[pltpu.VMEM((2, SWA_HEADS, Q, 2 * Q), F32)],
        compiler_params=_params(("arbitrary", "arbitrary")),
        name="swa_prompt",
    )(rel_bias, sinks, idx, q, k, k, v, v)


def _swa_sample_body(rb_ref, sink_ref, idx_ref, q_ref, kn_ref, vn_ref, kb_ref, vb_ref, o_ref, ko_ref, vo_ref,
                     bias_ref, extra_ref):
    W = WINDOW
    G, KVH = SWA_GROUP, SWA_KV_HEADS

    @pl.when(pl.program_id(0) == 0)
    def _():
        idx = jnp.broadcast_to(idx_ref[...], (SUBLANES, W))
        row = lax.broadcasted_iota(jnp.int32, (SUBLANES, W), 0)
        acc = jnp.zeros((SUBLANES, W), F32)
        ext = jnp.zeros((SUBLANES, LANES), F32)
        lane = lax.broadcasted_iota(jnp.int32, (SUBLANES, LANES), 1)
        for h in range(KVH):
            for g in range(G):
                head = h * G + g
                r = h * G + g
                acc = jnp.where(row == r, _bias_from_buckets(idx, rb_ref, head), acc)
                ext = jnp.where((row == r) & (lane == 0), rb_ref[0, head], ext)
                ext = jnp.where((row == r) & (lane == 1), sink_ref[head], ext)
        bias_ref[...] = acc
        extra_ref[...] = ext

    TB = q_ref.shape[0]
    lane_kv0 = lax.broadcasted_iota(jnp.int32, (TB, G, LANES), 2) < HEAD_DIM
    q4 = q_ref[...]
    qrows = jnp.concatenate([jnp.where(lane_kv0, q4, 0.0), jnp.where(lane_kv0, 0.0, q4)], axis=1)
    kb = kb_ref[...]
    vb = vb_ref[...]
    kn = kn_ref[...]
    vn = vn_ref[...]
    scale = HEAD_DIM ** -0.5
    bdims = (((2,), (2,)), ((0,), (0,)))
    s = lax.dot_general(qrows.astype(BF16), kb.astype(BF16), bdims, preferred_element_type=F32) * scale
    s = s + bias_ref[...][None]
    s_self = jnp.sum(qrows * kn, axis=-1, keepdims=True) * scale + extra_ref[:, 0:1][None]
    sink = extra_ref[:, 1:2][None]
    m = jnp.maximum(jnp.maximum(jnp.max(s, axis=-1, keepdims=True), s_self), sink)
    p = jnp.exp(s - m)
    p_self = jnp.exp(s_self - m)
    den = jnp.sum(p, axis=-1, keepdims=True) + p_self + jnp.exp(sink - m)
    pv = lax.dot_general(p.astype(BF16), vb.astype(BF16), (((2,), (1,)), ((0,), (0,))), preferred_element_type=F32)
    o = (pv + p_self * vn) / den
    o_ref[...] = jnp.where(lane_kv0, o[:, 0:G], o[:, G:2 * G])

    rowmod = lax.broadcasted_iota(jnp.int32, (TB, W, KV_WIDTH), 1)
    for buf, new, out in ((kb, kn, ko_ref), (vb, vn, vo_ref)):
        rolled = pltpu.roll(buf.reshape(TB * W, KV_WIDTH), TB * W - 1, 0).reshape(TB, W, KV_WIDTH)
        out[...] = jnp.where(rowmod == W - 1, new, rolled)


def _swa_sample(q, k, v, kbuf, vbuf, rel_bias, sinks):
    B = q.shape[0]
    W = WINDOW
    tb = 16
    idx = jnp.asarray(_rel_bucket_np(W - np.arange(W))[None, :])
    q4 = q.reshape(B, SWA_GROUP, LANES)
    kn = k.reshape(B, 1, KV_WIDTH)
    vn = v.reshape(B, 1, KV_WIDTH)
    kb = kbuf.reshape(B, W, KV_WIDTH)
    vb = vbuf.reshape(B, W, KV_WIDTH)
    smem = pl.BlockSpec(memory_space=pltpu.SMEM)
    b3 = lambda i: (i, 0, 0)
    o, ko, vo = pl.pallas_call(
        _swa_sample_body,
        out_shape=(jax.ShapeDtypeStruct((B, SWA_GROUP, LANES), F32), jax.ShapeDtypeStruct((B, W, KV_WIDTH), F32),
                   jax.ShapeDtypeStruct((B, W, KV_WIDTH), F32)),
        grid=(B // tb,),
        in_specs=[smem, smem, pl.BlockSpec((1, W), lambda i: (0, 0)),
                  pl.BlockSpec((tb, SWA_GROUP, LANES), b3), pl.BlockSpec((tb, 1, KV_WIDTH), b3),
                  pl.BlockSpec((tb, 1, KV_WIDTH), b3), pl.BlockSpec((tb, W, KV_WIDTH), b3),
                  pl.BlockSpec((tb, W, KV_WIDTH), b3)],
        out_specs=(pl.BlockSpec((tb, SWA_GROUP, LANES), b3), pl.BlockSpec((tb, W, KV_WIDTH), b3),
                   pl.BlockSpec((tb, W, KV_WIDTH), b3)),
        scratch_shapes=[pltpu.VMEM((SUBLANES, W), F32), pltpu.VMEM((SUBLANES, LANES), F32)],
        compiler_params=_params(("arbitrary",)),
        name="swa_sample",
    )(rel_bias, sinks, idx, q4, kn, vn, kb, vb)
    return o.reshape(B, SWA_WIDTH), ko.reshape(kbuf.shape), vo.reshape(vbuf.shape)


def _outproj_body(per_row, tiles_per_seq, prompt_row0, x_ref, mod_ref, yw_ref, g_ref, bonus_ref, ya_ref,
                  lnw_ref, lnb_ref, hm_ref, wr_ref, wa_ref, gpost_ref, o_ref):
    D = D_MODEL
    m = _mod_rows(mod_ref, per_row, tiles_per_seq, prompt_row0)
    yr = _rwkv_post(yw_ref[...], g_ref[...], bonus_ref[...], lnw_ref[...], lnb_ref[...], hm_ref[...])
    mix = _dot(yr, wr_ref[...]) + _dot(ya_ref[...], wa_ref[...])
    o_ref[...] = x_ref[...] + m[:, 2 * D:3 * D] * _rms(mix, gpost_ref[...])


def _outproj(x, mod, yw, g, bonus, ya, lw, per_row, tm, tiles_per_seq, prompt_row0):
    N, D = x.shape
    W = RWKV_WIDTH
    row = lambda i: (i, 0)
    fixed = lambda i: (0, 0)
    half = pl.BlockSpec((tm, W), row)
    vec = pl.BlockSpec((1, W), fixed)
    return pl.pallas_call(
        functools.partial(_outproj_body, per_row, tiles_per_seq, prompt_row0),
        out_shape=jax.ShapeDtypeStruct((N, D), F32),
        grid=(N // tm,),
        in_specs=[pl.BlockSpec((tm, D), row), _mod_spec(mod, per_row, tm, 1), half, half, half, half, vec, vec,
                  pl.BlockSpec((W, W), fixed), pl.BlockSpec((W, D), fixed), pl.BlockSpec((W, D), fixed),
                  pl.BlockSpec((1, D), fixed)],
        out_specs=pl.BlockSpec((tm, D), row),
        compiler_params=_params(("arbitrary",)),
        name="out_proj",
    )(x, mod, yw, g, bonus, ya, lw["lnw"], lw["lnb"], lw["headmean"], lw["w_out_r"], lw["w_out_a"], lw["g_post0"])


def _ffn_body(moe, per_row, tiles_per_seq, prompt_row0, x_ref, mod_ref, gpre_ref, gpost_ref, rw_ref, rb_ref,
              wg_ref, wu_ref, wd_ref, o_ref, h_ref, acc_ref, comb_ref):
    D = D_MODEL
    e = pl.program_id(1)
    f = pl.program_id(2)
    first = (e == 0) & (f == 0)
    last = (e == pl.num_programs(1) - 1) & (f == pl.num_programs(2) - 1)

    @pl.when(first)
    def _():
        m = _mod_rows(mod_ref, per_row, tiles_per_seq, prompt_row0)
        h = _rms(x_ref[...], gpre_ref[...]) * (1.0 + m[:, D:2 * D]) + m[:, 0:D]
        h_ref[...] = h.astype(BF16)
        acc_ref[...] = jnp.zeros_like(acc_ref)
        if moe:
            logits = _dot3(h, rw_ref[...]) + rb_ref[...]
            lane = lax.broadcasted_iota(jnp.int32, logits.shape, 1)
            m1 = jnp.max(logits, axis=-1, keepdims=True)
            i1 = jnp.min(jnp.where(logits == m1, lane, LANES), axis=-1, keepdims=True)
            rest = jnp.where(lane == i1, -jnp.inf, logits)
            m2 = jnp.max(rest, axis=-1, keepdims=True)
            i2 = jnp.min(jnp.where(rest == m2, lane, LANES), axis=-1, keepdims=True)
            e2 = jnp.exp(m2 - m1)
            comb_ref[...] = jnp.where(lane == i1, 1.0 / (1.0 + e2), 0.0) + jnp.where(lane == i2, e2 / (1.0 + e2), 0.0)

    h = h_ref[...]
    gate = _dot(h, wg_ref[...])
    up = _dot(h, wu_ref[...])
    act = gate * _sigmoid(gate) * up
    if moe:
        comb = comb_ref[...]
        lane = lax.broadcasted_iota(jnp.int32, comb.shape, 1)
        act = act * jnp.sum(jnp.where(lane == e, comb, 0.0), axis=-1, keepdims=True)
    acc_ref[...] += _dot(act, wd_ref[...])

    @pl.when(last)
    def _():
        m = _mod_rows(mod_ref, per_row, tiles_per_seq, prompt_row0)
        o_ref[...] = x_ref[...] + m[:, 2 * D:3 * D] * _rms(acc_ref[...], gpost_ref[...])


def _ffn(x, mod, gpre, gpost, router_w, router_b, wg, wu, wd, w0, E, moe, per_row, tm, tiles_per_seq, prompt_row0,
         tf):
    N, D = x.shape
    F = wg.shape[-1]
    row = lambda i, e, f: (i, 0)
    fixed = lambda i, e, f: (0, 0)
    return pl.pallas_call(
        functools.partial(_ffn_body, moe, per_row, tiles_per_seq, prompt_row0),
        out_shape=jax.ShapeDtypeStruct((N, D), F32),
        grid=(N // tm, E, F // tf),
        in_specs=[pl.BlockSpec((tm, D), row), _mod_spec(mod, per_row, tm, 3),
                  pl.BlockSpec((1, D), fixed), pl.BlockSpec((1, D), fixed),
                  pl.BlockSpec((D, LANES), fixed), pl.BlockSpec((1, LANES), fixed),
                  pl.BlockSpec((None, D, tf), lambda i, e, f: (w0 + e, 0, f)),
                  pl.BlockSpec((None, D, tf), lambda i, e, f: (w0 + e, 0, f)),
                  pl.BlockSpec((None, tf, D), lambda i, e, f: (w0 + e, f, 0))],
        out_specs=pl.BlockSpec((tm, D), row),
        scratch_shapes=[pltpu.VMEM((tm, D), BF16), pltpu.VMEM((tm, D), F32), pltpu.VMEM((tm, LANES), F32)],
        compiler_params=_params(("arbitrary", "arbitrary", "arbitrary")),
        name="moe_ffn" if moe else "dense_ffn",
    )(x, mod, gpre, gpost, router_w, router_b, wg, wu, wd)


ROW_TILE = D_MODEL // LANES
EXPERT_TILE = 512
INFO_E1, INFO_E2, INFO_P1, INFO_P2, INFO_POS1, INFO_POS2 = range(6)


def _top2(logits):
    lane = lax.broadcasted_iota(jnp.int32, logits.shape, 1)
    m1 = jnp.max(logits, axis=-1, keepdims=True)
    i1 = jnp.min(jnp.where(logits == m1, lane, LANES), axis=-1, keepdims=True)
    rest = jnp.where(lane == i1, -jnp.inf, logits)
    m2 = jnp.max(rest, axis=-1, keepdims=True)
    i2 = jnp.min(jnp.where(rest == m2, lane, LANES), axis=-1, keepdims=True)
    e2 = jnp.exp(m2 - m1)
    return lane, i1, i2, 1.0 / (1.0 + e2), e2 / (1.0 + e2)


def _to_row_tiles(ref, x):
    rows = x.shape[0]
    for c in range(ROW_TILE):
        ref[pl.ds(c, rows, stride=ROW_TILE), :] = x[:, c * LANES:(c + 1) * LANES]


def _from_row_tiles(ref, row0, rows):
    return jnp.concatenate([ref[pl.ds(row0 * ROW_TILE + c, rows, stride=ROW_TILE), :] for c in range(ROW_TILE)],
                           axis=1)


def _route_body(tiles_per_seq, prompt_row0, x_ref, mod_ref, gpre_ref, rw_ref, rb_ref, h_ref, info_ref, cnt_ref,
                base_ref):
    D = D_MODEL

    @pl.when(pl.program_id(0) == 0)
    def _():
        base_ref[...] = jnp.zeros_like(base_ref)

    m = _mod_rows(mod_ref, False, tiles_per_seq, prompt_row0)
    h = _rms(x_ref[...], gpre_ref[...]) * (1.0 + m[:, D:2 * D]) + m[:, 0:D]
    _to_row_tiles(h_ref, h)
    lane, i1, i2, p1, p2 = _top2(_dot3(h, rw_ref[...]) + rb_ref[...])
    onehot = jnp.where((lane == i1) | (lane == i2), 1.0, 0.0)
    T = h.shape[0]
    before = lax.broadcasted_iota(jnp.int32, (T, T), 1) < lax.broadcasted_iota(jnp.int32, (T, T), 0)
    rank = _dot(jnp.where(before, 1.0, 0.0), onehot) + base_ref[0:1, :]
    pos1 = jnp.sum(jnp.where(lane == i1, rank, 0.0), axis=-1, keepdims=True)
    pos2 = jnp.sum(jnp.where(lane == i2, rank, 0.0), axis=-1, keepdims=True)
    info = jnp.zeros(onehot.shape, F32)
    for col, val in ((INFO_E1, i1.astype(F32)), (INFO_E2, i2.astype(F32)), (INFO_P1, p1), (INFO_P2, p2),
                     (INFO_POS1, pos1), (INFO_POS2, pos2)):
        info = jnp.where(lane == col, val, info)
    info_ref[...] = info
    total = base_ref[...] + jnp.sum(onehot, axis=0, keepdims=True)
    base_ref[...] = total
    cnt_ref[...] = total


def _moe_route(x, mod, gpre, rw, rb, tm, tiles_per_seq, prompt_row0):
    N, D = x.shape
    row = lambda i: (i, 0)
    fixed = lambda i: (0, 0)
    return pl.pallas_call(
        functools.partial(_route_body, tiles_per_seq, prompt_row0),
        out_shape=(jax.ShapeDtypeStruct((N * ROW_TILE, LANES), F32), jax.ShapeDtypeStruct((N, LANES), F32),
                   jax.ShapeDtypeStruct((SUBLANES, LANES), F32)),
        grid=(N // tm,),
        in_specs=[pl.BlockSpec((tm, D), row), _mod_spec(mod, False, tm, 1), pl.BlockSpec((1, D), fixed),
                  pl.BlockSpec((D, LANES), fixed), pl.BlockSpec((1, LANES), fixed)],
        out_specs=(pl.BlockSpec((tm * ROW_TILE, LANES), row), pl.BlockSpec((tm, LANES), row),
                   pl.BlockSpec((SUBLANES, LANES), fixed)),
        scratch_shapes=[pltpu.VMEM((SUBLANES, LANES), F32)],
        compiler_params=_params(("arbitrary",)),
        name="moe_route",
    )(x, mod, gpre, rw, rb)


def _invert_body(n_tokens, dest_ref, src_ref):
    def clear(i, carry):
        src_ref[i] = 0
        return carry
    lax.fori_loop(0, src_ref.shape[0], clear, 0, unroll=16)

    for choice in range(2):
        def put(t, carry):
            src_ref[dest_ref[choice * n_tokens + t]] = t
            return carry
        lax.fori_loop(0, n_tokens, put, 0, unroll=16)


def _moe_invert(dest12, n_rows):
    smem = pl.BlockSpec(memory_space=pltpu.SMEM)
    return pl.pallas_call(
        functools.partial(_invert_body, dest12.shape[0] // 2),
        out_shape=jax.ShapeDtypeStruct((n_rows,), jnp.int32),
        in_specs=[smem],
        out_specs=smem,
        name="moe_invert",
    )(dest12)


def _row_copy(src_hbm, dst_vmem, sem, src_row, dst_row):
    return pltpu.make_async_copy(src_hbm.at[pl.ds(pl.multiple_of(src_row * ROW_TILE, ROW_TILE), ROW_TILE)],
                                 dst_vmem.at[pl.ds(pl.multiple_of(dst_row * ROW_TILE, ROW_TILE), ROW_TILE)], sem)


def _start_rows(idx_ref, idx0, src_hbm, dst_vmem, sem, rows):
    def body(r, carry):
        _row_copy(src_hbm, dst_vmem, sem, idx_ref[idx0 + r], r).start()
        return carry
    lax.fori_loop(0, rows, body, 0, unroll=16)


def _wait_rows(src_hbm, dst_vmem, sem, rows):
    pltpu.make_async_copy(src_hbm.at[pl.ds(0, rows * ROW_TILE)], dst_vmem.at[pl.ds(0, rows * ROW_TILE)], sem).wait()


def _experts_body(te_ref, nv_ref, src_ref, h_hbm, wg_ref, wu_ref, wd_ref, o_ref, xbuf, hb_ref, acc_ref, sem):
    TM = EXPERT_TILE
    i = pl.program_id(0)
    f = pl.program_id(1)
    slot = i % 2
    n_valid = nv_ref[0]
    valid = i < n_valid

    @pl.when(f == 0)
    def _():
        @pl.when((i == 0) & valid)
        def _():
            _start_rows(src_ref, 0, h_hbm, xbuf.at[0], sem.at[0], TM)

        @pl.when(i + 1 < n_valid)
        def _():
            _start_rows(src_ref, (i + 1) * TM, h_hbm, xbuf.at[1 - slot], sem.at[1 - slot], TM)

        @pl.when(valid)
        def _():
            _wait_rows(h_hbm, xbuf.at[slot], sem.at[slot], TM)
            hb_ref[...] = _from_row_tiles(xbuf.at[slot], 0, TM).astype(BF16)
            acc_ref[...] = jnp.zeros_like(acc_ref)

    @pl.when(valid)
    def _():
        h = hb_ref[...]
        gate = _dot(h, wg_ref[...])
        up = _dot(h, wu_ref[...])
        acc_ref[...] += _dot(gate * _sigmoid(gate) * up, wd_ref[...])

    @pl.when(f == pl.num_programs(1) - 1)
    def _():
        @pl.when(valid)
        def _():
            _to_row_tiles(o_ref, acc_ref[...])

        @pl.when(jnp.logical_not(valid))
        def _():
            o_ref[...] = jnp.zeros_like(o_ref)


def _moe_experts(h_rows, tile_expert, n_valid, src_tok, wg, wu, wd, tf):
    TM = EXPERT_TILE
    P = src_tok.shape[0]
    _, D, F = wg.shape
    grid_spec = pltpu.PrefetchScalarGridSpec(
        num_scalar_prefetch=3,
        grid=(P // TM, F // tf),
        in_specs=[pl.BlockSpec(memory_space=pl.ANY),
                  pl.BlockSpec((None, D, tf), lambda i, f, te, nv, src: (te[i], 0, f)),
                  pl.BlockSpec((None, D, tf), lambda i, f, te, nv, src: (te[i], 0, f)),
                  pl.BlockSpec((None, tf, D), lambda i, f, te, nv, src: (te[i], f, 0))],
        out_specs=pl.BlockSpec((TM * ROW_TILE, LANES), lambda i, f, te, nv, src: (i, 0)),
        scratch_shapes=[pltpu.VMEM((2, TM * ROW_TILE, LANES), F32), pltpu.VMEM((TM, D), BF16),
                        pltpu.VMEM((TM, D), F32), pltpu.SemaphoreType.DMA((2,))],
    )
    return pl.pallas_call(
        _experts_body,
        out_shape=jax.ShapeDtypeStruct((P * ROW_TILE, LANES), F32),
        grid_spec=grid_spec,
        compiler_params=_params(("arbitrary", "arbitrary")),
        name="moe_experts",
    )(tile_expert, n_valid, src_tok, h_rows, wg, wu, wd)


def _combine_body(tiles_per_seq, prompt_row0, dest_ref, f_hbm, x_ref, mod_ref, info_ref, gpost_ref, o_ref, gbuf, sem):
    D = D_MODEL
    i = pl.program_id(0)
    T = x_ref.shape[0]
    slot = i % 2

    @pl.when(i == 0)
    def _():
        _start_rows(dest_ref, 0, f_hbm, gbuf.at[0], sem.at[0], 2 * T)

    @pl.when(i + 1 < pl.num_programs(0))
    def _():
        _start_rows(dest_ref, (i + 1) * 2 * T, f_hbm, gbuf.at[1 - slot], sem.at[1 - slot], 2 * T)

    _wait_rows(f_hbm, gbuf.at[slot], sem.at[slot], 2 * T)
    info = info_ref[...]
    f1 = _from_row_tiles(gbuf.at[slot], 0, T)
    f2 = _from_row_tiles(gbuf.at[slot], T, T)
    y = info[:, INFO_P1:INFO_P1 + 1] * f1 + info[:, INFO_P2:INFO_P2 + 1] * f2
    m = _mod_rows(mod_ref, False, tiles_per_seq, prompt_row0)
    o_ref[...] = x_ref[...] + m[:, 2 * D:3 * D] * _rms(y, gpost_ref[...])


def _moe_combine(dest, f_rows, x, mod, info, gpost, tm, tiles_per_seq, prompt_row0):
    N, D = x.shape
    row = lambda i, d: (i, 0)
    fixed = lambda i, d: (0, 0)
    grid_spec = pltpu.PrefetchScalarGridSpec(
        num_scalar_prefetch=1,
        grid=(N // tm,),
        in_specs=[pl.BlockSpec(memory_space=pl.ANY), pl.BlockSpec((tm, D), row),
                  pl.BlockSpec(mod.shape, fixed), pl.BlockSpec((tm, LANES), row), pl.BlockSpec((1, D), fixed)],
        out_specs=pl.BlockSpec((tm, D), row),
        scratch_shapes=[pltpu.VMEM((2, 2 * tm * ROW_TILE, LANES), F32), pltpu.SemaphoreType.DMA((2,))],
    )
    return pl.pallas_call(
        functools.partial(_combine_body, tiles_per_seq, prompt_row0),
        out_shape=jax.ShapeDtypeStruct((N, D), F32),
        grid_spec=grid_spec,
        compiler_params=_params(("arbitrary",)),
        name="moe_combine",
    )(dest, f_rows, x, mod, info, gpost)


def _moe_prompt(x, mod, gpre, gpost, fw, tm, tiles_per_seq, prompt_row0, tf):
    N = x.shape[0]
    TM = EXPERT_TILE
    n_tiles = (2 * N) // TM + N_EXPERTS
    h_rows, info, cnt = _moe_route(x, mod, gpre, fw["rw"], fw["rb"], tm, tiles_per_seq, prompt_row0)
    e1 = info[:, INFO_E1].astype(jnp.int32)
    e2 = info[:, INFO_E2].astype(jnp.int32)
    counts = cnt[0, :N_EXPERTS].astype(jnp.int32)
    padded = ((counts + TM - 1) // TM) * TM
    ends = jnp.cumsum(padded)
    starts = ends - padded
    dest1 = starts[e1] + info[:, INFO_POS1].astype(jnp.int32)
    dest2 = starts[e2] + info[:, INFO_POS2].astype(jnp.int32)
    src_tok = _moe_invert(jnp.concatenate([dest1, dest2]), n_tiles * TM)
    tile_start = jnp.arange(n_tiles, dtype=jnp.int32) * TM
    tile_expert = jnp.minimum(jnp.sum(ends[None, :] <= tile_start[:, None], axis=1), N_EXPERTS - 1).astype(jnp.int32)
    n_valid = (ends[-1:] // TM).astype(jnp.int32)
    f_rows = _moe_experts(h_rows, tile_expert + fw["w0"], n_valid, src_tok, fw["wg"], fw["wu"], fw["wd"], tf)
    dest = jnp.concatenate([dest1.reshape(-1, 1, tm), dest2.reshape(-1, 1, tm)], axis=1).reshape(-1)
    return _moe_combine(dest, f_rows, x, mod, info, gpost, tm, tiles_per_seq, prompt_row0)


def _layer_weights(p, l):
    W = RWKV_WIDTH
    heads = np.arange(W) // HEAD_DIM
    headsum = jnp.asarray((heads[:, None] == heads[None, :]).astype(np.float32))
    perm = _q_perm()
    w_in = p["w_in"][l]
    w_in = jnp.concatenate([w_in[:, :RWKV_COLS], w_in[:, RWKV_COLS + perm], w_in[:, RWKV_COLS + SWA_WIDTH:]], axis=1)
    w_out = p["w_out"][l]
    zeros = jnp.zeros((DECAY_LORA, W), F32)
    row = lambda t: t.reshape(1, -1)
    return {
        "w_in": w_in.astype(BF16),
        "w_out_r": w_out[:W].astype(BF16),
        "w_out_a": w_out[W + perm].astype(BF16),
        "mu": row(p["mu_shift"][l]),
        "wd": jnp.concatenate([p["w_decay_up"][l], zeros], axis=0),
        "wi": jnp.concatenate([zeros, p["w_iclr_up"][l]], axis=0),
        "wg": p["w_gate_up"][l],
        "dbase": row(p["decay_base"][l]), "ibase": row(p["iclr_base"][l]),
        "kk": row(p["k_k"][l]), "ka": row(p["k_a"][l]), "rk": row(p["r_k"][l]),
        "lnw": row(p["lnx_w"][l]), "lnb": row(p["lnx_b"][l]),
        "headsum": headsum.astype(BF16), "headmean": (headsum / HEAD_DIM).astype(BF16),
        "g_pre0": row(p["norm_pre"][l, 0]), "g_pre1": row(p["norm_pre"][l, 1]),
        "g_post0": row(p["norm_post"][l, 0]), "g_post1": row(p["norm_post"][l, 1]),
    }


def _stacked(w, dtype):
    return w.astype(dtype).reshape((-1,) + w.shape[-2:])


def _ffn_weights(p, stacks, l):
    i = l // 2
    if l % 2 == 0:
        wg, wu, wd = stacks["dense"]
        return dict(moe=False, rw=jnp.zeros((D_MODEL, LANES), F32), rb=jnp.zeros((1, LANES), F32),
                    wg=wg, wu=wu, wd=wd, w0=i, n=1)
    rw = jnp.zeros((D_MODEL, LANES), F32).at[:, :N_EXPERTS].set(p["router_w"][i])
    rb = jnp.full((1, LANES), NEG_INF, F32).at[0, :N_EXPERTS].set(p["router_b"][i])
    wg, wu, wd = stacks["moe"]
    return dict(moe=True, rw=rw, rb=rb, wg=wg, wu=wu, wd=wd, w0=i * N_EXPERTS, n=N_EXPERTS)


def _pick_tile(n, pref):
    t = min(pref, n)
    while n % t:
        t //= 2
    return t


def _ffn_tile(f):
    for t in (1408, 896, 512, 256, 128):
        if f % t == 0:
            return t
    return f


def _trunk(x3, mods, lws, fws, p, prompt, prompt_row0, state=None):
    B, T, D = x3.shape
    N = B * T
    x = x3.reshape(N, D)
    per_row = not prompt
    tm = _pick_tile(T if prompt else N, 512)
    tps = (T // tm) if prompt else 1
    depth = len(lws)
    wkv_out, shift_out, k_out, v_out = [], [], [], []
    for l in range(depth):
        lw, fw = lws[l], fws[l]
        mod0, mod1 = mods[2 * l], mods[2 * l + 1]
        pr, q, k, v = _inproj(x, mod0, lw["g_pre0"], lw["w_in"], per_row, tm, tps, prompt_row0)
        if prompt:
            r, ld, kh, vv, a, b, g, bonus = _prep_prompt(pr, lw, tm, tps)
            yw, hbd = _wkv_prompt(r, ld, kh, vv, a, b, B, T)
            n_pairs = RWKV_WIDTH // PAIR
            hb = hbd.reshape(B, n_pairs, 2, HEAD_DIM, 2, HEAD_DIM)
            s_kv = jnp.stack([hb[:, :, 0, :, 0, :], hb[:, :, 1, :, 1, :]], axis=2)
            s_new = jnp.swapaxes(s_kv.reshape(B, RWKV_HEADS, HEAD_DIM, HEAD_DIM), -1, -2)
            ya = _swa_prompt(q, k, v, p["rel_bias"], p["attn_sinks"][l], B, T)
            window = lambda t: t.reshape(B, T, KV_WIDTH)[:, -WINDOW:].reshape(B, WINDOW, SWA_KV_HEADS, HEAD_DIM)
            kb, vb = window(k), window(v)
            last = pr.reshape(B, T, RWKV_COLS)[:, -1]
        else:
            r, ld, kh, vv, a, b, g, bonus = _prep_sample(pr, state["shift"][l], lw)
            yw, s_new = _wkv_sample(state["wkv"][l], r, ld, kh, vv, a, b)
            ya, kb, vb = _swa_sample(q, k, v, state["k"][l], state["v"][l], p["rel_bias"], p["attn_sinks"][l])
            last = pr
        x = _outproj(x, mod0, yw, g, bonus, ya, lw, per_row, tm, tps, prompt_row0)
        tf = _ffn_tile(fw["wg"].shape[-1])
        if prompt and fw["moe"]:
            x = _moe_prompt(x, mod1, lw["g_pre1"], lw["g_post1"], fw, tm, tps, prompt_row0, 2 * tf)
        else:
            x = _ffn(x, mod1, lw["g_pre1"], lw["g_post1"], fw["rw"], fw["rb"], fw["wg"], fw["wu"], fw["wd"],
                     fw["w0"], fw["n"], fw["moe"], per_row, tm, tps, prompt_row0, tf)
        wkv_out.append(s_new)
        shift_out.append(last)
        k_out.append(kb)
        v_out.append(vb)
    return x.reshape(B, T, D), jnp.stack(wkv_out), jnp.stack(shift_out), jnp.stack(k_out), jnp.stack(v_out)


def _forward(x_prompt, x_sample, c_prompt, c_sample, state_wkv, state_shift, cache_swa_k, cache_swa_v, p):
    depth = p["w_in"].shape[0]
    Bp, Bs = c_prompt.shape[0], c_sample.shape[0]
    D = D_MODEL
    pad = (-(Bs + Bp)) % SUBLANES
    c_all = jnp.concatenate([c_sample, c_prompt, jnp.zeros((pad, D), F32)], axis=0)
    mods = _ada_all(c_all, p["ada_w"].reshape(2 * depth, D, 3 * D), p["ada_b"].reshape(2 * depth, 1, 3 * D))
    lws = [_layer_weights(p, l) for l in range(depth)]
    stacks = {"dense": tuple(_stacked(p[k], BF16) for k in ("ffn_w_gate", "ffn_w_up", "ffn_w_down")),
              "moe": tuple(_stacked(p[k], BF16) for k in ("moe_w_gate", "moe_w_up", "moe_w_down"))}
    fws = [_ffn_weights(p, stacks, l) for l in range(depth)]
    y_p, wkv_p, shift_p, k_p, v_p = _trunk(x_prompt, mods, lws, fws, p, True, Bs)
    state = {"wkv": state_wkv, "shift": state_shift, "k": cache_swa_k, "v": cache_swa_v}
    y_s, wkv_s, shift_s, k_s, v_s = _trunk(x_sample, mods, lws, fws, p, False, Bs, state)
    return (y_p, y_s, wkv_p, shift_p, k_p, v_p, wkv_s, shift_s, k_s, v_s)


def kernel(x_prompt, x_sample, c_prompt, c_sample, state_wkv, state_shift, cache_swa_k, cache_swa_v, rel_bias, ada_w, ada_b, norm_pre, norm_post, w_in, mu_shift, w_decay_up, decay_base, w_iclr_up, iclr_base, w_gate_up, k_k, k_a, r_k, lnx_w, lnx_b, attn_sinks, w_out, ffn_w_gate, ffn_w_up, ffn_w_down, router_w, router_b, moe_w_gate, moe_w_up, moe_w_down):
    p = {"rel_bias": rel_bias, "ada_w": ada_w, "ada_b": ada_b, "norm_pre": norm_pre, "norm_post": norm_post,
         "w_in": w_in, "mu_shift": mu_shift, "w_decay_up": w_decay_up, "decay_base": decay_base,
         "w_iclr_up": w_iclr_up, "iclr_base": iclr_base, "w_gate_up": w_gate_up, "k_k": k_k, "k_a": k_a,
         "r_k": r_k.reshape(r_k.shape[0], -1), "lnx_w": lnx_w, "lnx_b": lnx_b, "attn_sinks": attn_sinks,
         "w_out": w_out, "ffn_w_gate": ffn_w_gate, "ffn_w_up": ffn_w_up, "ffn_w_down": ffn_w_down,
         "router_w": router_w, "router_b": router_b, "moe_w_gate": moe_w_gate, "moe_w_up": moe_w_up,
         "moe_w_down": moe_w_down}
    return _forward(x_prompt, x_sample, c_prompt, c_sample, state_wkv, state_shift, cache_swa_k, cache_swa_v, p)
```

```python
import functools

import numpy as np
import jax
import jax.numpy as jnp
from jax import lax
from jax.experimental import pallas as pl
from jax.experimental.pallas import tpu as pltpu

F32 = jnp.float32
BF16 = jnp.bfloat16

D_MODEL = 1024
HEAD_DIM = 64
RWKV_WIDTH = 512
RWKV_HEADS = RWKV_WIDTH // HEAD_DIM
SWA_WIDTH = 512
SWA_HEADS = SWA_WIDTH // HEAD_DIM
SWA_KV_HEADS = 2
SWA_GROUP = SWA_HEADS // SWA_KV_HEADS
KV_WIDTH = SWA_KV_HEADS * HEAD_DIM
WINDOW = 128
DECAY_LORA = 64
ICLR_LORA = 64
GATE_LORA = 128
RWKV_COLS = 3 * RWKV_WIDTH + DECAY_LORA + ICLR_LORA + GATE_LORA
IN_COLS = RWKV_COLS + SWA_WIDTH + 2 * KV_WIDTH
LORA_OFF = 3 * RWKV_WIDTH
GATE_OFF = LORA_OFF + DECAY_LORA + ICLR_LORA
LN_X_EPS = 64e-5
RMS_EPS = 1e-6
N_BUCKETS = 32
MAX_DISTANCE = 128
N_EXPERTS = 8
NEG_INF = -1e30

LANES = 128
SUBLANES = 8
VMEM_LIMIT = 56 * 1024 * 1024

WKV_CHUNK = 64
PAIR = 2 * HEAD_DIM

NN = (((1,), (0,)), ((), ()))
NT = (((1,), (1,)), ((), ()))
TN = (((0,), (0,)), ((), ()))


def _dot(a, b, dims=NN):
    return lax.dot_general(a.astype(BF16), b.astype(BF16), dims, preferred_element_type=F32)


def _split(x, pieces):
    out = []
    for _ in range(pieces - 1):
        hi = x.astype(BF16)
        out.append(hi)
        x = x - hi.astype(F32)
    out.append(x.astype(BF16))
    return out


def _select_dot(x, sel, pieces=3, sel_left=False):
    d = lambda t: lax.dot_general(*((sel, t) if sel_left else (t, sel)), NN, preferred_element_type=F32)
    return sum(d(t) for t in _split(x, pieces))


def _dot3(a, b):
    a_hi, a_lo = _split(a, 2)
    b_hi, b_lo = _split(b, 2)
    d = lambda x, y: lax.dot_general(x, y, NN, preferred_element_type=F32)
    return d(a_hi, b_hi) + d(a_lo, b_hi) + d(a_hi, b_lo)


def _sigmoid(x):
    return 1.0 / (1.0 + jnp.exp(-x))


def _params(sem):
    return pltpu.CompilerParams(dimension_semantics=sem, vmem_limit_bytes=VMEM_LIMIT)


def _rms(x, g):
    return x * lax.rsqrt(jnp.mean(x * x, axis=-1, keepdims=True) + RMS_EPS) * g


def _mod_rows(mod_ref, per_row, tiles_per_seq, prompt_row0):
    if per_row:
        return mod_ref[...]
    b = pl.program_id(0) // tiles_per_seq
    return mod_ref[pl.ds(prompt_row0 + b, 1), :]


def _ada_body(c_ref, w_ref, b_ref, o_ref):
    c = c_ref[...]
    o_ref[...] = _dot(c * _sigmoid(c), w_ref[...]) + b_ref[...]


def _ada_all(c_all, ada_w, ada_b):
    R, D = c_all.shape
    n = ada_w.shape[0]
    tn = 1024
    return pl.pallas_call(
        _ada_body,
        out_shape=jax.ShapeDtypeStruct((n, R, 3 * D), F32),
        grid=(n, 3 * D // tn),
        in_specs=[pl.BlockSpec((R, D), lambda i, j: (0, 0)),
                  pl.BlockSpec((None, D, tn), lambda i, j: (i, 0, j)),
                  pl.BlockSpec((None, 1, tn), lambda i, j: (i, 0, j))],
        out_specs=pl.BlockSpec((None, R, tn), lambda i, j: (i, 0, j)),
        compiler_params=_params(("arbitrary", "arbitrary")),
        name="ada_mod",
    )(c_all, ada_w, ada_b)


def _inproj_body(per_row, tiles_per_seq, prompt_row0, x_ref, mod_ref, g_ref, w_ref, pr_ref, q_ref, k_ref, v_ref):
    D = D_MODEL
    m = _mod_rows(mod_ref, per_row, tiles_per_seq, prompt_row0)
    h = _rms(x_ref[...], g_ref[...]) * (1.0 + m[:, D:2 * D]) + m[:, 0:D]
    proj = _dot(h, w_ref[...])
    pr_ref[...] = proj[:, 0:RWKV_COLS]
    q_ref[...] = proj[:, RWKV_COLS:RWKV_COLS + SWA_WIDTH]
    k_ref[...] = proj[:, RWKV_COLS + SWA_WIDTH:RWKV_COLS + SWA_WIDTH + KV_WIDTH]
    v_ref[...] = proj[:, RWKV_COLS + SWA_WIDTH + KV_WIDTH:IN_COLS]


def _mod_spec(mod, per_row, tm, nargs):
    R = mod.shape[0]
    if per_row:
        return pl.BlockSpec((tm, 3 * D_MODEL), lambda i, *_: (0, 0))
    return pl.BlockSpec((R, 3 * D_MODEL), lambda i, *_: (0, 0))


def _inproj(x, mod, g, w, per_row, tm, tiles_per_seq, prompt_row0):
    N, D = x.shape
    row = lambda i: (i, 0)
    fixed = lambda i: (0, 0)
    return pl.pallas_call(
        functools.partial(_inproj_body, per_row, tiles_per_seq, prompt_row0),
        out_shape=(jax.ShapeDtypeStruct((N, RWKV_COLS), F32), jax.ShapeDtypeStruct((N, SWA_WIDTH), F32),
                   jax.ShapeDtypeStruct((N, KV_WIDTH), F32), jax.ShapeDtypeStruct((N, KV_WIDTH), F32)),
        grid=(N // tm,),
        in_specs=[pl.BlockSpec((tm, D), row), _mod_spec(mod, per_row, tm, 1),
                  pl.BlockSpec((1, D), fixed), pl.BlockSpec((D, IN_COLS), fixed)],
        out_specs=(pl.BlockSpec((tm, RWKV_COLS), row), pl.BlockSpec((tm, SWA_WIDTH), row),
                   pl.BlockSpec((tm, KV_WIDTH), row), pl.BlockSpec((tm, KV_WIDTH), row)),
        compiler_params=_params(("arbitrary",)),
        name="in_proj",
    )(x, mod, g, w)


PREP_KEYS = ("mu", "wd", "wi", "wg", "dbase", "ibase", "kk", "ka", "rk", "headsum")
POST_KEYS = ("lnw", "lnb", "headmean")


def _prep_specs(fixed):
    W = RWKV_WIDTH
    vec = pl.BlockSpec((1, W), fixed)
    return [pl.BlockSpec((1, RWKV_COLS), fixed), pl.BlockSpec((LANES, W), fixed), pl.BlockSpec((LANES, W), fixed),
            pl.BlockSpec((GATE_LORA, W), fixed), vec, vec, vec, vec, vec, pl.BlockSpec((W, W), fixed)]


def _rwkv_features(pr, shifted, mu_ref, wd_ref, wi_ref, wg_ref, dbase_ref, ibase_ref, kk_ref, ka_ref, rk_ref, hs_ref):
    W = RWKV_WIDTH
    xs = pr + (shifted - pr) * mu_ref[...]
    r = xs[:, 0:W]
    k = xs[:, W:2 * W]
    v = xs[:, 2 * W:3 * W]
    lora = xs[:, LORA_OFF:GATE_OFF]
    gl = xs[:, GATE_OFF:RWKV_COLS]
    z = dbase_ref[...] + _dot3(jnp.tanh(lora), wd_ref[...])
    ld = -float(np.exp(-0.5)) * _sigmoid(z)
    iclr = _sigmoid(ibase_ref[...] + _dot3(lora, wi_ref[...]))
    g = _dot3(_sigmoid(gl), wg_ref[...])
    hs = hs_ref[...]
    kk = k * kk_ref[...]
    kk = kk / jnp.maximum(jnp.sqrt(_select_dot(kk * kk, hs, 2)), 1e-12)
    kh = k * (1.0 + (iclr - 1.0) * ka_ref[...])
    bonus = _select_dot(r * kh * rk_ref[...], hs, 2) * v
    return r, ld, kh, v, -kk, kk * iclr, g, bonus


def _rwkv_post(y, g, bonus, lnw, lnb, hm):
    mean = _select_dot(y, hm, 2)
    dv = y - mean
    var = _select_dot(dv * dv, hm, 2)
    return (dv * lax.rsqrt(var + LN_X_EPS) * lnw + lnb + bonus) * g


def _prep_body(pr_ref, prev_ref, *refs):
    outs = refs[len(PREP_KEYS):]
    for ref, val in zip(outs, _rwkv_features(pr_ref[...], prev_ref[...], *refs[:len(PREP_KEYS)])):
        ref[...] = val


def _prep_prompt_body(tiles_per_seq, pr_ref, prev_ref, *refs):
    pr = pr_ref[...]
    first = (pl.program_id(0) % tiles_per_seq) == 0
    carry = jnp.where(first, 0.0, prev_ref[SUBLANES - 1:SUBLANES, :])
    shifted = jnp.where(lax.broadcasted_iota(jnp.int32, pr.shape, 0) == 0, carry, pltpu.roll(pr, 1, 0))
    for ref, val in zip(refs[len(PREP_KEYS):], _rwkv_features(pr, shifted, *refs[:len(PREP_KEYS)])):
        ref[...] = val


def _prep_prompt(pr, lw, tm, tiles_per_seq):
    N = pr.shape[0]
    W = RWKV_WIDTH
    row = lambda i: (i, 0)
    fixed = lambda i: (0, 0)
    per = tm // SUBLANES
    out = jax.ShapeDtypeStruct((N, W), F32)
    return pl.pallas_call(
        functools.partial(_prep_prompt_body, tiles_per_seq),
        out_shape=(out,) * 8,
        grid=(N // tm,),
        in_specs=[pl.BlockSpec((tm, RWKV_COLS), row),
                  pl.BlockSpec((SUBLANES, RWKV_COLS), lambda i: (jnp.maximum(i * per - 1, 0), 0))] + _prep_specs(fixed),
        out_specs=(pl.BlockSpec((tm, W), row),) * 8,
        compiler_params=_params(("arbitrary",)),
        name="rwkv_prep",
    )(pr, pr, *[lw[k] for k in PREP_KEYS])


def _prep_sample(pr, prev, lw):
    N = pr.shape[0]
    W = RWKV_WIDTH
    row = lambda i: (i, 0)
    fixed = lambda i: (0, 0)
    out = jax.ShapeDtypeStruct((N, W), F32)
    return pl.pallas_call(
        _prep_body,
        out_shape=(out,) * 8,
        grid=(1,),
        in_specs=[pl.BlockSpec((N, RWKV_COLS), row), pl.BlockSpec((N, RWKV_COLS), row)] + _prep_specs(fixed),
        out_specs=(pl.BlockSpec((N, W), row),) * 8,
        compiler_params=_params(("arbitrary",)),
        name="rwkv_prep",
    )(pr, prev, *[lw[k] for k in PREP_KEYS])


def _stack_heads(x, lane_head0):
    return jnp.concatenate([jnp.where(lane_head0, x, 0.0), jnp.where(lane_head0, 0.0, x)], axis=0)


def _fold_heads(x):
    c = x.shape[0] // 2
    return x[0:c] + x[c:2 * c]


def _dots(xs, ys, dims=NN):
    return [_dot(x, y, dims) for x, y in zip(xs, ys)]


def _unit_lower_inverse(ns, same16, eye):
    nd = [jnp.where(same16, n, 0.0) for n in ns]
    no = [n - d for n, d in zip(ns, nd)]
    n2 = _dots(nd, nd)
    n4 = _dots(n2, n2)
    n8 = _dots(n4, n4)
    td = [eye + d for d in nd]
    for pw in (n2, n4, n8):
        td = [t + u for t, u in zip(td, _dots(td, pw))]
    q = _dots(td, no)
    q2 = _dots(q, q)
    z = [eye + x for x in q]
    z = [t + u for t, u in zip(z, _dots(z, q2))]
    return _dots(z, td)


def _wkv_chunk_body(chunks, r_all, ld_all, k_all, v_all_, a_all, b_all, y_ref, s_ref, h_ref):
    C = WKV_CHUNK
    n_pairs = RWKV_WIDTH // PAIR

    @pl.when(pl.program_id(1) == 0)
    def _():
        h_ref[...] = jnp.zeros_like(h_ref)

    ri = lax.broadcasted_iota(jnp.int32, (PAIR, PAIR), 0)
    ci = lax.broadcasted_iota(jnp.int32, (PAIR, PAIR), 1)
    same_head = (ri // C) == (ci // C)
    strict_lower = same_head & (ci < ri)
    incl_lower = same_head & (ci <= ri)
    same16 = (ri // 16) == (ci // 16)
    eye_b = ri == ci
    eye = jnp.where(eye_b, 1.0, 0.0)
    tri = jnp.where(lax.broadcasted_iota(jnp.int32, (C, C), 1) <= lax.broadcasted_iota(jnp.int32, (C, C), 0),
                    1.0, 0.0).astype(BF16)
    lane_head0 = lax.broadcasted_iota(jnp.int32, (C, PAIR), 1) < HEAD_DIM
    zeros = jnp.zeros((PAIR, PAIR), F32)

    cat0 = lambda x, y: jnp.concatenate([x, y], axis=0)
    cat1 = lambda x, y: jnp.concatenate([x, y], axis=1)

    xa, xr, v_st, bh_st, kh_st, yb, yk, p_all = [], [], [], [], [], [], [], []
    for c in range(chunks):
        rows = slice(c * C, (c + 1) * C)
        ld = ld_all[rows, :]
        cum = _select_dot(ld, tri, 3, sel_left=True)
        last = cum[C - 1:C, :]
        p_inv = jnp.exp(-cum)
        p_tail = jnp.exp(last - cum)
        p_end = jnp.exp(last)
        a_t = a_all[rows, :] * jnp.exp(cum - ld)
        r_t = r_all[rows, :] * jnp.exp(cum)
        b_raw = b_all[rows, :]
        k_raw = k_all[rows, :]
        b_t = b_raw * p_inv
        k_t = k_raw * p_inv
        b_h = b_raw * p_tail
        k_h = k_raw * p_tail
        v_all = v_all_[rows, :]
        for j in range(n_pairs):
            lanes = slice(j * PAIR, (j + 1) * PAIR)
            xa.append(_stack_heads(a_t[:, lanes], lane_head0))
            xr.append(_stack_heads(r_t[:, lanes], lane_head0))
            v_st.append(_stack_heads(v_all[:, lanes], lane_head0))
            bh_st.append(_stack_heads(b_h[:, lanes], lane_head0))
            kh_st.append(_stack_heads(k_h[:, lanes], lane_head0))
            yb.append(cat0(b_t[:, lanes], b_t[:, lanes]))
            yk.append(cat0(k_t[:, lanes], k_t[:, lanes]))
            p_all.append(p_end[:, lanes])

    gram = _dots([cat0(x, y) for x, y in zip(xa, xr)], [cat0(x, y) for x, y in zip(yb, yk)], NT)
    n_mat = [jnp.where(strict_lower, g[0:PAIR, 0:PAIR], 0.0) for g in gram]
    m_mat = [jnp.where(strict_lower, g[0:PAIR, PAIR:2 * PAIR], 0.0) for g in gram]
    a_rbk = [cat1(jnp.where(incl_lower, g[PAIR:2 * PAIR, 0:PAIR], 0.0),
                  jnp.where(incl_lower, g[PAIR:2 * PAIR, PAIR:2 * PAIR], 0.0)) for g in gram]
    t_inv = _unit_lower_inverse(n_mat, same16, eye)
    mv = _dots(m_mat, v_st)
    tx = _dots(t_inv, [cat1(x, y) for x, y in zip(xa, mv)])
    rhs = [cat0(t, cat1(zeros, v)) for t, v in zip(tx, v_st)]
    ry = _dots(a_rbk, rhs)
    pp = _dots([cat0(x, y) for x, y in zip(bh_st, kh_st)], rhs, TN)

    for c in range(chunks):
        us = [c * n_pairs + j for j in range(n_pairs)]
        h0 = [h_ref[j] for j in range(n_pairs)]
        r_bar = [_fold_heads(xr[u] + ry[u][:, 0:PAIR]) for u in us]
        phi = [pp[u][:, 0:PAIR] + jnp.where(eye_b, p_all[u], 0.0) for u in us]
        ys = _dots(r_bar, h0)
        hs = _dots(phi, h0)
        for j, u in enumerate(us):
            y_ref[c * C:(c + 1) * C, j * PAIR:(j + 1) * PAIR] = ys[j] + _fold_heads(ry[u][:, PAIR:2 * PAIR])
            h_ref[j] = hs[j] + pp[u][:, PAIR:2 * PAIR]

    @pl.when(pl.program_id(1) == pl.num_programs(1) - 1)
    def _():
        s_ref[...] = h_ref[...]


def _wkv_prompt(r, ld, k, v, a, b, batch, seq):
    N, W = r.shape
    chunks = 4
    tt = chunks * WKV_CHUNK
    steps = seq // tt
    n_pairs = W // PAIR
    row = lambda bb, t: (bb * steps + t, 0)
    spec = pl.BlockSpec((tt, W), row)
    return pl.pallas_call(
        functools.partial(_wkv_chunk_body, chunks),
        out_shape=(jax.ShapeDtypeStruct((N, W), F32), jax.ShapeDtypeStruct((batch, n_pairs, PAIR, PAIR), F32)),
        grid=(batch, steps),
        in_specs=[spec] * 6,
        out_specs=(spec, pl.BlockSpec((None, n_pairs, PAIR, PAIR), lambda bb, t: (bb, 0, 0, 0))),
        scratch_shapes=[pltpu.VMEM((n_pairs, PAIR, PAIR), F32)],
        compiler_params=_params(("arbitrary", "arbitrary")),
        name="wkv_chunk_scan",
    )(r, ld, k, v, a, b)


def _wkv_step_body(s_ref, r_ref, ld_ref, k_ref, v_ref, a_ref, b_ref, exp_ref, red_ref, y_ref, so_ref):
    HD = HEAD_DIM
    nrep = HD * HD // LANES
    lane = lax.broadcasted_iota(jnp.int32, r_ref.shape, 1)
    low = lane < HD
    y = jnp.zeros(r_ref.shape, F32)
    for hh in range(2):
        def tiled(ref, fn=None):
            x = ref[...]
            if fn is not None:
                x = fn(x)
            sw = pltpu.roll(x, HD, 1)
            both = jnp.where(low, x, sw) if hh == 0 else jnp.where(low, sw, x)
            return jnp.tile(both, (1, nrep))
        cols = slice(hh * HD * HD, (hh + 1) * HD * HD)
        s = s_ref[:, cols]
        expand = exp_ref[hh]
        reduce_ = red_ref[hh]
        sa = _select_dot(s * tiled(a_ref), reduce_)
        s_new = (s * tiled(ld_ref, jnp.exp) + _select_dot(sa, expand) * tiled(b_ref)
                 + _select_dot(v_ref[...], expand) * tiled(k_ref))
        so_ref[:, cols] = s_new
        y = y + _select_dot(s_new * tiled(r_ref), reduce_)
    y_ref[...] = y


def _wkv_step_consts():
    HD = HEAD_DIM
    expand = np.zeros((2, PAIR, HD * HD), np.float32)
    for hh in range(2):
        for vv in range(HD):
            expand[hh, hh * HD + vv, vv * HD:(vv + 1) * HD] = 1.0
    return jnp.asarray(expand, BF16), jnp.asarray(expand.transpose(0, 2, 1), BF16)


def _wkv_sample(state, r, ld, k, v, a, b):
    B = state.shape[0]
    W = RWKV_WIDTH
    HD2 = HEAD_DIM * HEAD_DIM
    expand, reduce_ = _wkv_step_consts()
    s2 = state.reshape(B, RWKV_HEADS * HD2)
    st_spec = pl.BlockSpec((B, 2 * HD2), lambda j: (0, j))
    vec = pl.BlockSpec((B, PAIR), lambda j: (0, j))
    y, s_new = pl.pallas_call(
        _wkv_step_body,
        out_shape=(jax.ShapeDtypeStruct((B, W), F32), jax.ShapeDtypeStruct((B, RWKV_HEADS * HD2), F32)),
        grid=(W // PAIR,),
        in_specs=[st_spec] + [vec] * 6 + [pl.BlockSpec((2, PAIR, HD2), lambda j: (0, 0, 0)),
                                          pl.BlockSpec((2, HD2, PAIR), lambda j: (0, 0, 0))],
        out_specs=(vec, st_spec),
        compiler_params=_params(("arbitrary",)),
        name="wkv_step",
    )(s2, r, ld, k, v, a, b, expand, reduce_)
    return y, s_new.reshape(state.shape)


def _q_perm():
    return np.array([(h * SWA_GROUP + g) * HEAD_DIM + d for g in range(SWA_GROUP) for h in range(SWA_KV_HEADS)
                     for d in range(HEAD_DIM)], np.int32)


def _rel_bucket_np(dist):
    max_exact = N_BUCKETS // 2
    d = np.maximum(dist, 0)
    ratio = np.log(np.maximum(d, 1).astype(np.float32) / np.float32(max_exact)) / np.float32(
        np.log(MAX_DISTANCE / max_exact))
    large = np.minimum(max_exact + (ratio.astype(np.float32) * np.float32(N_BUCKETS - max_exact)).astype(np.int32),
                       N_BUCKETS - 1)
    return np.where(d < max_exact, d, large).astype(np.int32)


def _bias_from_buckets(idx, rb_ref, head):
    acc = jnp.zeros(idx.shape, F32)
    for bk in range(N_BUCKETS):
        acc = jnp.where(idx == bk, rb_ref[bk, head], acc)
    return acc


def _swa_prompt_body(rb_ref, sink_ref, idx_ref, q_ref, kp_ref, kc_ref, vp_ref, vc_ref, o_ref, bias_ref):
    Q = WINDOW
    first = (pl.program_id(0) == 0) & (pl.program_id(1) == 0)

    @pl.when(first)
    def _():
        idx = idx_ref[...]
        qi = lax.broadcasted_iota(jnp.int32, (Q, 2 * Q), 0)
        kj = lax.broadcasted_iota(jnp.int32, (Q, 2 * Q), 1)
        in_window = ((kj < Q) & (kj >= qi)) | ((kj >= Q) & ((kj - Q) <= qi))
        for g in range(SWA_GROUP):
            for h in range(SWA_KV_HEADS):
                bias = jnp.where(in_window, _bias_from_buckets(idx, rb_ref, h * SWA_GROUP + g), NEG_INF)
                bias_ref[0, g * SWA_KV_HEADS + h] = bias
                bias_ref[1, g * SWA_KV_HEADS + h] = jnp.where(kj < Q, NEG_INF, bias)

    lane_kv0 = lax.broadcasted_iota(jnp.int32, (Q, LANES), 1) < HEAD_DIM
    scale = HEAD_DIM ** -0.5
    blocks = q_ref.shape[0] // Q
    k_all = jnp.concatenate([kp_ref[...], kc_ref[...]], axis=0).astype(BF16)
    v_all = jnp.concatenate([vp_ref[...], vc_ref[...]], axis=0).astype(BF16)
    for j in range(blocks):
        rows = slice(j * Q, (j + 1) * Q)
        table = jnp.where(pl.program_id(1) > 0, 0, 1) if j == 0 else 0
        kcat = k_all[j * Q:(j + 2) * Q]
        vcat = v_all[j * Q:(j + 2) * Q]
        for g in range(SWA_GROUP):
            qg = q_ref[rows, g * LANES:(g + 1) * LANES] * scale
            outs = []
            for h in range(SWA_KV_HEADS):
                qm = jnp.where(lane_kv0, qg, 0.0) if h == 0 else jnp.where(lane_kv0, 0.0, qg)
                s = _dot(qm, kcat, NT) + bias_ref[table, g * SWA_KV_HEADS + h]
                sink = sink_ref[h * SWA_GROUP + g]
                m = jnp.maximum(jnp.max(s, axis=-1, keepdims=True), sink)
                p = jnp.exp(s - m)
                den = jnp.sum(p, axis=-1, keepdims=True) + jnp.exp(sink - m)
                outs.append(_dot(p, vcat) / den)
            o_ref[rows, g * LANES:(g + 1) * LANES] = jnp.where(lane_kv0, outs[0], outs[1])


def _swa_prompt(q, k, v, rel_bias, sinks, batch, seq):
    N = q.shape[0]
    Q = WINDOW
    per_step = _pick_tile(seq // Q, 8)
    nb = seq // (Q * per_step)
    qi = np.arange(Q)[:, None]
    kj = np.arange(2 * Q)[None, :]
    idx = jnp.asarray(_rel_bucket_np(qi + Q - kj))
    cur = lambda bb, n: (bb * nb + n, 0)
    prev = lambda bb, n: (jnp.maximum((bb * nb + n) * per_step - 1, 0), 0)
    kv_c = pl.BlockSpec((per_step * Q, KV_WIDTH), cur)
    kv_p = pl.BlockSpec((Q, KV_WIDTH), prev)
    smem = pl.BlockSpec(memory_space=pltpu.SMEM)
    return pl.pallas_call(
        _swa_prompt_body,
        out_shape=jax.ShapeDtypeStruct((N, SWA_WIDTH), F32),
        grid=(batch, nb),
        in_specs=[smem, smem, pl.BlockSpec((Q, 2 * Q), lambda bb, n: (0, 0)),
                  pl.BlockSpec((per_step * Q, SWA_WIDTH), cur), kv_p, kv_c, kv_p, kv_c],
        out_specs=pl.BlockSpec((per_step * Q, SWA_WIDTH), cur),
        scratch_shapes=[pltpu.VMEM((2, SWA_HEADS, Q, 2 * Q), F32)],
        compiler_params=_params(("arbitrary", "arbitrary")),
        name="swa_prompt",
    )(rel_bias, sinks, idx, q, k, k, v, v)


def _swa_sample_body(rb_ref, sink_ref, idx_ref, q_ref, kn_ref, vn_ref, kb_ref, vb_ref, o_ref, ko_ref, vo_ref,
                     bias_ref, extra_ref):
    W = WINDOW
    G, KVH = SWA_GROUP, SWA_KV_HEADS

    @pl.when(pl.program_id(0) == 0)
    def _():
        idx = jnp.broadcast_to(idx_ref[...], (SUBLANES, W))
        row = lax.broadcasted_iota(jnp.int32, (SUBLANES, W), 0)
        acc = jnp.zeros((SUBLANES, W), F32)
        ext = jnp.zeros((SUBLANES, LANES), F32)
        lane = lax.broadcasted_iota(jnp.int32, (SUBLANES, LANES), 1)
        for h in range(KVH):
            for g in range(G):
                head = h * G + g
                r = h * G + g
                acc = jnp.where(row == r, _bias_from_buckets(idx, rb_ref, head), acc)
                ext = jnp.where((row == r) & (lane == 0), rb_ref[0, head], ext)
                ext = jnp.where((row == r) & (lane == 1), sink_ref[head], ext)
        bias_ref[...] = acc
        extra_ref[...] = ext

    TB = q_ref.shape[0]
    lane_kv0 = lax.broadcasted_iota(jnp.int32, (TB, G, LANES), 2) < HEAD_DIM
    q4 = q_ref[...]
    qrows = jnp.concatenate([jnp.where(lane_kv0, q4, 0.0), jnp.where(lane_kv0, 0.0, q4)], axis=1)
    kb = kb_ref[...]
    vb = vb_ref[...]
    kn = kn_ref[...]
    vn = vn_ref[...]
    scale = HEAD_DIM ** -0.5
    bdims = (((2,), (2,)), ((0,), (0,)))
    s = lax.dot_general(qrows.astype(BF16), kb.astype(BF16), bdims, preferred_element_type=F32) * scale
    s = s + bias_ref[...][None]
    s_self = jnp.sum(qrows * kn, axis=-1, keepdims=True) * scale + extra_ref[:, 0:1][None]
    sink = extra_ref[:, 1:2][None]
    m = jnp.maximum(jnp.maximum(jnp.max(s, axis=-1, keepdims=True), s_self), sink)
    p = jnp.exp(s - m)
    p_self = jnp.exp(s_self - m)
    den = jnp.sum(p, axis=-1, keepdims=True) + p_self + jnp.exp(sink - m)
    pv = lax.dot_general(p.astype(BF16), vb.astype(BF16), (((2,), (1,)), ((0,), (0,))), preferred_element_type=F32)
    o = (pv + p_self * vn) / den
    o_ref[...] = jnp.where(lane_kv0, o[:, 0:G], o[:, G:2 * G])

    rowmod = lax.broadcasted_iota(jnp.int32, (TB, W, KV_WIDTH), 1)
    for buf, new, out in ((kb, kn, ko_ref), (vb, vn, vo_ref)):
        rolled = pltpu.roll(buf.reshape(TB * W, KV_WIDTH), TB * W - 1, 0).reshape(TB, W, KV_WIDTH)
        out[...] = jnp.where(rowmod == W - 1, new, rolled)


def _swa_sample(q, k, v, kbuf, vbuf, rel_bias, sinks):
    B = q.shape[0]
    W = WINDOW
    tb = 16
    idx = jnp.asarray(_rel_bucket_np(W - np.arange(W))[None, :])
    q4 = q.reshape(B, SWA_GROUP, LANES)
    kn = k.reshape(B, 1, KV_WIDTH)
    vn = v.reshape(B, 1, KV_WIDTH)
    kb = kbuf.reshape(B, W, KV_WIDTH)
    vb = vbuf.reshape(B, W, KV_WIDTH)
    smem = pl.BlockSpec(memory_space=pltpu.SMEM)
    b3 = lambda i: (i, 0, 0)
    o, ko, vo = pl.pallas_call(
        _swa_sample_body,
        out_shape=(jax.ShapeDtypeStruct((B, SWA_GROUP, LANES), F32), jax.ShapeDtypeStruct((B, W, KV_WIDTH), F32),
                   jax.ShapeDtypeStruct((B, W, KV_WIDTH), F32)),
        grid=(B // tb,),
        in_specs=[smem, smem, pl.BlockSpec((1, W), lambda i: (0, 0)),
                  pl.BlockSpec((tb, SWA_GROUP, LANES), b3), pl.BlockSpec((tb, 1, KV_WIDTH), b3),
                  pl.BlockSpec((tb, 1, KV_WIDTH), b3), pl.BlockSpec((tb, W, KV_WIDTH), b3),
                  pl.BlockSpec((tb, W, KV_WIDTH), b3)],
        out_specs=(pl.BlockSpec((tb, SWA_GROUP, LANES), b3), pl.BlockSpec((tb, W, KV_WIDTH), b3),
                   pl.BlockSpec((tb, W, KV_WIDTH), b3)),
        scratch_shapes=[pltpu.VMEM((SUBLANES, W), F32), pltpu.VMEM((SUBLANES, LANES), F32)],
        compiler_params=_params(("arbitrary",)),
        name="swa_sample",
    )(rel_bias, sinks, idx, q4, kn, vn, kb, vb)
    return o.reshape(B, SWA_WIDTH), ko.reshape(kbuf.shape), vo.reshape(vbuf.shape)


def _outproj_body(per_row, tiles_per_seq, prompt_row0, x_ref, mod_ref, yw_ref, g_ref, bonus_ref, ya_ref,
                  lnw_ref, lnb_ref, hm_ref, wr_ref, wa_ref, gpost_ref, o_ref):
    D = D_MODEL
    m = _mod_rows(mod_ref, per_row, tiles_per_seq, prompt_row0)
    yr = _rwkv_post(yw_ref[...], g_ref[...], bonus_ref[...], lnw_ref[...], lnb_ref[...], hm_ref[...])
    mix = _dot(yr, wr_ref[...]) + _dot(ya_ref[...], wa_ref[...])
    o_ref[...] = x_ref[...] + m[:, 2 * D:3 * D] * _rms(mix, gpost_ref[...])


def _outproj(x, mod, yw, g, bonus, ya, lw, per_row, tm, tiles_per_seq, prompt_row0):
    N, D = x.shape
    W = RWKV_WIDTH
    row = lambda i: (i, 0)
    fixed = lambda i: (0, 0)
    half = pl.BlockSpec((tm, W), row)
    vec = pl.BlockSpec((1, W), fixed)
    return pl.pallas_call(
        functools.partial(_outproj_body, per_row, tiles_per_seq, prompt_row0),
        out_shape=jax.ShapeDtypeStruct((N, D), F32),
        grid=(N // tm,),
        in_specs=[pl.BlockSpec((tm, D), row), _mod_spec(mod, per_row, tm, 1), half, half, half, half, vec, vec,
                  pl.BlockSpec((W, W), fixed), pl.BlockSpec((W, D), fixed), pl.BlockSpec((W, D), fixed),
                  pl.BlockSpec((1, D), fixed)],
        out_specs=pl.BlockSpec((tm, D), row),
        compiler_params=_params(("arbitrary",)),
        name="out_proj",
    )(x, mod, yw, g, bonus, ya, lw["lnw"], lw["lnb"], lw["headmean"], lw["w_out_r"], lw["w_out_a"], lw["g_post0"])


def _ffn_body(moe, per_row, tiles_per_seq, prompt_row0, x_ref, mod_ref, gpre_ref, gpost_ref, rw_ref, rb_ref,
              wg_ref, wu_ref, wd_ref, o_ref, h_ref, acc_ref, comb_ref):
    D = D_MODEL
    e = pl.program_id(1)
    f = pl.program_id(2)
    first = (e == 0) & (f == 0)
    last = (e == pl.num_programs(1) - 1) & (f == pl.num_programs(2) - 1)

    @pl.when(first)
    def _():
        m = _mod_rows(mod_ref, per_row, tiles_per_seq, prompt_row0)
        h = _rms(x_ref[...], gpre_ref[...]) * (1.0 + m[:, D:2 * D]) + m[:, 0:D]
        h_ref[...] = h.astype(BF16)
        acc_ref[...] = jnp.zeros_like(acc_ref)
        if moe:
            logits = _dot3(h, rw_ref[...]) + rb_ref[...]
            lane = lax.broadcasted_iota(jnp.int32, logits.shape, 1)
            m1 = jnp.max(logits, axis=-1, keepdims=True)
            i1 = jnp.min(jnp.where(logits == m1, lane, LANES), axis=-1, keepdims=True)
            rest = jnp.where(lane == i1, -jnp.inf, logits)
            m2 = jnp.max(rest, axis=-1, keepdims=True)
            i2 = jnp.min(jnp.where(rest == m2, lane, LANES), axis=-1, keepdims=True)
            e2 = jnp.exp(m2 - m1)
            comb_ref[...] = jnp.where(lane == i1, 1.0 / (1.0 + e2), 0.0) + jnp.where(lane == i2, e2 / (1.0 + e2), 0.0)

    h = h_ref[...]
    gate = _dot(h, wg_ref[...])
    up = _dot(h, wu_ref[...])
    act = gate * _sigmoid(gate) * up
    if moe:
        comb = comb_ref[...]
        lane = lax.broadcasted_iota(jnp.int32, comb.shape, 1)
        act = act * jnp.sum(jnp.where(lane == e, comb, 0.0), axis=-1, keepdims=True)
    acc_ref[...] += _dot(act, wd_ref[...])

    @pl.when(last)
    def _():
        m = _mod_rows(mod_ref, per_row, tiles_per_seq, prompt_row0)
        o_ref[...] = x_ref[...] + m[:, 2 * D:3 * D] * _rms(acc_ref[...], gpost_ref[...])


def _ffn(x, mod, gpre, gpost, router_w, router_b, wg, wu, wd, w0, E, moe, per_row, tm, tiles_per_seq, prompt_row0,
         tf):
    N, D = x.shape
    F = wg.shape[-1]
    row = lambda i, e, f: (i, 0)
    fixed = lambda i, e, f: (0, 0)
    return pl.pallas_call(
        functools.partial(_ffn_body, moe, per_row, tiles_per_seq, prompt_row0),
        out_shape=jax.ShapeDtypeStruct((N, D), F32),
        grid=(N // tm, E, F // tf),
        in_specs=[pl.BlockSpec((tm, D), row), _mod_spec(mod, per_row, tm, 3),
                  pl.BlockSpec((1, D), fixed), pl.BlockSpec((1, D), fixed),
                  pl.BlockSpec((D, LANES), fixed), pl.BlockSpec((1, LANES), fixed),
                  pl.BlockSpec((None, D, tf), lambda i, e, f: (w0 + e, 0, f)),
                  pl.BlockSpec((None, D, tf), lambda i, e, f: (w0 + e, 0, f)),
                  pl.BlockSpec((None, tf, D), lambda i, e, f: (w0 + e, f, 0))],
        out_specs=pl.BlockSpec((tm, D), row),
        scratch_shapes=[pltpu.VMEM((tm, D), BF16), pltpu.VMEM((tm, D), F32), pltpu.VMEM((tm, LANES), F32)],
        compiler_params=_params(("arbitrary", "arbitrary", "arbitrary")),
        name="moe_ffn" if moe else "dense_ffn",
    )(x, mod, gpre, gpost, router_w, router_b, wg, wu, wd)


ROW_TILE = D_MODEL // LANES
EXPERT_TILE = 512
INFO_E1, INFO_E2, INFO_P1, INFO_P2, INFO_POS1, INFO_POS2 = range(6)


def _top2(logits):
    lane = lax.broadcasted_iota(jnp.int32, logits.shape, 1)
    m1 = jnp.max(logits, axis=-1, keepdims=True)
    i1 = jnp.min(jnp.where(logits == m1, lane, LANES), axis=-1, keepdims=True)
    rest = jnp.where(lane == i1, -jnp.inf, logits)
    m2 = jnp.max(rest, axis=-1, keepdims=True)
    i2 = jnp.min(jnp.where(rest == m2, lane, LANES), axis=-1, keepdims=True)
    e2 = jnp.exp(m2 - m1)
    return lane, i1, i2, 1.0 / (1.0 + e2), e2 / (1.0 + e2)


def _to_row_tiles(ref, x):
    rows = x.shape[0]
    for c in range(ROW_TILE):
        ref[pl.ds(c, rows, stride=ROW_TILE), :] = x[:, c * LANES:(c + 1) * LANES]


def _from_row_tiles(ref, row0, rows):
    return jnp.concatenate([ref[pl.ds(row0 * ROW_TILE + c, rows, stride=ROW_TILE), :] for c in range(ROW_TILE)],
                           axis=1)


def _route_body(tiles_per_seq, prompt_row0, x_ref, mod_ref, gpre_ref, rw_ref, rb_ref, h_ref, info_ref, info_t_ref,
                cnt_ref, base_ref):
    D = D_MODEL

    @pl.when(pl.program_id(0) == 0)
    def _():
        base_ref[...] = jnp.zeros_like(base_ref)

    m = _mod_rows(mod_ref, False, tiles_per_seq, prompt_row0)
    h = _rms(x_ref[...], gpre_ref[...]) * (1.0 + m[:, D:2 * D]) + m[:, 0:D]
    _to_row_tiles(h_ref, h)
    lane, i1, i2, p1, p2 = _top2(_dot3(h, rw_ref[...]) + rb_ref[...])
    onehot = jnp.where((lane == i1) | (lane == i2), 1.0, 0.0)
    T = h.shape[0]
    before = lax.broadcasted_iota(jnp.int32, (T, T), 1) < lax.broadcasted_iota(jnp.int32, (T, T), 0)
    rank = _dot(jnp.where(before, 1.0, 0.0), onehot) + base_ref[0:1, :]
    pos1 = jnp.sum(jnp.where(lane == i1, rank, 0.0), axis=-1, keepdims=True)
    pos2 = jnp.sum(jnp.where(lane == i2, rank, 0.0), axis=-1, keepdims=True)
    info = jnp.zeros(onehot.shape, F32)
    for col, val in ((INFO_E1, i1.astype(F32)), (INFO_E2, i2.astype(F32)), (INFO_P1, p1), (INFO_P2, p2),
                     (INFO_POS1, pos1), (INFO_POS2, pos2)):
        info = jnp.where(lane == col, val, info)
    info_ref[...] = info
    info_t_ref[...] = info.T[0:SUBLANES, :]
    total = base_ref[...] + jnp.sum(onehot, axis=0, keepdims=True)
    base_ref[...] = total
    cnt_ref[...] = total


def _moe_route(x, mod, gpre, rw, rb, tm, tiles_per_seq, prompt_row0):
    N, D = x.shape
    row = lambda i: (i, 0)
    fixed = lambda i: (0, 0)
    return pl.pallas_call(
        functools.partial(_route_body, tiles_per_seq, prompt_row0),
        out_shape=(jax.ShapeDtypeStruct((N * ROW_TILE, LANES), F32), jax.ShapeDtypeStruct((N, LANES), F32),
                   jax.ShapeDtypeStruct((SUBLANES, N), F32), jax.ShapeDtypeStruct((SUBLANES, LANES), F32)),
        grid=(N // tm,),
        in_specs=[pl.BlockSpec((tm, D), row), _mod_spec(mod, False, tm, 1), pl.BlockSpec((1, D), fixed),
                  pl.BlockSpec((D, LANES), fixed), pl.BlockSpec((1, LANES), fixed)],
        out_specs=(pl.BlockSpec((tm * ROW_TILE, LANES), row), pl.BlockSpec((tm, LANES), row),
                   pl.BlockSpec((SUBLANES, tm), lambda i: (0, i)), pl.BlockSpec((SUBLANES, LANES), fixed)),
        scratch_shapes=[pltpu.VMEM((SUBLANES, LANES), F32)],
        compiler_params=_params(("arbitrary",)),
        name="moe_route",
    )(x, mod, gpre, rw, rb)


def _invert_body(n_tokens, dest_ref, src_ref):
    def clear(i, carry):
        src_ref[i] = 0
        return carry
    lax.fori_loop(0, src_ref.shape[0], clear, 0, unroll=16)

    for choice in range(2):
        def put(t, carry):
            src_ref[dest_ref[choice * n_tokens + t]] = t
            return carry
        lax.fori_loop(0, n_tokens, put, 0, unroll=16)


def _moe_invert(dest12, n_rows):
    smem = pl.BlockSpec(memory_space=pltpu.SMEM)
    return pl.pallas_call(
        functools.partial(_invert_body, dest12.shape[0] // 2),
        out_shape=jax.ShapeDtypeStruct((n_rows,), jnp.int32),
        in_specs=[smem],
        out_specs=smem,
        name="moe_invert",
    )(dest12)


def _row_copy(src_hbm, dst_vmem, sem, src_row, dst_row):
    return pltpu.make_async_copy(src_hbm.at[pl.ds(pl.multiple_of(src_row * ROW_TILE, ROW_TILE), ROW_TILE)],
                                 dst_vmem.at[pl.ds(pl.multiple_of(dst_row * ROW_TILE, ROW_TILE), ROW_TILE)], sem)


def _start_rows(idx_ref, idx0, src_hbm, dst_vmem, sem, rows):
    def body(r, carry):
        _row_copy(src_hbm, dst_vmem, sem, idx_ref[idx0 + r], r).start()
        return carry
    lax.fori_loop(0, rows, body, 0, unroll=16)


def _wait_rows(src_hbm, dst_vmem, sem, rows):
    pltpu.make_async_copy(src_hbm.at[pl.ds(0, rows * ROW_TILE)], dst_vmem.at[pl.ds(0, rows * ROW_TILE)], sem).wait()


def _experts_body(te_ref, nv_ref, src_ref, h_hbm, wg_ref, wu_ref, wd_ref, o_ref, xbuf, hb_ref, acc_ref, sem):
    TM = EXPERT_TILE
    i = pl.program_id(0)
    f = pl.program_id(1)
    slot = i % 2
    n_valid = nv_ref[0]
    valid = i < n_valid

    @pl.when(f == 0)
    def _():
        @pl.when((i == 0) & valid)
        def _():
            _start_rows(src_ref, 0, h_hbm, xbuf.at[0], sem.at[0], TM)

        @pl.when(i + 1 < n_valid)
        def _():
            _start_rows(src_ref, (i + 1) * TM, h_hbm, xbuf.at[1 - slot], sem.at[1 - slot], TM)

        @pl.when(valid)
        def _():
            _wait_rows(h_hbm, xbuf.at[slot], sem.at[slot], TM)
            hb_ref[...] = _from_row_tiles(xbuf.at[slot], 0, TM).astype(BF16)
            acc_ref[...] = jnp.zeros_like(acc_ref)

    @pl.when(valid)
    def _():
        h = hb_ref[...]
        gate = _dot(h, wg_ref[...])
        up = _dot(h, wu_ref[...])
        acc_ref[...] += _dot(gate * _sigmoid(gate) * up, wd_ref[...])

    @pl.when(f == pl.num_programs(1) - 1)
    def _():
        @pl.when(valid)
        def _():
            _to_row_tiles(o_ref, acc_ref[...])

        @pl.when(jnp.logical_not(valid))
        def _():
            o_ref[...] = jnp.zeros_like(o_ref)


def _moe_experts(h_rows, tile_expert, n_valid, src_tok, wg, wu, wd, tf):
    TM = EXPERT_TILE
    P = src_tok.shape[0]
    _, D, F = wg.shape
    grid_spec = pltpu.PrefetchScalarGridSpec(
        num_scalar_prefetch=3,
        grid=(P // TM, F // tf),
        in_specs=[pl.BlockSpec(memory_space=pl.ANY),
                  pl.BlockSpec((None, D, tf), lambda i, f, te, nv, src: (te[i], 0, f)),
                  pl.BlockSpec((None, D, tf), lambda i, f, te, nv, src: (te[i], 0, f)),
                  pl.BlockSpec((None, tf, D), lambda i, f, te, nv, src: (te[i], f, 0))],
        out_specs=pl.BlockSpec((TM * ROW_TILE, LANES), lambda i, f, te, nv, src: (i, 0)),
        scratch_shapes=[pltpu.VMEM((2, TM * ROW_TILE, LANES), F32), pltpu.VMEM((TM, D), BF16),
                        pltpu.VMEM((TM, D), F32), pltpu.SemaphoreType.DMA((2,))],
    )
    return pl.pallas_call(
        _experts_body,
        out_shape=jax.ShapeDtypeStruct((P * ROW_TILE, LANES), F32),
        grid_spec=grid_spec,
        compiler_params=_params(("arbitrary", "arbitrary")),
        name="moe_experts",
    )(tile_expert, n_valid, src_tok, h_rows, wg, wu, wd)


def _combine_body(tiles_per_seq, prompt_row0, dest_ref, f_hbm, x_ref, mod_ref, info_ref, gpost_ref, o_ref, gbuf, sem):
    D = D_MODEL
    i = pl.program_id(0)
    T = x_ref.shape[0]
    slot = i % 2

    @pl.when(i == 0)
    def _():
        _start_rows(dest_ref, 0, f_hbm, gbuf.at[0], sem.at[0], 2 * T)

    @pl.when(i + 1 < pl.num_programs(0))
    def _():
        _start_rows(dest_ref, (i + 1) * 2 * T, f_hbm, gbuf.at[1 - slot], sem.at[1 - slot], 2 * T)

    _wait_rows(f_hbm, gbuf.at[slot], sem.at[slot], 2 * T)
    info = info_ref[...]
    f1 = _from_row_tiles(gbuf.at[slot], 0, T)
    f2 = _from_row_tiles(gbuf.at[slot], T, T)
    y = info[:, INFO_P1:INFO_P1 + 1] * f1 + info[:, INFO_P2:INFO_P2 + 1] * f2
    m = _mod_rows(mod_ref, False, tiles_per_seq, prompt_row0)
    o_ref[...] = x_ref[...] + m[:, 2 * D:3 * D] * _rms(y, gpost_ref[...])


def _moe_combine(dest, f_rows, x, mod, info, gpost, tm, tiles_per_seq, prompt_row0):
    N, D = x.shape
    row = lambda i, d: (i, 0)
    fixed = lambda i, d: (0, 0)
    grid_spec = pltpu.PrefetchScalarGridSpec(
        num_scalar_prefetch=1,
        grid=(N // tm,),
        in_specs=[pl.BlockSpec(memory_space=pl.ANY), pl.BlockSpec((tm, D), row),
                  pl.BlockSpec(mod.shape, fixed), pl.BlockSpec((tm, LANES), row), pl.BlockSpec((1, D), fixed)],
        out_specs=pl.BlockSpec((tm, D), row),
        scratch_shapes=[pltpu.VMEM((2, 2 * tm * ROW_TILE, LANES), F32), pltpu.SemaphoreType.DMA((2,))],
    )
    return pl.pallas_call(
        functools.partial(_combine_body, tiles_per_seq, prompt_row0),
        out_shape=jax.ShapeDtypeStruct((N, D), F32),
        grid_spec=grid_spec,
        compiler_params=_params(("arbitrary",)),
        name="moe_combine",
    )(dest, f_rows, x, mod, info, gpost)


def _moe_prompt(x, mod, gpre, gpost, fw, tm, tiles_per_seq, prompt_row0, tf):
    N = x.shape[0]
    TM = EXPERT_TILE
    n_tiles = (2 * N) // TM + N_EXPERTS
    h_rows, info, info_t, cnt = _moe_route(x, mod, gpre, fw["rw"], fw["rb"], tm, tiles_per_seq, prompt_row0)
    fields = info_t.astype(jnp.int32)
    e1, e2 = fields[INFO_E1], fields[INFO_E2]
    counts = cnt[0, :N_EXPERTS].astype(jnp.int32)
    padded = ((counts + TM - 1) // TM) * TM
    ends = jnp.cumsum(padded)
    starts = ends - padded
    dest1 = starts[e1] + fields[INFO_POS1]
    dest2 = starts[e2] + fields[INFO_POS2]
    src_tok = _moe_invert(jnp.concatenate([dest1, dest2]), n_tiles * TM)
    tile_start = jnp.arange(n_tiles, dtype=jnp.int32) * TM
    tile_expert = jnp.minimum(jnp.sum(ends[None, :] <= tile_start[:, None], axis=1), N_EXPERTS - 1).astype(jnp.int32)
    n_valid = (ends[-1:] // TM).astype(jnp.int32)
    f_rows = _moe_experts(h_rows, tile_expert + fw["w0"], n_valid, src_tok, fw["wg"], fw["wu"], fw["wd"], tf)
    dest = jnp.concatenate([dest1.reshape(-1, 1, tm), dest2.reshape(-1, 1, tm)], axis=1).reshape(-1)
    return _moe_combine(dest, f_rows, x, mod, info, gpost, tm, tiles_per_seq, prompt_row0)


def _layer_weights(p, l):
    W = RWKV_WIDTH
    heads = np.arange(W) // HEAD_DIM
    headsum = jnp.asarray((heads[:, None] == heads[None, :]).astype(np.float32))
    perm = _q_perm()
    w_in = p["w_in"][l]
    w_in = jnp.concatenate([w_in[:, :RWKV_COLS], w_in[:, RWKV_COLS + perm], w_in[:, RWKV_COLS + SWA_WIDTH:]], axis=1)
    w_out = p["w_out"][l]
    zeros = jnp.zeros((DECAY_LORA, W), F32)
    row = lambda t: t.reshape(1, -1)
    return {
        "w_in": w_in.astype(BF16),
        "w_out_r": w_out[:W].astype(BF16),
        "w_out_a": w_out[W + perm].astype(BF16),
        "mu": row(p["mu_shift"][l]),
        "wd": jnp.concatenate([p["w_decay_up"][l], zeros], axis=0),
        "wi": jnp.concatenate([zeros, p["w_iclr_up"][l]], axis=0),
        "wg": p["w_gate_up"][l],
        "dbase": row(p["decay_base"][l]), "ibase": row(p["iclr_base"][l]),
        "kk": row(p["k_k"][l]), "ka": row(p["k_a"][l]), "rk": row(p["r_k"][l]),
        "lnw": row(p["lnx_w"][l]), "lnb": row(p["lnx_b"][l]),
        "headsum": headsum.astype(BF16), "headmean": (headsum / HEAD_DIM).astype(BF16),
        "g_pre0": row(p["norm_pre"][l, 0]), "g_pre1": row(p["norm_pre"][l, 1]),
        "g_post0": row(p["norm_post"][l, 0]), "g_post1": row(p["norm_post"][l, 1]),
    }


def _stacked(w, dtype):
    return w.astype(dtype).reshape((-1,) + w.shape[-2:])


def _ffn_weights(p, stacks, l):
    i = l // 2
    if l % 2 == 0:
        wg, wu, wd = stacks["dense"]
        return dict(moe=False, rw=jnp.zeros((D_MODEL, LANES), F32), rb=jnp.zeros((1, LANES), F32),
                    wg=wg, wu=wu, wd=wd, w0=i, n=1)
    rw = jnp.zeros((D_MODEL, LANES), F32).at[:, :N_EXPERTS].set(p["router_w"][i])
    rb = jnp.full((1, LANES), NEG_INF, F32).at[0, :N_EXPERTS].set(p["router_b"][i])
    wg, wu, wd = stacks["moe"]
    return dict(moe=True, rw=rw, rb=rb, wg=wg, wu=wu, wd=wd, w0=i * N_EXPERTS, n=N_EXPERTS)


def _pick_tile(n, pref):
    t = min(pref, n)
    while n % t:
        t //= 2
    return t


def _ffn_tile(f):
    for t in (1408, 896, 512, 256, 128):
        if f % t == 0:
            return t
    return f


def _trunk(x3, mods, lws, fws, p, prompt, prompt_row0, state=None):
    B, T, D = x3.shape
    N = B * T
    x = x3.reshape(N, D)
    per_row = not prompt
    tm = _pick_tile(T if prompt else N, 512)
    tps = (T // tm) if prompt else 1
    depth = len(lws)
    wkv_out, shift_out, k_out, v_out = [], [], [], []
    for l in range(depth):
        lw, fw = lws[l], fws[l]
        mod0, mod1 = mods[2 * l], mods[2 * l + 1]
        pr, q, k, v = _inproj(x, mod0, lw["g_pre0"], lw["w_in"], per_row, tm, tps, prompt_row0)
        if prompt:
            r, ld, kh, vv, a, b, g, bonus = _prep_prompt(pr, lw, tm, tps)
            yw, hbd = _wkv_prompt(r, ld, kh, vv, a, b, B, T)
            n_pairs = RWKV_WIDTH // PAIR
            hb = hbd.reshape(B, n_pairs, 2, HEAD_DIM, 2, HEAD_DIM)
            s_kv = jnp.stack([hb[:, :, 0, :, 0, :], hb[:, :, 1, :, 1, :]], axis=2)
            s_new = jnp.swapaxes(s_kv.reshape(B, RWKV_HEADS, HEAD_DIM, HEAD_DIM), -1, -2)
            ya = _swa_prompt(q, k, v, p["rel_bias"], p["attn_sinks"][l], B, T)
            window = lambda t: t.reshape(B, T, KV_WIDTH)[:, -WINDOW:].reshape(B, WINDOW, SWA_KV_HEADS, HEAD_DIM)
            kb, vb = window(k), window(v)
            last = pr.reshape(B, T, RWKV_COLS)[:, -1]
        else:
            r, ld, kh, vv, a, b, g, bonus = _prep_sample(pr, state["shift"][l], lw)
            yw, s_new = _wkv_sample(state["wkv"][l], r, ld, kh, vv, a, b)
            ya, kb, vb = _swa_sample(q, k, v, state["k"][l], state["v"][l], p["rel_bias"], p["attn_sinks"][l])
            last = pr
        x = _outproj(x, mod0, yw, g, bonus, ya, lw, per_row, tm, tps, prompt_row0)
        tf = _ffn_tile(fw["wg"].shape[-1])
        if prompt and fw["moe"]:
            x = _moe_prompt(x, mod1, lw["g_pre1"], lw["g_post1"], fw, tm, tps, prompt_row0, 2 * tf)
        else:
            x = _ffn(x, mod1, lw["g_pre1"], lw["g_post1"], fw["rw"], fw["rb"], fw["wg"], fw["wu"], fw["wd"],
                     fw["w0"], fw["n"], fw["moe"], per_row, tm, tps, prompt_row0, tf)
        wkv_out.append(s_new)
        shift_out.append(last)
        k_out.append(kb)
        v_out.append(vb)
    return x.reshape(B, T, D), jnp.stack(wkv_out), jnp.stack(shift_out), jnp.stack(k_out), jnp.stack(v_out)


def _forward(x_prompt, x_sample, c_prompt, c_sample, state_wkv, state_shift, cache_swa_k, cache_swa_v, p):
    depth = p["w_in"].shape[0]
    Bp, Bs = c_prompt.shape[0], c_sample.shape[0]
    D = D_MODEL
    pad = (-(Bs + Bp)) % SUBLANES
    c_all = jnp.concatenate([c_sample, c_prompt, jnp.zeros((pad, D), F32)], axis=0)
    mods = _ada_all(c_all, p["ada_w"].reshape(2 * depth, D, 3 * D), p["ada_b"].reshape(2 * depth, 1, 3 * D))
    lws = [_layer_weights(p, l) for l in range(depth)]
    stacks = {"dense": tuple(_stacked(p[k], BF16) for k in ("ffn_w_gate", "ffn_w_up", "ffn_w_down")),
              "moe": tuple(_stacked(p[k], BF16) for k in ("moe_w_gate", "moe_w_up", "moe_w_down"))}
    fws = [_ffn_weights(p, stacks, l) for l in range(depth)]
    y_p, wkv_p, shift_p, k_p, v_p = _trunk(x_prompt, mods, lws, fws, p, True, Bs)
    state = {"wkv": state_wkv, "shift": state_shift, "k": cache_swa_k, "v": cache_swa_v}
    y_s, wkv_s, shift_s, k_s, v_s = _trunk(x_sample, mods, lws, fws, p, False, Bs, state)
    return (y_p, y_s, wkv_p, shift_p, k_p, v_p, wkv_s, shift_s, k_s, v_s)


def kernel(x_prompt, x_sample, c_prompt, c_sample, state_wkv, state_shift, cache_swa_k, cache_swa_v, rel_bias, ada_w, ada_b, norm_pre, norm_post, w_in, mu_shift, w_decay_up, decay_base, w_iclr_up, iclr_base, w_gate_up, k_k, k_a, r_k, lnx_w, lnx_b, attn_sinks, w_out, ffn_w_gate, ffn_w_up, ffn_w_down, router_w, router_b, moe_w_gate, moe_w_up, moe_w_down):
    p = {"rel_bias": rel_bias, "ada_w": ada_w, "ada_b": ada_b, "norm_pre": norm_pre, "norm_post": norm_post,
         "w_in": w_in, "mu_shift": mu_shift, "w_decay_up": w_decay_up, "decay_base": decay_base,
         "w_iclr_up": w_iclr_up, "iclr_base": iclr_base, "w_gate_up": w_gate_up, "k_k": k_k, "k_a": k_a,
         "r_k": r_k.reshape(r_k.shape[0], -1), "lnx_w": lnx_w, "lnx_b": lnx_b, "attn_sinks": attn_sinks,
         "w_out": w_out, "ffn_w_gate": ffn_w_gate, "ffn_w_up": ffn_w_up, "ffn_w_down": ffn_w_down,
         "router_w": router_w, "router_b": router_b, "moe_w_gate": moe_w_gate, "moe_w_up": moe_w_up,
         "moe_w_down": moe_w_down}
    return _forward(x_prompt, x_sample, c_prompt, c_sample, state_wkv, state_shift, cache_swa_k, cache_swa_v, p)
```
